```python
import jax, jax.numpy as jnp
from jax import lax
import numpy as np

D_MODEL = 1024
BATCH = 8
SEQ = 8192
DEPTH = 4

N_MIXERS = 3
CONF_CONV_WIDTH = 31
POOL_WINDOWS = (2, 4, 8, 16)
POOL_GROUPS = len(POOL_WINDOWS)
POOL_GROUP_DIM = D_MODEL // POOL_GROUPS
SHORT_CONV_WIDTH = 3
FFN_CONV_WIDTH = 3
D_FF = 2816
RMS_EPS = 1e-6
LN_EPS = 1e-5
N_A = (DEPTH + N_MIXERS - 1) // N_MIXERS
N_B = (DEPTH + N_MIXERS - 2) // N_MIXERS
N_C = (DEPTH + N_MIXERS - 3) // N_MIXERS

kernel_name = "hybrid_interleaved_conv_pool_shortconv_trunk"


def rmsnorm(x, g):
    x32 = x.astype(jnp.float32)
    y = x32 * lax.rsqrt(jnp.mean(x32 * x32, axis=-1, keepdims=True) + RMS_EPS)
    return y.astype(x.dtype) * g


def layernorm(x, g, b):
    x32 = x.astype(jnp.float32)
    mu = jnp.mean(x32, axis=-1, keepdims=True)
    var = jnp.mean(jnp.square(x32 - mu), axis=-1, keepdims=True)
    y = (x32 - mu) * lax.rsqrt(var + LN_EPS)
    return y.astype(x.dtype) * g + b


def causal_dwconv(x, w):
    k, ch = w.shape
    return lax.conv_general_dilated(
        x, w[:, None, :].astype(x.dtype), window_strides=(1,), padding=[(k - 1, 0)],
        dimension_numbers=("NWC", "WIO", "NWC"), feature_group_count=ch)


def conformer_conv_module(h, w1, b1, dw_w, dw_b, ln_g, ln_b, w2, b2):
    u = h @ w1 + b1
    a, gt = jnp.split(u, 2, axis=-1)
    u = a * jax.nn.sigmoid(gt)
    u = causal_dwconv(u, dw_w) + dw_b
    u = jax.nn.silu(layernorm(u, ln_g, ln_b))
    return u @ w2 + b2


def pooling_mixer(h, group_w, group_b, ch_scale):
    bsz, t_len, d = h.shape
    h32 = h.astype(jnp.float32)
    cs = jnp.cumsum(h32, axis=1)
    pos = jnp.arange(1, t_len + 1, dtype=jnp.float32)[None, :, None]
    means = []
    for g, w in enumerate(POOL_WINDOWS):
        csg = cs[..., g * POOL_GROUP_DIM:(g + 1) * POOL_GROUP_DIM]
        prev = jnp.pad(csg, ((0, 0), (w, 0), (0, 0)))[:, :t_len]
        means.append((csg - prev) / jnp.minimum(pos, float(w)))
    pooled = (jnp.concatenate(means, axis=-1) - h32).astype(h.dtype)
    pooled = pooled.reshape(bsz, t_len, POOL_GROUPS, POOL_GROUP_DIM)
    mixed = jnp.einsum("btgc,gcd->btgd", pooled, group_w).reshape(bsz, t_len, d)
    return (mixed + group_b) * ch_scale


def short_conv_mixer(h, w_in, w_conv, w_out):
    bcx = h @ w_in
    gb, gc, v = jnp.split(bcx, 3, axis=-1)
    return (gb * causal_dwconv(gc * v, w_conv)) @ w_out


def conv_ffn(h, w_up, dw_w, dw_b, w_down):
    u = causal_dwconv(h @ w_up, dw_w) + dw_b
    gt, v = jnp.split(u, 2, axis=-1)
    return (jax.nn.silu(gt) * v) @ w_down


def _fwd_setup_inputs(seed: int = 0) -> dict:
    key = jax.random.key(seed)
    ks = iter(jax.random.split(key, 40))
    d, f = D_MODEL, D_FF

    def nrm(shape, scale):
        return scale * jax.random.normal(next(ks), shape, jnp.float32)

    def gain(shape):
        return 1.0 + nrm(shape, 0.05)

    return {
        "x": nrm((BATCH, SEQ, d), 1.0),
        "c": nrm((BATCH, d), 1.0),
        "mod_w": nrm((DEPTH, d, 6 * d), 0.5 * d ** -0.5),
        "mod_b": nrm((DEPTH, 6 * d), 0.02),
        "norm_pre_mix": gain((DEPTH, d)),
        "norm_post_mix": gain((DEPTH, d)),
        "norm_pre_ffn": gain((DEPTH, d)),
        "norm_post_ffn": gain((DEPTH, d)),
        "a_pw1_w": nrm((N_A, d, 2 * d), d ** -0.5),
        "a_pw1_b": nrm((N_A, 2 * d), 0.02),
        "a_dw_w": nrm((N_A, CONF_CONV_WIDTH, d), CONF_CONV_WIDTH ** -0.5),
        "a_dw_b": nrm((N_A, d), 0.02),
        "a_ln_g": gain((N_A, d)),
        "a_ln_b": nrm((N_A, d), 0.02),
        "a_pw2_w": nrm((N_A, d, d), d ** -0.5),
        "a_pw2_b": nrm((N_A, d), 0.02),
        "b_group_w": nrm((N_B, POOL_GROUPS, POOL_GROUP_DIM, POOL_GROUP_DIM), POOL_GROUP_DIM ** -0.5),
        "b_group_b": nrm((N_B, d), 0.02),
        "b_scale": gain((N_B, d)),
        "c_in_w": nrm((N_C, d, 3 * d), d ** -0.5),
        "c_conv_w": nrm((N_C, SHORT_CONV_WIDTH, d), SHORT_CONV_WIDTH ** -0.5),
        "c_out_w": nrm((N_C, d, d), d ** -0.5),
        "f_up_w": nrm((DEPTH, d, 2 * f), d ** -0.5),
        "f_dw_w": nrm((DEPTH, FFN_CONV_WIDTH, 2 * f), FFN_CONV_WIDTH ** -0.5),
        "f_dw_b": nrm((DEPTH, 2 * f), 0.02),
        "f_down_w": nrm((DEPTH, f, d), f ** -0.5),
    }


def _fwd_reference(x, c, mod_w, mod_b, norm_pre_mix, norm_post_mix, norm_pre_ffn, norm_post_ffn,
              a_pw1_w, a_pw1_b, a_dw_w, a_dw_b, a_ln_g, a_ln_b, a_pw2_w, a_pw2_b,
              b_group_w, b_group_b, b_scale,
              c_in_w, c_conv_w, c_out_w,
              f_up_w, f_dw_w, f_dw_b, f_down_w):
    c_act = jax.nn.silu(c)
    for i in range(DEPTH):
        mod = c_act @ mod_w[i] + mod_b[i]
        sh_m, sc_m, gt_m, sh_f, sc_f, gt_f = [m[:, None, :] for m in jnp.split(mod, 6, axis=-1)]

        h = rmsnorm(x, norm_pre_mix[i]) * (1.0 + sc_m) + sh_m
        kind, slot = i % N_MIXERS, i // N_MIXERS
        if kind == 0:
            y = conformer_conv_module(h, a_pw1_w[slot], a_pw1_b[slot], a_dw_w[slot], a_dw_b[slot],
                                      a_ln_g[slot], a_ln_b[slot], a_pw2_w[slot], a_pw2_b[slot])
        elif kind == 1:
            y = pooling_mixer(h, b_group_w[slot], b_group_b[slot], b_scale[slot])
        else:
            y = short_conv_mixer(h, c_in_w[slot], c_conv_w[slot], c_out_w[slot])
        x = x + gt_m * rmsnorm(y, norm_post_mix[i])

        h = rmsnorm(x, norm_pre_ffn[i]) * (1.0 + sc_f) + sh_f
        y = conv_ffn(h, f_up_w[i], f_dw_w[i], f_dw_b[i], f_down_w[i])
        x = x + gt_f * rmsnorm(y, norm_post_ffn[i])
    return x


import jax as _jax
import jax.numpy as _jnp

TWIN_FORMAT = 'train_step'
FWD_PARAMS = ['x', 'c', 'mod_w', 'mod_b', 'norm_pre_mix', 'norm_post_mix', 'norm_pre_ffn', 'norm_post_ffn', 'a_pw1_w', 'a_pw1_b', 'a_dw_w', 'a_dw_b', 'a_ln_g', 'a_ln_b', 'a_pw2_w', 'a_pw2_b', 'b_group_w', 'b_group_b', 'b_scale', 'c_in_w', 'c_conv_w', 'c_out_w', 'f_up_w', 'f_dw_w', 'f_dw_b', 'f_down_w']
TWIN_WEIGHTS = ['mod_w', 'mod_b', 'norm_pre_mix', 'norm_post_mix', 'norm_pre_ffn', 'norm_post_ffn', 'a_pw1_w', 'a_pw1_b', 'a_dw_w', 'a_dw_b', 'a_ln_g', 'a_ln_b', 'a_pw2_w', 'a_pw2_b', 'b_group_w', 'b_group_b', 'b_scale', 'c_in_w', 'c_conv_w', 'c_out_w', 'f_up_w', 'f_dw_w', 'f_dw_b', 'f_down_w']
TWIN_DIFF_INPUT = 'x'
TWIN_INPUTS = ['x', 'c', 'mod_w', 'mod_b', 'norm_pre_mix', 'norm_post_mix', 'norm_pre_ffn', 'norm_post_ffn', 'a_pw1_w', 'a_pw1_b', 'a_dw_w', 'a_dw_b', 'a_ln_g', 'a_ln_b', 'a_pw2_w', 'a_pw2_b', 'b_group_w', 'b_group_b', 'b_scale', 'c_in_w', 'c_conv_w', 'c_out_w', 'f_up_w', 'f_dw_w', 'f_dw_b', 'f_down_w', 'loss_target', 'm_mod_w', 'm_mod_b', 'm_norm_pre_mix', 'm_norm_post_mix', 'm_norm_pre_ffn', 'm_norm_post_ffn', 'm_a_pw1_w', 'm_a_pw1_b', 'm_a_dw_w', 'm_a_dw_b', 'm_a_ln_g', 'm_a_ln_b', 'm_a_pw2_w', 'm_a_pw2_b', 'm_b_group_w', 'm_b_group_b', 'm_b_scale', 'm_c_in_w', 'm_c_conv_w', 'm_c_out_w', 'm_f_up_w', 'm_f_dw_w', 'm_f_dw_b', 'm_f_down_w', 'v_mod_w', 'v_mod_b', 'v_norm_pre_mix', 'v_norm_post_mix', 'v_norm_pre_ffn', 'v_norm_post_ffn', 'v_a_pw1_w', 'v_a_pw1_b', 'v_a_dw_w', 'v_a_dw_b', 'v_a_ln_g', 'v_a_ln_b', 'v_a_pw2_w', 'v_a_pw2_b', 'v_b_group_w', 'v_b_group_b', 'v_b_scale', 'v_c_in_w', 'v_c_conv_w', 'v_c_out_w', 'v_f_up_w', 'v_f_dw_w', 'v_f_dw_b', 'v_f_down_w']
TWIN_OUTPUTS = ['loss', 'grad_x', 'grad_mod_w', 'grad_mod_b', 'grad_norm_pre_mix', 'grad_norm_post_mix', 'grad_norm_pre_ffn', 'grad_norm_post_ffn', 'grad_a_pw1_w', 'grad_a_pw1_b', 'grad_a_dw_w', 'grad_a_dw_b', 'grad_a_ln_g', 'grad_a_ln_b', 'grad_a_pw2_w', 'grad_a_pw2_b', 'grad_b_group_w', 'grad_b_group_b', 'grad_b_scale', 'grad_c_in_w', 'grad_c_conv_w', 'grad_c_out_w', 'grad_f_up_w', 'grad_f_dw_w', 'grad_f_dw_b', 'grad_f_down_w', 'delta_mod_w', 'delta_mod_b', 'delta_norm_pre_mix', 'delta_norm_post_mix', 'delta_norm_pre_ffn', 'delta_norm_post_ffn', 'delta_a_pw1_w', 'delta_a_pw1_b', 'delta_a_dw_w', 'delta_a_dw_b', 'delta_a_ln_g', 'delta_a_ln_b', 'delta_a_pw2_w', 'delta_a_pw2_b', 'delta_b_group_w', 'delta_b_group_b', 'delta_b_scale', 'delta_c_in_w', 'delta_c_conv_w', 'delta_c_out_w', 'delta_f_up_w', 'delta_f_dw_w', 'delta_f_dw_b', 'delta_f_down_w', 'new_m_mod_w', 'new_m_mod_b', 'new_m_norm_pre_mix', 'new_m_norm_post_mix', 'new_m_norm_pre_ffn', 'new_m_norm_post_ffn', 'new_m_a_pw1_w', 'new_m_a_pw1_b', 'new_m_a_dw_w', 'new_m_a_dw_b', 'new_m_a_ln_g', 'new_m_a_ln_b', 'new_m_a_pw2_w', 'new_m_a_pw2_b', 'new_m_b_group_w', 'new_m_b_group_b', 'new_m_b_scale', 'new_m_c_in_w', 'new_m_c_conv_w', 'new_m_c_out_w', 'new_m_f_up_w', 'new_m_f_dw_w', 'new_m_f_dw_b', 'new_m_f_down_w', 'new_v_mod_w', 'new_v_mod_b', 'new_v_norm_pre_mix', 'new_v_norm_post_mix', 'new_v_norm_pre_ffn', 'new_v_norm_post_ffn', 'new_v_a_pw1_w', 'new_v_a_pw1_b', 'new_v_a_dw_w', 'new_v_a_dw_b', 'new_v_a_ln_g', 'new_v_a_ln_b', 'new_v_a_pw2_w', 'new_v_a_pw2_b', 'new_v_b_group_w', 'new_v_b_group_b', 'new_v_b_scale', 'new_v_c_in_w', 'new_v_c_conv_w', 'new_v_c_out_w', 'new_v_f_up_w', 'new_v_f_dw_w', 'new_v_f_dw_b', 'new_v_f_down_w']
TWIN_LEAF_KINDS = {'loss': 'loss', 'grad_x': 'grad_x', 'grad_mod_w': 'grad_w', 'grad_mod_b': 'grad_w', 'grad_norm_pre_mix': 'grad_w', 'grad_norm_post_mix': 'grad_w', 'grad_norm_pre_ffn': 'grad_w', 'grad_norm_post_ffn': 'grad_w', 'grad_a_pw1_w': 'grad_w', 'grad_a_pw1_b': 'grad_w', 'grad_a_dw_w': 'grad_w', 'grad_a_dw_b': 'grad_w', 'grad_a_ln_g': 'grad_w', 'grad_a_ln_b': 'grad_w', 'grad_a_pw2_w': 'grad_w', 'grad_a_pw2_b': 'grad_w', 'grad_b_group_w': 'grad_w', 'grad_b_group_b': 'grad_w', 'grad_b_scale': 'grad_w', 'grad_c_in_w': 'grad_w', 'grad_c_conv_w': 'grad_w', 'grad_c_out_w': 'grad_w', 'grad_f_up_w': 'grad_w', 'grad_f_dw_w': 'grad_w', 'grad_f_dw_b': 'grad_w', 'grad_f_down_w': 'grad_w', 'delta_mod_w': 'delta_w', 'delta_mod_b': 'delta_w', 'delta_norm_pre_mix': 'delta_w', 'delta_norm_post_mix': 'delta_w', 'delta_norm_pre_ffn': 'delta_w', 'delta_norm_post_ffn': 'delta_w', 'delta_a_pw1_w': 'delta_w', 'delta_a_pw1_b': 'delta_w', 'delta_a_dw_w': 'delta_w', 'delta_a_dw_b': 'delta_w', 'delta_a_ln_g': 'delta_w', 'delta_a_ln_b': 'delta_w', 'delta_a_pw2_w': 'delta_w', 'delta_a_pw2_b': 'delta_w', 'delta_b_group_w': 'delta_w', 'delta_b_group_b': 'delta_w', 'delta_b_scale': 'delta_w', 'delta_c_in_w': 'delta_w', 'delta_c_conv_w': 'delta_w', 'delta_c_out_w': 'delta_w', 'delta_f_up_w': 'delta_w', 'delta_f_dw_w': 'delta_w', 'delta_f_dw_b': 'delta_w', 'delta_f_down_w': 'delta_w', 'new_m_mod_w': 'new_m', 'new_m_mod_b': 'new_m', 'new_m_norm_pre_mix': 'new_m', 'new_m_norm_post_mix': 'new_m', 'new_m_norm_pre_ffn': 'new_m', 'new_m_norm_post_ffn': 'new_m', 'new_m_a_pw1_w': 'new_m', 'new_m_a_pw1_b': 'new_m', 'new_m_a_dw_w': 'new_m', 'new_m_a_dw_b': 'new_m', 'new_m_a_ln_g': 'new_m', 'new_m_a_ln_b': 'new_m', 'new_m_a_pw2_w': 'new_m', 'new_m_a_pw2_b': 'new_m', 'new_m_b_group_w': 'new_m', 'new_m_b_group_b': 'new_m', 'new_m_b_scale': 'new_m', 'new_m_c_in_w': 'new_m', 'new_m_c_conv_w': 'new_m', 'new_m_c_out_w': 'new_m', 'new_m_f_up_w': 'new_m', 'new_m_f_dw_w': 'new_m', 'new_m_f_dw_b': 'new_m', 'new_m_f_down_w': 'new_m', 'new_v_mod_w': 'new_v', 'new_v_mod_b': 'new_v', 'new_v_norm_pre_mix': 'new_v', 'new_v_norm_post_mix': 'new_v', 'new_v_norm_pre_ffn': 'new_v', 'new_v_norm_post_ffn': 'new_v', 'new_v_a_pw1_w': 'new_v', 'new_v_a_pw1_b': 'new_v', 'new_v_a_dw_w': 'new_v', 'new_v_a_dw_b': 'new_v', 'new_v_a_ln_g': 'new_v', 'new_v_a_ln_b': 'new_v', 'new_v_a_pw2_w': 'new_v', 'new_v_a_pw2_b': 'new_v', 'new_v_b_group_w': 'new_v', 'new_v_b_group_b': 'new_v', 'new_v_b_scale': 'new_v', 'new_v_c_in_w': 'new_v', 'new_v_c_conv_w': 'new_v', 'new_v_c_out_w': 'new_v', 'new_v_f_up_w': 'new_v', 'new_v_f_dw_w': 'new_v', 'new_v_f_dw_b': 'new_v', 'new_v_f_down_w': 'new_v'}


def _forward(args):
    return _fwd_reference(*[args[k] for k in FWD_PARAMS])


def _output_shape():
    def fwd():
        inp = _fwd_setup_inputs(0)
        return _fwd_reference(*[inp[k] for k in FWD_PARAMS])
    out = _jax.eval_shape(fwd)
    return out.shape, out.dtype

N_MICROBATCH = 1
ADAM_LR = 0.001
ADAM_B1 = 0.9
ADAM_B2 = 0.999
ADAM_EPS = 1e-08
ADAM_WD = 0.01
ADAM_STEP = 10
PER_EXAMPLE_BATCH_AXIS = {'x': 0, 'c': 0, 'loss_target': 0}
SHARED_INPUTS = []
_WEIGHT_DTYPES = {'mod_w': _jnp.float32, 'mod_b': _jnp.float32, 'norm_pre_mix': _jnp.float32, 'norm_post_mix': _jnp.float32, 'norm_pre_ffn': _jnp.float32, 'norm_post_ffn': _jnp.float32, 'a_pw1_w': _jnp.float32, 'a_pw1_b': _jnp.float32, 'a_dw_w': _jnp.float32, 'a_dw_b': _jnp.float32, 'a_ln_g': _jnp.float32, 'a_ln_b': _jnp.float32, 'a_pw2_w': _jnp.float32, 'a_pw2_b': _jnp.float32, 'b_group_w': _jnp.float32, 'b_group_b': _jnp.float32, 'b_scale': _jnp.float32, 'c_in_w': _jnp.float32, 'c_conv_w': _jnp.float32, 'c_out_w': _jnp.float32, 'f_up_w': _jnp.float32, 'f_dw_w': _jnp.float32, 'f_dw_b': _jnp.float32, 'f_down_w': _jnp.float32}
MOMENT_SCALE = {'mod_w': 2.435112e+00, 'mod_b': 5.303421e+00, 'norm_pre_mix': 2.786402e-01, 'norm_post_mix': 6.713939e+00, 'norm_pre_ffn': 2.231233e-01, 'norm_post_ffn': 6.648940e+00, 'a_pw1_w': 2.155067e-01, 'a_pw1_b': 8.657556e-01, 'a_dw_w': 2.989905e-01, 'a_dw_b': 1.943051e+00, 'a_ln_g': 8.650631e-01, 'a_ln_b': 1.287032e+00, 'a_pw2_w': 5.443599e-01, 'a_pw2_b': 2.620829e+00, 'b_group_w': 4.215510e-01, 'b_group_b': 7.897976e-01, 'b_scale': 2.925893e+00, 'c_in_w': 1.505646e-01, 'c_conv_w': 1.604539e-01, 'c_out_w': 1.639212e-01, 'f_up_w': 1.114132e-01, 'f_dw_w': 1.204596e-01, 'f_dw_b': 2.250521e-01, 'f_down_w': 2.108845e-01}


def _to_microbatches(a, axis):
    t = _jnp.moveaxis(a, axis, 0)
    t = t.reshape((N_MICROBATCH, t.shape[0] // N_MICROBATCH) + t.shape[1:])
    return _jnp.moveaxis(t, 1, axis + 1)


def setup_inputs(seed: int = 0) -> dict:
    inp = _fwd_setup_inputs(seed)
    key = _jax.random.fold_in(_jax.random.key(seed), 7919)
    shape, _ = _output_shape()
    out = dict(inp)
    out["loss_target"] = _jax.random.normal(_jax.random.fold_in(key, 0), shape, _jnp.float32)
    for i, name in enumerate(TWIN_WEIGHTS):
        w = inp[name].astype(_jnp.float32)
        if MOMENT_SCALE is None:
            s = _jnp.sqrt(_jnp.mean(_jnp.square(w)) + 1e-30)
        else:
            s = MOMENT_SCALE[name]
        km, kv = _jax.random.split(_jax.random.fold_in(key, i + 1))
        out[name] = w
        out["m_" + name] = s * _jax.random.normal(km, w.shape, _jnp.float32)
        out["v_" + name] = (s * s) * _jax.random.uniform(kv, w.shape, _jnp.float32, 0.5, 1.5)
    if N_MICROBATCH > 1:
        for name, axis in PER_EXAMPLE_BATCH_AXIS.items():
            out[name] = _to_microbatches(out[name], axis)
    return {'x': out['x'], 'c': out['c'], 'mod_w': out['mod_w'], 'mod_b': out['mod_b'], 'norm_pre_mix': out['norm_pre_mix'], 'norm_post_mix': out['norm_post_mix'], 'norm_pre_ffn': out['norm_pre_ffn'], 'norm_post_ffn': out['norm_post_ffn'], 'a_pw1_w': out['a_pw1_w'], 'a_pw1_b': out['a_pw1_b'], 'a_dw_w': out['a_dw_w'], 'a_dw_b': out['a_dw_b'], 'a_ln_g': out['a_ln_g'], 'a_ln_b': out['a_ln_b'], 'a_pw2_w': out['a_pw2_w'], 'a_pw2_b': out['a_pw2_b'], 'b_group_w': out['b_group_w'], 'b_group_b': out['b_group_b'], 'b_scale': out['b_scale'], 'c_in_w': out['c_in_w'], 'c_conv_w': out['c_conv_w'], 'c_out_w': out['c_out_w'], 'f_up_w': out['f_up_w'], 'f_dw_w': out['f_dw_w'], 'f_dw_b': out['f_dw_b'], 'f_down_w': out['f_down_w'], 'loss_target': out['loss_target'], 'm_mod_w': out['m_mod_w'], 'm_mod_b': out['m_mod_b'], 'm_norm_pre_mix': out['m_norm_pre_mix'], 'm_norm_post_mix': out['m_norm_post_mix'], 'm_norm_pre_ffn': out['m_norm_pre_ffn'], 'm_norm_post_ffn': out['m_norm_post_ffn'], 'm_a_pw1_w': out['m_a_pw1_w'], 'm_a_pw1_b': out['m_a_pw1_b'], 'm_a_dw_w': out['m_a_dw_w'], 'm_a_dw_b': out['m_a_dw_b'], 'm_a_ln_g': out['m_a_ln_g'], 'm_a_ln_b': out['m_a_ln_b'], 'm_a_pw2_w': out['m_a_pw2_w'], 'm_a_pw2_b': out['m_a_pw2_b'], 'm_b_group_w': out['m_b_group_w'], 'm_b_group_b': out['m_b_group_b'], 'm_b_scale': out['m_b_scale'], 'm_c_in_w': out['m_c_in_w'], 'm_c_conv_w': out['m_c_conv_w'], 'm_c_out_w': out['m_c_out_w'], 'm_f_up_w': out['m_f_up_w'], 'm_f_dw_w': out['m_f_dw_w'], 'm_f_dw_b': out['m_f_dw_b'], 'm_f_down_w': out['m_f_down_w'], 'v_mod_w': out['v_mod_w'], 'v_mod_b': out['v_mod_b'], 'v_norm_pre_mix': out['v_norm_pre_mix'], 'v_norm_post_mix': out['v_norm_post_mix'], 'v_norm_pre_ffn': out['v_norm_pre_ffn'], 'v_norm_post_ffn': out['v_norm_post_ffn'], 'v_a_pw1_w': out['v_a_pw1_w'], 'v_a_pw1_b': out['v_a_pw1_b'], 'v_a_dw_w': out['v_a_dw_w'], 'v_a_dw_b': out['v_a_dw_b'], 'v_a_ln_g': out['v_a_ln_g'], 'v_a_ln_b': out['v_a_ln_b'], 'v_a_pw2_w': out['v_a_pw2_w'], 'v_a_pw2_b': out['v_a_pw2_b'], 'v_b_group_w': out['v_b_group_w'], 'v_b_group_b': out['v_b_group_b'], 'v_b_scale': out['v_b_scale'], 'v_c_in_w': out['v_c_in_w'], 'v_c_conv_w': out['v_c_conv_w'], 'v_c_out_w': out['v_c_out_w'], 'v_f_up_w': out['v_f_up_w'], 'v_f_dw_w': out['v_f_dw_w'], 'v_f_dw_b': out['v_f_dw_b'], 'v_f_down_w': out['v_f_down_w']}


def _loss(weights, diff, rest, loss_target):
    with _jax.named_scope("forward"):
        args = {**rest, TWIN_DIFF_INPUT: diff, **{k: w.astype(_WEIGHT_DTYPES[k]) for k, w in weights.items()}}
        y = _forward(args)
    with _jax.named_scope("loss_head"):
        err = _jnp.square(y.astype(_jnp.float32) - loss_target)
        return 0.5 * _jnp.sum(_jnp.mean(err, axis=-1)) if err.ndim else 0.5 * err


def _adamw(w, g, m, v):
    m = ADAM_B1 * m + (1.0 - ADAM_B1) * g
    v = ADAM_B2 * v + (1.0 - ADAM_B2) * _jnp.square(g)
    m_hat = m / (1.0 - ADAM_B1 ** ADAM_STEP)
    v_hat = v / (1.0 - ADAM_B2 ** ADAM_STEP)
    delta = -ADAM_LR * (m_hat / (_jnp.sqrt(v_hat) + ADAM_EPS) + ADAM_WD * w)
    return delta, m, v


def reference(x, c, mod_w, mod_b, norm_pre_mix, norm_post_mix, norm_pre_ffn, norm_post_ffn, a_pw1_w, a_pw1_b, a_dw_w, a_dw_b, a_ln_g, a_ln_b, a_pw2_w, a_pw2_b, b_group_w, b_group_b, b_scale, c_in_w, c_conv_w, c_out_w, f_up_w, f_dw_w, f_dw_b, f_down_w, loss_target, m_mod_w, m_mod_b, m_norm_pre_mix, m_norm_post_mix, m_norm_pre_ffn, m_norm_post_ffn, m_a_pw1_w, m_a_pw1_b, m_a_dw_w, m_a_dw_b, m_a_ln_g, m_a_ln_b, m_a_pw2_w, m_a_pw2_b, m_b_group_w, m_b_group_b, m_b_scale, m_c_in_w, m_c_conv_w, m_c_out_w, m_f_up_w, m_f_dw_w, m_f_dw_b, m_f_down_w, v_mod_w, v_mod_b, v_norm_pre_mix, v_norm_post_mix, v_norm_pre_ffn, v_norm_post_ffn, v_a_pw1_w, v_a_pw1_b, v_a_dw_w, v_a_dw_b, v_a_ln_g, v_a_ln_b, v_a_pw2_w, v_a_pw2_b, v_b_group_w, v_b_group_b, v_b_scale, v_c_in_w, v_c_conv_w, v_c_out_w, v_f_up_w, v_f_dw_w, v_f_dw_b, v_f_down_w):
    given = dict(x=x, c=c, mod_w=mod_w, mod_b=mod_b, norm_pre_mix=norm_pre_mix, norm_post_mix=norm_post_mix, norm_pre_ffn=norm_pre_ffn, norm_post_ffn=norm_post_ffn, a_pw1_w=a_pw1_w, a_pw1_b=a_pw1_b, a_dw_w=a_dw_w, a_dw_b=a_dw_b, a_ln_g=a_ln_g, a_ln_b=a_ln_b, a_pw2_w=a_pw2_w, a_pw2_b=a_pw2_b, b_group_w=b_group_w, b_group_b=b_group_b, b_scale=b_scale, c_in_w=c_in_w, c_conv_w=c_conv_w, c_out_w=c_out_w, f_up_w=f_up_w, f_dw_w=f_dw_w, f_dw_b=f_dw_b, f_down_w=f_down_w, loss_target=loss_target, m_mod_w=m_mod_w, m_mod_b=m_mod_b, m_norm_pre_mix=m_norm_pre_mix, m_norm_post_mix=m_norm_post_mix, m_norm_pre_ffn=m_norm_pre_ffn, m_norm_post_ffn=m_norm_post_ffn, m_a_pw1_w=m_a_pw1_w, m_a_pw1_b=m_a_pw1_b, m_a_dw_w=m_a_dw_w, m_a_dw_b=m_a_dw_b, m_a_ln_g=m_a_ln_g, m_a_ln_b=m_a_ln_b, m_a_pw2_w=m_a_pw2_w, m_a_pw2_b=m_a_pw2_b, m_b_group_w=m_b_group_w, m_b_group_b=m_b_group_b, m_b_scale=m_b_scale, m_c_in_w=m_c_in_w, m_c_conv_w=m_c_conv_w, m_c_out_w=m_c_out_w, m_f_up_w=m_f_up_w, m_f_dw_w=m_f_dw_w, m_f_dw_b=m_f_dw_b, m_f_down_w=m_f_down_w, v_mod_w=v_mod_w, v_mod_b=v_mod_b, v_norm_pre_mix=v_norm_pre_mix, v_norm_post_mix=v_norm_post_mix, v_norm_pre_ffn=v_norm_pre_ffn, v_norm_post_ffn=v_norm_post_ffn, v_a_pw1_w=v_a_pw1_w, v_a_pw1_b=v_a_pw1_b, v_a_dw_w=v_a_dw_w, v_a_dw_b=v_a_dw_b, v_a_ln_g=v_a_ln_g, v_a_ln_b=v_a_ln_b, v_a_pw2_w=v_a_pw2_w, v_a_pw2_b=v_a_pw2_b, v_b_group_w=v_b_group_w, v_b_group_b=v_b_group_b, v_b_scale=v_b_scale, v_c_in_w=v_c_in_w, v_c_conv_w=v_c_conv_w, v_c_out_w=v_c_out_w, v_f_up_w=v_f_up_w, v_f_dw_w=v_f_dw_w, v_f_dw_b=v_f_dw_b, v_f_down_w=v_f_down_w)
    weights = {n: given[n] for n in TWIN_WEIGHTS}
    shared = {n: given[n] for n in SHARED_INPUTS}
    per_example = {n: given[n] for n in ['x', 'c']}
    grad_fn = _jax.value_and_grad(_loss, argnums=(0, 1))

    def one_microbatch(ex, loss_target):
        ex = dict(ex)
        diff = ex.pop(TWIN_DIFF_INPUT)
        return grad_fn(weights, diff, {**shared, **ex}, loss_target)

    if N_MICROBATCH == 1:
        loss, (grad_w, grad_x) = one_microbatch(per_example, given["loss_target"])
    else:
        def body(carry, xs):
            loss_sum, grad_sum = carry
            l_k, (gw_k, gx_k) = one_microbatch(xs[0], xs[1])
            with _jax.named_scope("update"):
                return (loss_sum + l_k, _jax.tree.map(_jnp.add, grad_sum, gw_k)), gx_k

        init = (_jnp.zeros((), _jnp.float32), _jax.tree.map(_jnp.zeros_like, weights))
        (loss, grad_w), grad_x = _jax.lax.scan(body, init, (per_example, given["loss_target"]))
    with _jax.named_scope("update"):
        delta_w, new_m, new_v = {}, {}, {}
        for n in TWIN_WEIGHTS:
            delta_w[n], new_m[n], new_v[n] = _adamw(weights[n], grad_w[n], given["m_" + n], given["v_" + n])
    return (loss, grad_x, *[grad_w[n] for n in TWIN_WEIGHTS], *[delta_w[n] for n in TWIN_WEIGHTS],
            *[new_m[n] for n in TWIN_WEIGHTS], *[new_v[n] for n in TWIN_WEIGHTS])
```

```python
import functools

import jax
import jax.numpy as jnp
from jax import lax
from jax.experimental import pallas as pl
from jax.experimental.pallas import tpu as pltpu

F32, BF16 = jnp.float32, jnp.bfloat16
AXES = ("x", "y", "c")
N_DEV = 8
LANE = 128
VMEM_LIMIT = 48 * 1024 * 1024
RMS_EPS, LN_EPS = 1e-6, 1e-5
POOL_WINDOWS = (2, 4, 8, 16)
N_MIXERS = 3
ADAM_LR, ADAM_B1, ADAM_B2, ADAM_EPS, ADAM_WD, ADAM_STEP = 0.001, 0.9, 0.999, 1e-08, 0.01, 10

WEIGHTS = ['mod_w', 'mod_b', 'norm_pre_mix', 'norm_post_mix', 'norm_pre_ffn', 'norm_post_ffn', 'a_pw1_w', 'a_pw1_b',
           'a_dw_w', 'a_dw_b', 'a_ln_g', 'a_ln_b', 'a_pw2_w', 'a_pw2_b', 'b_group_w', 'b_group_b', 'b_scale', 'c_in_w',
           'c_conv_w', 'c_out_w', 'f_up_w', 'f_dw_w', 'f_dw_b', 'f_down_w']
INPUTS = (['x', 'c'] + WEIGHTS + ['loss_target'] + ['m_' + w for w in WEIGHTS] + ['v_' + w for w in WEIGHTS])


def _tile(n, cap, mult):
    for t in range(min(n, cap), 0, -1):
        if n % t == 0 and t % mult == 0:
            return t
    return n


def _round_up(n, m):
    return -(-n // m) * m


def _params(*dims):
    return pltpu.CompilerParams(dimension_semantics=dims or None, vmem_limit_bytes=VMEM_LIMIT)


def _sigmoid(v):
    return jax.nn.sigmoid(v)


def _exchange(arrs, *, scatter, name):
    n = len(arrs)

    def body(*refs):
        ins, outs = refs[:n], refs[n:2 * n]
        send_sems, recv_sems, local_sems = refs[2 * n:]
        x, y, c = (lax.axis_index(a) for a in AXES)
        me = 4 * x + 2 * y + c

        def peer(k):
            px = 1 - x if (k >> 2) & 1 else x
            py = 1 - y if (k >> 1) & 1 else y
            pc = 1 - c if k & 1 else c
            return (px, py, pc), 4 * px + 2 * py + pc

        def remote(i, k, src, dst):
            return pltpu.make_async_remote_copy(
                src_ref=src, dst_ref=dst, send_sem=send_sems.at[i, k - 1], recv_sem=recv_sems.at[i, k - 1],
                device_id=peer(k)[0], device_id_type=pl.DeviceIdType.MESH)

        own = []
        for i in range(n):
            cp = pltpu.make_async_copy(ins[i].at[me] if scatter else ins[i], outs[i].at[me], local_sems.at[i])
            cp.start()
            own.append(cp)
        sent = []
        for k in range(1, N_DEV):
            for i in range(n):
                cp = remote(i, k, ins[i].at[peer(k)[1]] if scatter else ins[i], outs[i].at[me])
                cp.start()
                sent.append(cp)
        for k in range(1, N_DEV):
            for i in range(n):
                landing = outs[i].at[peer(k)[1]]
                remote(i, k, landing, landing).wait_recv()
        for cp in sent:
            cp.wait_send()
        for cp in own:
            cp.wait()

    any_spec = pl.BlockSpec(memory_space=pl.ANY)
    out_shape = [jax.ShapeDtypeStruct(a.shape if scatter else (N_DEV,) + a.shape, a.dtype) for a in arrs]
    return pl.pallas_call(
        body, name=name, out_shape=out_shape, in_specs=[any_spec] * n, out_specs=[any_spec] * n,
        scratch_shapes=[pltpu.SemaphoreType.DMA((n, N_DEV - 1)), pltpu.SemaphoreType.DMA((n, N_DEV - 1)),
                        pltpu.SemaphoreType.DMA((n,))],
    )(*arrs)


def _mm(a, b, *, name, nt=False, bias=None, add=None, out_dtype=F32, shard_cols=0):
    M, K = a.shape
    N = b.shape[0] if nt else b.shape[1]
    tm, tn, tk = _tile(M, 1024, 16), shard_cols or _tile(N, 768, LANE), _tile(K, 1024, LANE)
    nk = K // tk
    dims = (((1,), (1,)), ((), ())) if nt else (((1,), (0,)), ((), ()))

    def body(*refs):
        refs = list(refs)
        a_ref, b_ref = refs.pop(0), refs.pop(0)
        bias_ref = refs.pop(0) if bias is not None else None
        add_ref = refs.pop(0) if add is not None else None
        o_ref = refs.pop(0)
        part = lax.dot_general(a_ref[...], b_ref[...], dims, preferred_element_type=F32)

        def finish(val):
            if bias_ref is not None:
                val = val + bias_ref[...]
            if add_ref is not None:
                val = val + add_ref[...]
            o_ref[...] = val.astype(o_ref.dtype)

        if nk == 1:
            finish(part)
        else:
            acc_ref = refs.pop(0)
            k = pl.program_id(2)

            @pl.when(k == 0)
            def _():
                acc_ref[...] = part

            @pl.when(k > 0)
            def _():
                acc_ref[...] += part

            @pl.when(k == nk - 1)
            def _():
                finish(acc_ref[...])

    in_specs = [pl.BlockSpec((tm, tk), lambda i, j, k: (i, k)),
                pl.BlockSpec((tn, tk), lambda i, j, k: (j, k)) if nt else pl.BlockSpec((tk, tn), lambda i, j, k: (k, j))]
    operands = [a, b]
    if bias is not None:
        in_specs.append(pl.BlockSpec((1, tn), lambda i, j, k: (0, j)))
        operands.append(bias)
    if add is not None:
        in_specs.append(pl.BlockSpec((tm, tn), lambda i, j, k: (i, j)))
        operands.append(add)
    if shard_cols:
        out_shape = jax.ShapeDtypeStruct((N // tn, M, tn), out_dtype)
        out_spec = pl.BlockSpec((None, tm, tn), lambda i, j, k: (j, i, 0))
    else:
        out_shape = jax.ShapeDtypeStruct((M, N), out_dtype)
        out_spec = pl.BlockSpec((tm, tn), lambda i, j, k: (i, j))
    return pl.pallas_call(
        body, name=name, grid=(M // tm, N // tn, nk), in_specs=in_specs, out_specs=out_spec, out_shape=out_shape,
        scratch_shapes=[pltpu.VMEM((tm, tn), F32)] if nk > 1 else [],
        compiler_params=_params("parallel", "parallel", "arbitrary"),
    )(*operands)


def _transpose(x, *, name):
    T, C = x.shape
    tt, tc = _tile(T, 512, LANE), _tile(C, 512, LANE)

    def body(x_ref, o_ref):
        o_ref[...] = x_ref[...].astype(F32).T.astype(BF16)

    return pl.pallas_call(
        body, name=name, grid=(T // tt, C // tc), in_specs=[pl.BlockSpec((tt, tc), lambda i, j: (i, j))],
        out_specs=pl.BlockSpec((tc, tt), lambda i, j: (j, i)), out_shape=jax.ShapeDtypeStruct((C, T), BF16),
        compiler_params=_params("parallel", "parallel"),
    )(x)


def _group_mm(a, w, *, name, nt=False, bias=None, scale=None):
    T, D = a.shape
    G, gd, _ = w.shape
    tm = _tile(T, 1024, 16)
    dims = (((1,), (1,)), ((), ())) if nt else (((1,), (0,)), ((), ()))
    fused = bias is not None

    def body(*refs):
        if fused:
            a_ref, w_ref, b_ref, s_ref, mb_ref, y_ref = refs
        else:
            a_ref, w_ref, y_ref = refs
        val = lax.dot_general(a_ref[...], w_ref[...], dims, preferred_element_type=F32)
        if fused:
            val = val + b_ref[...]
            mb_ref[...] = val
            val = val * s_ref[...]
        y_ref[...] = val

    blk = pl.BlockSpec((tm, gd), lambda i, g: (i, g))
    vec = pl.BlockSpec((1, gd), lambda i, g: (0, g))
    in_specs = [blk, pl.BlockSpec((None, gd, gd), lambda i, g: (g, 0, 0))] + ([vec, vec] if fused else [])
    td = jax.ShapeDtypeStruct((T, D), F32)
    return pl.pallas_call(
        body, name=name, grid=(T // tm, G), in_specs=in_specs, out_specs=[blk, blk] if fused else blk,
        out_shape=[td, td] if fused else td, compiler_params=_params("parallel", "parallel"),
    )(*([a, w, bias, scale] if fused else [a, w]))


def _group_dw(at, b, groups, *, name):
    D, T = at.shape
    gd = D // groups
    tk = _tile(T, 2048, LANE)

    def body(a_ref, b_ref, o_ref):
        @pl.when(pl.program_id(1) == 0)
        def _():
            o_ref[...] = jnp.zeros_like(o_ref)

        o_ref[...] += jnp.dot(a_ref[...], b_ref[...], preferred_element_type=F32)

    return pl.pallas_call(
        body, name=name, grid=(groups, T // tk),
        in_specs=[pl.BlockSpec((gd, tk), lambda g, k: (g, k)), pl.BlockSpec((tk, gd), lambda g, k: (k, g))],
        out_specs=pl.BlockSpec((None, gd, gd), lambda g, k: (g, 0, 0)),
        out_shape=jax.ShapeDtypeStruct((groups, gd, gd), F32), compiler_params=_params("parallel", "arbitrary"),
    )(at, b)


def _mod_fwd(c_all, w, b, *, name):
    L, D, n = w.shape
    B = c_all.shape[0]

    def body(c_ref, w_ref, b_ref, o_ref):
        cv = c_ref[...]
        o_ref[...] = jnp.dot(cv * _sigmoid(cv), w_ref[...], preferred_element_type=F32) + b_ref[...]

    return pl.pallas_call(
        body, name=name, grid=(L,),
        in_specs=[pl.BlockSpec((B, D), lambda l: (0, 0)), pl.BlockSpec((None, D, n), lambda l: (l, 0, 0)),
                  pl.BlockSpec((None, 1, n), lambda l: (l, 0, 0))],
        out_specs=pl.BlockSpec((None, B, n), lambda l: (l, 0, 0)), out_shape=jax.ShapeDtypeStruct((L, B, n), F32),
        compiler_params=_params("parallel"),
    )(c_all, w, b.reshape(L, 1, n))


def _mod_bwd(c_all_t, dmod, *, name):
    L, B, n = dmod.shape
    D = c_all_t.shape[0]

    def body(c_ref, d_ref, o_ref):
        cv = c_ref[...]
        o_ref[...] = jnp.dot(cv * _sigmoid(cv), d_ref[...], preferred_element_type=F32)

    return pl.pallas_call(
        body, name=name, grid=(L,),
        in_specs=[pl.BlockSpec((D, B), lambda l: (0, 0)), pl.BlockSpec((None, B, n), lambda l: (l, 0, 0))],
        out_specs=pl.BlockSpec((None, D, n), lambda l: (l, 0, 0)), out_shape=jax.ShapeDtypeStruct((L, D, n), F32),
        compiler_params=_params("parallel"),
    )(c_all_t, dmod)


def _rowwise(fn, tiles, vecs, out_dtypes, n_sums, *, name):
    T, C = tiles[0].shape
    tr = _tile(T, 256, 16)
    ch = _tile(tr, 32, 16)
    nt_, nv, no = len(tiles), len(vecs), len(out_dtypes)

    def body(*refs):
        t_refs, v_refs = refs[:nt_], refs[nt_:nt_ + nv]
        o_refs, s_refs = refs[nt_ + nv:nt_ + nv + no], refs[nt_ + nv + no:]
        vv = [v[...] for v in v_refs]
        acc = [None] * n_sums
        for r0 in range(0, tr, ch):
            rows = pl.ds(r0, ch)
            o_vals, s_vals = fn([t[rows, :] for t in t_refs], vv)
            for o, val in zip(o_refs, o_vals):
                o[rows, :] = val.astype(o.dtype)
            for i, val in enumerate(s_vals):
                part = jnp.sum(val, axis=0, keepdims=True)
                acc[i] = part if acc[i] is None else acc[i] + part
        if n_sums:
            first = pl.program_id(0) == 0

            @pl.when(first)
            def _():
                for s, val in zip(s_refs, acc):
                    s[...] = val

            @pl.when(jnp.logical_not(first))
            def _():
                for s, val in zip(s_refs, acc):
                    s[...] += val

    tile_spec = pl.BlockSpec((tr, C), lambda i: (i, 0))
    vec_spec = pl.BlockSpec((1, C), lambda i: (0, 0))
    res = pl.pallas_call(
        body, name=name, grid=(T // tr,), in_specs=[tile_spec] * nt_ + [vec_spec] * nv,
        out_specs=[tile_spec] * no + [vec_spec] * n_sums,
        out_shape=[jax.ShapeDtypeStruct((T, C), d) for d in out_dtypes] + [jax.ShapeDtypeStruct((1, C), F32)] * n_sums,
        compiler_params=_params("arbitrary"),
    )(*tiles, *vecs)
    return res


def _rms(v):
    return lax.rsqrt(jnp.mean(v * v, axis=-1, keepdims=True) + RMS_EPS)


def _prenorm_fwd(x, g, sc, sh, out_dtype, *, name):
    def fn(t, v):
        (xv,), (g_, sc_, sh_) = t, v
        return [(xv * _rms(xv)) * g_ * (1.0 + sc_) + sh_], []

    return _rowwise(fn, [x], [g, sc, sh], [out_dtype], 0, name=name)[0]


def _post_pre_fwd(x, y, gp, gate, g, sc, sh, out_dtype, *, name):
    def fn(t, v):
        (xv, yv), (gp_, gate_, g_, sc_, sh_) = t, v
        x1 = xv + gate_ * ((yv * _rms(yv)) * gp_)
        return [x1, (x1 * _rms(x1)) * g_ * (1.0 + sc_) + sh_], []

    return _rowwise(fn, [x, y], [gp, gate, g, sc, sh], [F32, out_dtype], 0, name=name)


def _post_loss(x, y, gp, gate, target, *, name):
    D = x.shape[1]

    def fn(t, v):
        (xv, yv, tv), (gp_, gate_) = t, v
        err = xv + gate_ * ((yv * _rms(yv)) * gp_) - tv
        return [err / D], [err * err]

    return _rowwise(fn, [x, y, target], [gp, gate], [F32], 1, name=name)


def _post_bwd(y, dxo, gp, gate, out_dtype, *, name):
    def fn(t, v):
        (yv, dv), (gp_, gate_) = t, v
        r = _rms(yv)
        yn = yv * r
        dyn = dv * gate_ * gp_
        dy = r * (dyn - yn * jnp.mean(dyn * yn, axis=-1, keepdims=True))
        return [dy], [dv * gate_ * yn, dv * yn * gp_, dy]

    return _rowwise(fn, [y, dxo], [gp, gate], [out_dtype], 3, name=name)


def _pre_bwd(x, dh, dres, g, sc, *, name):
    def fn(t, v):
        (xv, dhv, dr), (g_, sc_) = t, v
        dhv = dhv.astype(F32)
        r = _rms(xv)
        xn = xv * r
        dxn = dhv * g_ * (1.0 + sc_)
        dx = dr + r * (dxn - xn * jnp.mean(dxn * xn, axis=-1, keepdims=True))
        return [dx], [dhv, dhv * xn * g_, dhv * xn * (1.0 + sc_)]

    return _rowwise(fn, [x, dh, dres], [g, sc], [F32], 3, name=name)


def _a_ln_bwd(z, dq, g, b, *, name):
    def fn(t, v):
        (zv, dqv), (g_, b_) = t, v
        mu = jnp.mean(zv, axis=-1, keepdims=True)
        zc = zv - mu
        rstd = lax.rsqrt(jnp.mean(zc * zc, axis=-1, keepdims=True) + LN_EPS)
        zn = zc * rstd
        l = zn * g_ + b_
        s = _sigmoid(l)
        dl = dqv * (s * (1.0 + l * (1.0 - s)))
        dzn = dl * g_
        dz = rstd * (dzn - jnp.mean(dzn, axis=-1, keepdims=True) - zn * jnp.mean(dzn * zn, axis=-1, keepdims=True))
        return [dz], [dl * zn, dl, dz]

    return _rowwise(fn, [z, dq], [g, b], [F32], 3, name=name)


def _a_glu_bwd(ua, ug, dp, *, name):
    def fn(t, v):
        uav, ugv, dpv = t
        s = _sigmoid(ugv)
        dua = dpv * s
        dug = dpv * uav * s * (1.0 - s)
        return [dua, dug], [dua, dug]

    return _rowwise(fn, [ua, ug, dp], [], [BF16, BF16], 2, name=name)


def _b_scale_bwd(dy, mb, scale, *, name):
    def fn(t, v):
        (dyv, mbv), (s_,) = t, v
        dm = dyv * s_
        return [dm], [dyv * mbv, dm]

    return _rowwise(fn, [dy, mb], [scale], [BF16], 2, name=name)


def _c_gate_bwd1(gb, cv, ds, *, name):
    def fn(t, v):
        gbv, cvv, dsv = t
        return [dsv * cvv, dsv * gbv], []

    return _rowwise(fn, [gb, cv, ds], [], [BF16, F32], 0, name=name)


def _c_gate_bwd2(gc, v_, dp, *, name):
    def fn(t, v):
        gcv, vv, dpv = t
        return [dpv * vv, dpv * gcv], []

    return _rowwise(fn, [gc, v_, dp], [], [BF16, BF16], 0, name=name)


def _stage_causal(buf, t, halo, tt):
    @pl.when(t == 0)
    def _():
        buf[pl.ds(0, halo), :] = jnp.zeros((halo, buf.shape[1]), F32)

    @pl.when(t > 0)
    def _():
        buf[pl.ds(0, halo), :] = buf[pl.ds(tt, halo), :]


def _stage_anticausal(buf, t, halo, tt):
    @pl.when(t == 0)
    def _():
        buf[pl.ds(tt, halo), :] = jnp.zeros((halo, buf.shape[1]), F32)

    @pl.when(t > 0)
    def _():
        buf[pl.ds(tt, halo), :] = buf[pl.ds(0, halo), :]


def _causal_taps(buf, w_ref, taps, halo, r0, rows, cols):
    acc = None
    for k in range(taps):
        term = w_ref[pl.ds(k, 1), cols] * buf[pl.ds(halo - (taps - 1) + k + r0, rows), cols]
        acc = term if acc is None else acc + term
    return acc


def _halo_tiles(T, C, halo, col_cap):
    tt = _tile(T, 256, 16)
    assert tt >= halo, (tt, halo)
    return tt, _tile(C, col_cap, LANE), _tile(tt, 128, 16)


def _col_blocks(tc):
    cb = LANE if tc % LANE == 0 else tc
    return [pl.ds(c0, cb) for c0 in range(0, tc, cb)]


def _ffn_act_fwd(ug, uv, wg, wv, bg, bv, *, name):
    T, F = ug.shape
    taps, halo = wg.shape[0], 8
    tt, tc, rc = _halo_tiles(T, F, halo, 1024)

    def body(ug_ref, uv_ref, wg_ref, wv_ref, bg_ref, bv_ref, a_ref, bufg, bufv):
        t = pl.program_id(1)
        for u_ref, buf in ((ug_ref, bufg), (uv_ref, bufv)):
            _stage_causal(buf, t, halo, tt)
            buf[pl.ds(halo, tt), :] = u_ref[...]
        for cols in _col_blocks(tc):
            for r0 in range(0, tt, rc):
                yg = _causal_taps(bufg, wg_ref, taps, halo, r0, rc, cols) + bg_ref[:, cols]
                yv = _causal_taps(bufv, wv_ref, taps, halo, r0, rc, cols) + bv_ref[:, cols]
                a_ref[pl.ds(r0, rc), cols] = (yg * _sigmoid(yg) * yv).astype(BF16)

    blk = pl.BlockSpec((tt, tc), lambda j, t: (t, j))
    wspec = pl.BlockSpec((taps, tc), lambda j, t: (0, j))
    vspec = pl.BlockSpec((1, tc), lambda j, t: (0, j))
    return pl.pallas_call(
        body, name=name, grid=(F // tc, T // tt), in_specs=[blk, blk, wspec, wspec, vspec, vspec], out_specs=blk,
        out_shape=jax.ShapeDtypeStruct((T, F), BF16),
        scratch_shapes=[pltpu.VMEM((halo + tt, tc), F32), pltpu.VMEM((halo + tt, tc), F32)],
        compiler_params=_params("parallel", "arbitrary"),
    )(ug, uv, wg, wv, bg, bv)


def _ffn_act_bwd(ug, uv, wg, wv, bg, bv, da, *, name):
    T, F = ug.shape
    taps, halo = wg.shape[0], 8
    tt, tc, rc = _halo_tiles(T, F, halo, 1024)

    def body(ug_ref, uv_ref, wg_ref, wv_ref, bg_ref, bv_ref, da_ref, dg_ref, dv_ref, sg_ref, sv_ref, bufg, bufv):
        t = pl.program_id(1)

        @pl.when(t == 0)
        def _():
            sg_ref[...] = jnp.zeros_like(sg_ref)
            sv_ref[...] = jnp.zeros_like(sv_ref)

        for u_ref, buf in ((ug_ref, bufg), (uv_ref, bufv)):
            _stage_causal(buf, t, halo, tt)
            buf[pl.ds(halo, tt), :] = u_ref[...]
        for cols in _col_blocks(tc):
            for r0 in range(0, tt, rc):
                rows = pl.ds(r0, rc)
                yg = _causal_taps(bufg, wg_ref, taps, halo, r0, rc, cols) + bg_ref[:, cols]
                yv = _causal_taps(bufv, wv_ref, taps, halo, r0, rc, cols) + bv_ref[:, cols]
                s = _sigmoid(yg)
                d = da_ref[rows, cols]
                dyg = d * yv * (s * (1.0 + yg * (1.0 - s)))
                dyv = d * (yg * s)
                dg_ref[rows, cols] = dyg
                dv_ref[rows, cols] = dyv
                sg_ref[:, cols] += jnp.sum(dyg, axis=0, keepdims=True)
                sv_ref[:, cols] += jnp.sum(dyv, axis=0, keepdims=True)

    blk = pl.BlockSpec((tt, tc), lambda j, t: (t, j))
    wspec = pl.BlockSpec((taps, tc), lambda j, t: (0, j))
    vspec = pl.BlockSpec((1, tc), lambda j, t: (0, j))
    td, vd = jax.ShapeDtypeStruct((T, F), F32), jax.ShapeDtypeStruct((1, F), F32)
    return pl.pallas_call(
        body, name=name, grid=(F // tc, T // tt), in_specs=[blk, blk, wspec, wspec, vspec, vspec, blk],
        out_specs=[blk, blk, vspec, vspec], out_shape=[td, td, vd, vd],
        scratch_shapes=[pltpu.VMEM((halo + tt, tc), F32), pltpu.VMEM((halo + tt, tc), F32)],
        compiler_params=_params("parallel", "arbitrary"),
    )(ug, uv, wg, wv, bg, bv, da)


def _dwconv_bwd(x, dy, w, out_dtype, *, name):
    T, C = x.shape
    taps = w.shape[0]
    halo = _round_up(taps - 1, 8)
    tt, tc, rc = _halo_tiles(T, C, halo, 512)
    nt_ = T // tt

    def body(x_ref, dy_ref, w_ref, dx_ref, dw_ref, buf):
        t = pl.program_id(1)

        @pl.when(t == 0)
        def _():
            dw_ref[...] = jnp.zeros_like(dw_ref)

        _stage_anticausal(buf, t, halo, tt)
        buf[pl.ds(0, tt), :] = dy_ref[...]
        for cols in _col_blocks(tc):
            for r0 in range(0, tt, rc):
                rows = pl.ds(r0, rc)
                xv = x_ref[rows, cols]
                acc = None
                for j in range(taps):
                    k = taps - 1 - j
                    ahead = buf[pl.ds(r0 + j, rc), cols]
                    term = w_ref[pl.ds(k, 1), cols] * ahead
                    acc = term if acc is None else acc + term
                    dw_ref[pl.ds(k, 1), cols] += jnp.sum(xv * ahead, axis=0, keepdims=True)
                dx_ref[rows, cols] = acc.astype(dx_ref.dtype)

    blk = pl.BlockSpec((tt, tc), lambda j, t: (nt_ - 1 - t, j))
    wspec = pl.BlockSpec((taps, tc), lambda j, t: (0, j))
    return pl.pallas_call(
        body, name=name, grid=(C // tc, nt_), in_specs=[blk, blk, wspec], out_specs=[blk, wspec],
        out_shape=[jax.ShapeDtypeStruct((T, C), out_dtype), jax.ShapeDtypeStruct((taps, C), F32)],
        scratch_shapes=[pltpu.VMEM((tt + halo, tc), F32)], compiler_params=_params("parallel", "arbitrary"),
    )(x, dy, w)


def _a_conv_fwd(ua, ug, w, b, ln_g, ln_b, *, name):
    T, D = ua.shape
    taps = w.shape[0]
    halo = _round_up(taps - 1, 8)
    tt = _tile(T, 128, 16)
    assert tt >= halo
    rc, ch = _tile(tt, 128, 16), _tile(tt, 32, 16)

    def body(ua_ref, ug_ref, w_ref, b_ref, g_ref, lb_ref, p_ref, z_ref, q_ref, buf):
        t = pl.program_id(0)
        _stage_causal(buf, t, halo, tt)
        for r0 in range(0, tt, ch):
            rows = pl.ds(r0, ch)
            pv = ua_ref[rows, :] * _sigmoid(ug_ref[rows, :])
            p_ref[rows, :] = pv
            buf[pl.ds(halo + r0, ch), :] = pv
        for cols in _col_blocks(D):
            for r0 in range(0, tt, rc):
                z_ref[pl.ds(r0, rc), cols] = _causal_taps(buf, w_ref, taps, halo, r0, rc, cols) + b_ref[:, cols]
        for r0 in range(0, tt, ch):
            rows = pl.ds(r0, ch)
            zv = z_ref[rows, :]
            zc = zv - jnp.mean(zv, axis=-1, keepdims=True)
            l = zc * lax.rsqrt(jnp.mean(zc * zc, axis=-1, keepdims=True) + LN_EPS) * g_ref[...] + lb_ref[...]
            q_ref[rows, :] = (l * _sigmoid(l)).astype(BF16)

    blk = pl.BlockSpec((tt, D), lambda t: (t, 0))
    vec = pl.BlockSpec((1, D), lambda t: (0, 0))
    td = jax.ShapeDtypeStruct((T, D), F32)
    return pl.pallas_call(
        body, name=name, grid=(T // tt,),
        in_specs=[blk, blk, pl.BlockSpec((taps, D), lambda t: (0, 0)), vec, vec, vec], out_specs=[blk, blk, blk],
        out_shape=[td, td, jax.ShapeDtypeStruct((T, D), BF16)],
        scratch_shapes=[pltpu.VMEM((halo + tt, D), F32)], compiler_params=_params("arbitrary"),
    )(ua, ug, w, b, ln_g, ln_b)


def _c_gate_fwd(gb, gc, v, w, *, name):
    T, D = gb.shape
    taps, halo = w.shape[0], 8
    tt, tc, rc = _halo_tiles(T, D, halo, 1024)

    def body(gb_ref, gc_ref, v_ref, w_ref, p_ref, cv_ref, s_ref, buf):
        t = pl.program_id(1)
        _stage_causal(buf, t, halo, tt)
        pv = gc_ref[...] * v_ref[...]
        p_ref[...] = pv
        buf[pl.ds(halo, tt), :] = pv
        for cols in _col_blocks(tc):
            for r0 in range(0, tt, rc):
                rows = pl.ds(r0, rc)
                cv = _causal_taps(buf, w_ref, taps, halo, r0, rc, cols)
                cv_ref[rows, cols] = cv
                s_ref[rows, cols] = (gb_ref[rows, cols] * cv).astype(BF16)

    blk = pl.BlockSpec((tt, tc), lambda j, t: (t, j))
    td = jax.ShapeDtypeStruct((T, D), F32)
    return pl.pallas_call(
        body, name=name, grid=(D // tc, T // tt),
        in_specs=[blk, blk, blk, pl.BlockSpec((taps, tc), lambda j, t: (0, j))], out_specs=[blk, blk, blk],
        out_shape=[td, td, jax.ShapeDtypeStruct((T, D), BF16)],
        scratch_shapes=[pltpu.VMEM((halo + tt, tc), F32)], compiler_params=_params("parallel", "arbitrary"),
    )(gb, gc, v, w)


def _pool_count(base, r0, rows, width, window):
    pos = lax.broadcasted_iota(jnp.int32, (rows, width), 0) + (base + r0 + 1)
    return jnp.minimum(pos, window).astype(F32)


def _pool_fwd(h, *, name):
    T, D = h.shape
    halo = _round_up(max(POOL_WINDOWS), 8)
    gd = D // len(POOL_WINDOWS)
    tt = _tile(T, 256, 16)
    assert tt >= halo
    rc = _tile(tt, 128, 16)

    def body(h_ref, o_ref, buf):
        t = pl.program_id(0)
        _stage_causal(buf, t, halo, tt)
        buf[pl.ds(halo, tt), :] = h_ref[...]
        for g, window in enumerate(POOL_WINDOWS):
            for cols_in_group in _col_blocks(gd):
                cols = pl.ds(g * gd + cols_in_group.start, cols_in_group.size)
                for r0 in range(0, tt, rc):
                    acc = None
                    for j in range(window):
                        term = buf[pl.ds(halo - j + r0, rc), cols]
                        acc = term if acc is None else acc + term
                    cnt = _pool_count(t * tt, r0, rc, cols.size, window)
                    o_ref[pl.ds(r0, rc), cols] = (acc / cnt - buf[pl.ds(halo + r0, rc), cols]).astype(BF16)

    blk = pl.BlockSpec((tt, D), lambda t: (t, 0))
    return pl.pallas_call(
        body, name=name, grid=(T // tt,), in_specs=[blk], out_specs=blk, out_shape=jax.ShapeDtypeStruct((T, D), BF16),
        scratch_shapes=[pltpu.VMEM((halo + tt, D), F32)], compiler_params=_params("arbitrary"),
    )(h)


def _pool_bwd(dp, *, name):
    T, D = dp.shape
    halo = _round_up(max(POOL_WINDOWS), 8)
    gd = D // len(POOL_WINDOWS)
    tt = _tile(T, 256, 16)
    assert tt >= halo
    rc = _tile(tt, 128, 16)
    nt_ = T // tt

    def body(dp_ref, o_ref, buf):
        t = pl.program_id(0)
        base = (nt_ - 1 - t) * tt
        _stage_anticausal(buf, t, halo, tt)
        for g, window in enumerate(POOL_WINDOWS):
            for cols_in_group in _col_blocks(gd):
                cols = pl.ds(g * gd + cols_in_group.start, cols_in_group.size)
                for r0 in range(0, tt, rc):
                    rows = pl.ds(r0, rc)
                    buf[rows, cols] = dp_ref[rows, cols] / _pool_count(base, r0, rc, cols.size, window)
                for r0 in range(0, tt, rc):
                    acc = None
                    for j in range(window):
                        term = buf[pl.ds(r0 + j, rc), cols]
                        acc = term if acc is None else acc + term
                    o_ref[pl.ds(r0, rc), cols] = acc - dp_ref[pl.ds(r0, rc), cols]

    blk = pl.BlockSpec((tt, D), lambda t: (nt_ - 1 - t, 0))
    return pl.pallas_call(
        body, name=name, grid=(nt_,), in_specs=[blk], out_specs=blk, out_shape=jax.ShapeDtypeStruct((T, D), F32),
        scratch_shapes=[pltpu.VMEM((tt + halo, D), F32)], compiler_params=_params("arbitrary"),
    )(dp)


def _sum_slabs(slabs, *, name):
    S, R, C = slabs.shape

    def body(s_ref, o_ref):
        acc = s_ref[0]
        for p in range(1, S):
            acc = acc + s_ref[p]
        o_ref[...] = acc

    return pl.pallas_call(body, name=name, out_shape=jax.ShapeDtypeStruct((R, C), F32),
                          compiler_params=_params())(slabs)


def _adamw(w, slabs, m, v, *, name):
    S, R, C = slabs.shape
    tr = _tile(R, max(8, (1 << 18) // (C * S)), 8)
    c1, c2 = 1.0 - ADAM_B1 ** ADAM_STEP, 1.0 - ADAM_B2 ** ADAM_STEP

    def body(w_ref, s_ref, m_ref, v_ref, g_ref, d_ref, nm_ref, nv_ref):
        g = s_ref[0]
        for p in range(1, S):
            g = g + s_ref[p]
        nm = ADAM_B1 * m_ref[...] + (1.0 - ADAM_B1) * g
        nv = ADAM_B2 * v_ref[...] + (1.0 - ADAM_B2) * (g * g)
        g_ref[...] = g
        nm_ref[...] = nm
        nv_ref[...] = nv
        d_ref[...] = -ADAM_LR * ((nm / c1) / (jnp.sqrt(nv / c2) + ADAM_EPS) + ADAM_WD * w_ref[...])

    blk = pl.BlockSpec((tr, C), lambda i: (i, 0))
    sd = jax.ShapeDtypeStruct((R, C), F32)
    return pl.pallas_call(
        body, name=name, grid=(R // tr,), in_specs=[blk, pl.BlockSpec((S, tr, C), lambda i: (0, i, 0)), blk, blk],
        out_specs=[blk] * 4, out_shape=[sd] * 4, compiler_params=_params("parallel"),
    )(w, slabs, m, v)


def _cols_natural(g):
    g = jnp.moveaxis(g, 0, -2)
    return g.reshape(g.shape[:-2] + (g.shape[-2] * g.shape[-1],))


def _rows_natural(g):
    g = jnp.moveaxis(g, 0, 1)
    return g.reshape((g.shape[0], g.shape[1] * g.shape[2]) + g.shape[3:])


def _col_slabs(full, n_shards=N_DEV):
    n = full.shape[-1] // n_shards
    return jnp.moveaxis(full.reshape(full.shape[:-1] + (n_shards, n)), -2, 0)


def _pad_last(a, n):
    return jnp.pad(a, [(0, 0)] * (a.ndim - 1) + [(0, n - a.shape[-1])])


def kernel(x, c, mod_w, mod_b, norm_pre_mix, norm_post_mix, norm_pre_ffn, norm_post_ffn, a_pw1_w, a_pw1_b, a_dw_w, a_dw_b, a_ln_g, a_ln_b, a_pw2_w, a_pw2_b, b_group_w, b_group_b, b_scale, c_in_w, c_conv_w, c_out_w, f_up_w, f_dw_w, f_dw_b, f_down_w, loss_target, m_mod_w, m_mod_b, m_norm_pre_mix, m_norm_post_mix, m_norm_pre_ffn, m_norm_post_ffn, m_a_pw1_w, m_a_pw1_b, m_a_dw_w, m_a_dw_b, m_a_ln_g, m_a_ln_b, m_a_pw2_w, m_a_pw2_b, m_b_group_w, m_b_group_b, m_b_scale, m_c_in_w, m_c_conv_w, m_c_out_w, m_f_up_w, m_f_dw_w, m_f_dw_b, m_f_down_w, v_mod_w, v_mod_b, v_norm_pre_mix, v_norm_post_mix, v_norm_pre_ffn, v_norm_post_ffn, v_a_pw1_w, v_a_pw1_b, v_a_dw_w, v_a_dw_b, v_a_ln_g, v_a_ln_b, v_a_pw2_w, v_a_pw2_b, v_b_group_w, v_b_group_b, v_b_scale, v_c_in_w, v_c_conv_w, v_c_out_w, v_f_up_w, v_f_dw_w, v_f_dw_b, v_f_down_w):
    p = dict(locals())
    assert list(p) == INPUTS
    x0 = p['x'][0]
    target = p['loss_target'][0]
    T, D = x0.shape
    L = p['mod_w'].shape[0]
    G = len(POOL_WINDOWS)
    gd = D // G
    ns = p['f_up_w'].shape[-1]
    npad = _round_up(ns, LANE)
    Fp = 4 * npad
    me = 4 * lax.axis_index("x") + 2 * lax.axis_index("y") + lax.axis_index("c")
    row = lambda a, i: a[i:i + 1]

    big = ['a_pw1_w', 'a_pw2_w', 'b_group_w', 'c_in_w', 'c_out_w', 'f_down_w']
    small = ['a_pw1_b', 'a_dw_w', 'a_dw_b', 'a_ln_g', 'a_ln_b', 'a_pw2_b', 'c_conv_w']
    send = [p[k].astype(BF16) for k in big] + [_pad_last(p['f_up_w'], npad).astype(BF16)]
    send += [p[k] for k in small] + [_pad_last(p['f_dw_w'], npad), p['c']]
    got = _exchange(send, scatter=False, name="gather_weights")
    gw = dict(zip(big + ['f_up_w'] + small + ['f_dw_w', 'c'], got))

    w1 = _cols_natural(gw['a_pw1_w'])
    b1 = _cols_natural(gw['a_pw1_b'])
    a_dw_w = _cols_natural(gw['a_dw_w'])
    a_dw_b, a_ln_g, a_ln_b, b2 = (_cols_natural(gw[k]) for k in ('a_dw_b', 'a_ln_g', 'a_ln_b', 'a_pw2_b'))
    w2 = _rows_natural(gw['a_pw2_w'])
    bgw = jnp.moveaxis(gw['b_group_w'], 0, 2).reshape((-1, G, gd, gd))
    cin = _cols_natural(gw['c_in_w'])
    cconv = _cols_natural(gw['c_conv_w'])
    cout = _rows_natural(gw['c_out_w'])
    wup_g, wup_v = _cols_natural(gw['f_up_w'][:4]), _cols_natural(gw['f_up_w'][4:])
    fdw_g, fdw_v = _cols_natural(gw['f_dw_w'][:4]), _cols_natural(gw['f_dw_w'][4:])
    pad_ff = lambda a: _pad_last(a.reshape(a.shape[:-1] + (4, ns)), npad).reshape(a.shape[:-1] + (Fp,))
    fdb_g, fdb_v = pad_ff(p['f_dw_b'][:, :4 * ns]), pad_ff(p['f_dw_b'][:, 4 * ns:])
    wdown = _rows_natural(gw['f_down_w'])
    wdown = jnp.pad(wdown.reshape(L, 4, ns, D), ((0, 0), (0, 0), (0, npad - ns), (0, 0))).reshape(L, Fp, D)

    c_all = gw['c'].reshape(N_DEV, D)
    n6 = p['mod_w'].shape[-1]
    mod_part = _mod_fwd(c_all, p['mod_w'], lax.dynamic_slice_in_dim(p['mod_b'], me * n6, n6, axis=1), name="mod_fwd")
    mod_all = _exchange([mod_part], scatter=False, name="gather_mod")[0]
    mod_all = jnp.moveaxis(mod_all, 0, 2).reshape(L, N_DEV, N_DEV * n6)
    mod = lax.dynamic_index_in_dim(mod_all, me, axis=1, keepdims=False).reshape(L, 6, D)

    saved = []
    x = x0
    h = _prenorm_fwd(x, row(p['norm_pre_mix'], 0), row(mod[0], 1), row(mod[0], 0), BF16, name="prenorm_first")
    dx = loss_cols = None
    for l in range(L):
        kind, slot = l % N_MIXERS, l // N_MIXERS
        sh_m, sc_m, gt_m, sh_f, sc_f, gt_f = (row(mod[l], i) for i in range(6))
        s = dict(x=x, h=h)
        if kind == 0:
            ua = _mm(h, w1[slot, :, :D], bias=b1[slot:slot + 1, :D], name=f"a_pw1_a_{l}")
            ug = _mm(h, w1[slot, :, D:], bias=b1[slot:slot + 1, D:], name=f"a_pw1_g_{l}")
            pa, z, q = _a_conv_fwd(ua, ug, a_dw_w[slot], a_dw_b[slot:slot + 1], a_ln_g[slot:slot + 1],
                                   a_ln_b[slot:slot + 1], name=f"a_conv_fwd_{l}")
            y = _mm(q, w2[slot], bias=b2[slot:slot + 1], name=f"a_pw2_{l}")
            s.update(ua=ua, ug=ug, p=pa, z=z, q=q)
        elif kind == 1:
            pooled = _pool_fwd(h, name=f"pool_fwd_{l}")
            mb, y = _group_mm(pooled, bgw[slot], bias=p['b_group_b'][slot:slot + 1],
                              scale=p['b_scale'][slot:slot + 1], name=f"b_mix_{l}")
            s.update(pooled=pooled, mb=mb)
        else:
            gb, gc, v = (_mm(h, cin[slot, :, i * D:(i + 1) * D], name=f"c_in_{i}_{l}") for i in range(3))
            pc, cv, sg = _c_gate_fwd(gb, gc, v, cconv[slot], name=f"c_gate_fwd_{l}")
            y = _mm(sg, cout[slot], name=f"c_out_{l}")
            s.update(gb=gb, gc=gc, v=v, p=pc, cv=cv, sg=sg)
        x1, h2 = _post_pre_fwd(x, y, row(p['norm_post_mix'], l), gt_m, row(p['norm_pre_ffn'], l), sc_f, sh_f, BF16,
                               name=f"post_mix_{l}")
        fug = _mm(h2, wup_g[l], name=f"f_up_g_{l}")
        fuv = _mm(h2, wup_v[l], name=f"f_up_v_{l}")
        act = _ffn_act_fwd(fug, fuv, fdw_g[l], fdw_v[l], fdb_g[l:l + 1], fdb_v[l:l + 1], name=f"ffn_act_fwd_{l}")
        y2 = _mm(act, wdown[l], name=f"f_down_{l}")
        s.update(y=y, x1=x1, h2=h2, fug=fug, fuv=fuv, act=act, y2=y2)
        saved.append(s)
        if l + 1 < L:
            nxt_dtype = F32 if (l + 1) % N_MIXERS == 1 else BF16
            x, h = _post_pre_fwd(x1, y2, row(p['norm_post_ffn'], l), gt_f, row(p['norm_pre_mix'], l + 1),
                                 row(mod[l + 1], 1), row(mod[l + 1], 0), nxt_dtype, name=f"post_ffn_{l}")
        else:
            dx, loss_cols = _post_loss(x1, y2, row(p['norm_post_ffn'], l), gt_f, target, name="post_loss")
    loss = lax.psum(0.5 * jnp.sum(loss_cols) / D, AXES)

    gfull = {}
    for k in ('a_pw1_w', 'a_pw1_b', 'a_dw_w', 'a_dw_b', 'a_ln_g', 'a_ln_b', 'a_pw2_w', 'a_pw2_b', 'b_group_w',
              'c_in_w', 'c_conv_w', 'c_out_w', 'f_up_w', 'f_dw_w', 'f_down_w'):
        gfull[k] = {}
    dmod, dnorm, d_fdb = {}, {}, {}
    d_bgb = d_bscale = None
    for l in reversed(range(L)):
        kind, slot = l % N_MIXERS, l // N_MIXERS
        s = saved[l]
        sh_m, sc_m, gt_m, sh_f, sc_f, gt_f = (row(mod[l], i) for i in range(6))
        dy2, dg4, dgt_f, _ = _post_bwd(s['y2'], dx, row(p['norm_post_ffn'], l), gt_f, BF16, name=f"post_ffn_bwd_{l}")
        da = _mm(dy2, wdown[l], nt=True, name=f"f_down_dx_{l}")
        dwd = _mm(_transpose(s['act'], name=f"act_t_{l}"), dy2, name=f"f_down_dw_{l}")
        gfull['f_down_w'][l] = dwd.reshape(4, npad, D)[:, :ns].reshape(N_DEV, ns // 2, D)
        dcg, dcv, dbg, dbv = _ffn_act_bwd(s['fug'], s['fuv'], fdw_g[l], fdw_v[l], fdb_g[l:l + 1], fdb_v[l:l + 1], da,
                                          name=f"ffn_act_bwd_{l}")
        dug, dwg = _dwconv_bwd(s['fug'], dcg, fdw_g[l], BF16, name=f"f_dw_bwd_g_{l}")
        duv, dwv = _dwconv_bwd(s['fuv'], dcv, fdw_v[l], BF16, name=f"f_dw_bwd_v_{l}")
        dh2 = _mm(dug, wup_g[l], nt=True, name=f"f_up_dx_g_{l}")
        dh2 = _mm(duv, wup_v[l], nt=True, add=dh2, name=f"f_up_dx_v_{l}")
        h2t = _transpose(s['h2'], name=f"h2_t_{l}")
        gfull['f_up_w'][l] = jnp.concatenate([_mm(h2t, dug, shard_cols=npad, name=f"f_up_dw_g_{l}"),
                                              _mm(h2t, duv, shard_cols=npad, name=f"f_up_dw_v_{l}")], axis=0)
        gfull['f_dw_w'][l] = jnp.concatenate([_col_slabs(dwg, 4), _col_slabs(dwv, 4)], axis=0)
        unpad = lambda a: a.reshape(4, npad)[:, :ns].reshape(4 * ns)
        d_fdb[l] = jnp.concatenate([unpad(dbg), unpad(dbv)])
        dx1, dsh_f, dsc_f, dg3 = _pre_bwd(s['x1'], dh2, dx, row(p['norm_pre_ffn'], l), sc_f, name=f"pre_ffn_bwd_{l}")
        dy_dtype = F32 if kind == 1 else BF16
        dy, dg2, dgt_m, dy_sum = _post_bwd(s['y'], dx1, row(p['norm_post_mix'], l), gt_m, dy_dtype,
                                           name=f"post_mix_bwd_{l}")
        if kind == 0:
            dq = _mm(dy, w2[slot], nt=True, name=f"a_pw2_dx_{l}")
            gfull['a_pw2_w'][slot] = _mm(_transpose(s['q'], name=f"q_t_{l}"), dy, name=f"a_pw2_dw_{l}").reshape(
                N_DEV, D // N_DEV, D)
            gfull['a_pw2_b'][slot] = dy_sum.reshape(N_DEV, D // N_DEV)
            dz, dlg, dlb, dzs = _a_ln_bwd(s['z'], dq, a_ln_g[slot:slot + 1], a_ln_b[slot:slot + 1], name=f"a_ln_bwd_{l}")
            gfull['a_ln_g'][slot], gfull['a_ln_b'][slot], gfull['a_dw_b'][slot] = (
                t.reshape(N_DEV, D // N_DEV) for t in (dlg, dlb, dzs))
            dp, ddw = _dwconv_bwd(s['p'], dz, a_dw_w[slot], F32, name=f"a_dw_bwd_{l}")
            gfull['a_dw_w'][slot] = _col_slabs(ddw)
            dua, dug_a, sua, sug = _a_glu_bwd(s['ua'], s['ug'], dp, name=f"a_glu_bwd_{l}")
            gfull['a_pw1_b'][slot] = jnp.concatenate([sua, sug], axis=1).reshape(N_DEV, 2 * D // N_DEV)
            dh = _mm(dua, w1[slot, :, :D], nt=True, name=f"a_pw1_dx_a_{l}")
            dh = _mm(dug_a, w1[slot, :, D:], nt=True, add=dh, name=f"a_pw1_dx_g_{l}")
            ht = _transpose(s['h'], name=f"h_t_{l}")
            n1 = 2 * D // N_DEV
            gfull['a_pw1_w'][slot] = jnp.concatenate([_mm(ht, dua, shard_cols=n1, name=f"a_pw1_dw_a_{l}"),
                                                      _mm(ht, dug_a, shard_cols=n1, name=f"a_pw1_dw_g_{l}")], axis=0)
        elif kind == 1:
            dm, d_bscale, d_bgb = _b_scale_bwd(dy, s['mb'], p['b_scale'][slot:slot + 1], name=f"b_scale_bwd_{l}")
            dpool = _group_mm(dm, bgw[slot], nt=True, name=f"b_mix_dx_{l}")
            dgw = _group_dw(_transpose(s['pooled'], name=f"pooled_t_{l}"), dm, G, name=f"b_mix_dw_{l}")
            gfull['b_group_w'][slot] = jnp.moveaxis(dgw.reshape(G, N_DEV, gd // N_DEV, gd), 1, 0)
            dh = _pool_bwd(dpool, name=f"pool_bwd_{l}")
        else:
            dsg = _mm(dy, cout[slot], nt=True, name=f"c_out_dx_{l}")
            gfull['c_out_w'][slot] = _mm(_transpose(s['sg'], name=f"sg_t_{l}"), dy, name=f"c_out_dw_{l}").reshape(
                N_DEV, D // N_DEV, D)
            dgb, dcv_c = _c_gate_bwd1(s['gb'], s['cv'], dsg, name=f"c_gate_bwd1_{l}")
            dpc, dcw = _dwconv_bwd(s['p'], dcv_c, cconv[slot], F32, name=f"c_conv_bwd_{l}")
            gfull['c_conv_w'][slot] = _col_slabs(dcw)
            dgc, dv = _c_gate_bwd2(s['gc'], s['v'], dpc, name=f"c_gate_bwd2_{l}")
            ht = _transpose(s['h'], name=f"h_t_{l}")
            dh, parts = None, []
            for i, d_i in enumerate((dgb, dgc, dv)):
                dh = _mm(d_i, cin[slot, :, i * D:(i + 1) * D], nt=True, add=dh, name=f"c_in_dx_{i}_{l}")
                parts.append(_mm(ht, d_i, name=f"c_in_dw_{i}_{l}"))
            gfull['c_in_w'][slot] = _col_slabs(jnp.concatenate(parts, axis=1))
        dx, dsh_m, dsc_m, dg1 = _pre_bwd(s['x'], dh, dx1, row(p['norm_pre_mix'], l), sc_m, name=f"pre_mix_bwd_{l}")
        dmod[l] = jnp.concatenate([dsh_m, dsc_m, dgt_m, dsh_f, dsc_f, dgt_f], axis=0)
        dnorm[l] = (dg1, dg2, dg3, dg4)

    zero_row = jnp.zeros((1, D), F32)
    rows_ = [dmod[l] for l in range(L)]
    rows_ += [dnorm[l][i] for i in range(4) for l in range(L)]
    rows_ += [d_bgb if d_bgb is not None else zero_row, d_bscale if d_bscale is not None else zero_row]
    fdb_flat = jnp.concatenate([d_fdb[l] for l in range(L)])
    rows_ += [_pad_last(fdb_flat, _round_up(fdb_flat.size, D)).reshape(-1, D)]
    pack = jnp.concatenate(rows_, axis=0)
    pack = jnp.pad(pack, ((0, _round_up(pack.shape[0], 8) - pack.shape[0]), (0, 0)))
    pack_all = _exchange([pack], scatter=False, name="gather_small_grads")[0]
    red = _sum_slabs(pack_all, name="sum_small_grads")
    g_rep = {'mod_b': red[:6 * L].reshape(L, 6 * D)}
    for i, k in enumerate(('norm_pre_mix', 'norm_post_mix', 'norm_pre_ffn', 'norm_post_ffn')):
        g_rep[k] = red[6 * L + i * L:6 * L + (i + 1) * L]
    o = 6 * L + 4 * L
    g_rep['b_group_b'], g_rep['b_scale'] = red[o:o + 1], red[o + 1:o + 2]
    g_rep['f_dw_b'] = red[o + 2:].reshape(-1)[:L * 8 * ns].reshape(L, 8 * ns)

    dmod_all = pack_all[:, :6 * L].reshape(N_DEV, L, 6 * D)
    dmod_mine = jnp.moveaxis(lax.dynamic_slice_in_dim(dmod_all, me * n6, n6, axis=2), 0, 1)
    g_mod_w = _mod_bwd(c_all.T, dmod_mine, name="mod_bwd")

    stack = lambda d: jnp.stack([d[i] for i in sorted(d)], axis=1)
    sharded = [k for k in gfull if gfull[k]]
    recv = _exchange([stack(gfull[k]) for k in sharded], scatter=True, name="scatter_grads")
    slabs = dict(zip(sharded, recv))

    out = {}
    for k in WEIGHTS:
        w, m, v = p[k], p['m_' + k], p['v_' + k]
        if k in g_rep:
            g = g_rep[k].reshape((1,) + w.shape)
        elif k == 'mod_w':
            g = g_mod_w.reshape((1,) + w.shape)
        else:
            g = slabs[k]
        if k in ('f_up_w', 'f_dw_w'):
            w, m, v = (_pad_last(t, npad) for t in (w, m, v))
        shape = w.shape
        two_d = (-1, shape[-1])
        res = _adamw(w.reshape(two_d), g.reshape((g.shape[0],) + (w.size // shape[-1], shape[-1])), m.reshape(two_d),
                     v.reshape(two_d), name=f"adamw_{k}")
        res = [t.reshape(shape) for t in res]
        if k in ('f_up_w', 'f_dw_w'):
            res = [t[..., :ns] for t in res]
        out[k] = res

    grads, deltas, new_m, new_v = ([out[k][i] for k in WEIGHTS] for i in range(4))
    return (loss, dx.reshape(1, T, D), *grads, *deltas, *new_m, *new_v)
```

```python
import functools

import jax
import jax.numpy as jnp
from jax import lax
from jax.experimental import pallas as pl
from jax.experimental.pallas import tpu as pltpu

F32, BF16 = jnp.float32, jnp.bfloat16
AXES = ("x", "y", "c")
N_DEV = 8
LANE = 128
VMEM_LIMIT = 48 * 1024 * 1024
RMS_EPS, LN_EPS = 1e-6, 1e-5
POOL_WINDOWS = (2, 4, 8, 16)
N_MIXERS = 3
ADAM_LR, ADAM_B1, ADAM_B2, ADAM_EPS, ADAM_WD, ADAM_STEP = 0.001, 0.9, 0.999, 1e-08, 0.01, 10

WEIGHTS = ['mod_w', 'mod_b', 'norm_pre_mix', 'norm_post_mix', 'norm_pre_ffn', 'norm_post_ffn', 'a_pw1_w', 'a_pw1_b',
           'a_dw_w', 'a_dw_b', 'a_ln_g', 'a_ln_b', 'a_pw2_w', 'a_pw2_b', 'b_group_w', 'b_group_b', 'b_scale', 'c_in_w',
           'c_conv_w', 'c_out_w', 'f_up_w', 'f_dw_w', 'f_dw_b', 'f_down_w']
INPUTS = (['x', 'c'] + WEIGHTS + ['loss_target'] + ['m_' + w for w in WEIGHTS] + ['v_' + w for w in WEIGHTS])


def _tile(n, cap, mult):
    for t in range(min(n, cap), 0, -1):
        if n % t == 0 and t % mult == 0:
            return t
    return n


def _round_up(n, m):
    return -(-n // m) * m


def _params(*dims):
    return pltpu.CompilerParams(dimension_semantics=dims or None, vmem_limit_bytes=VMEM_LIMIT)


def _sigmoid(v):
    return jax.nn.sigmoid(v)


def _exchange(arrs, *, scatter, name):
    n = len(arrs)

    def body(*refs):
        ins, outs = refs[:n], refs[n:2 * n]
        send_sems, recv_sems, local_sems = refs[2 * n:]
        x, y, c = (lax.axis_index(a) for a in AXES)
        me = 4 * x + 2 * y + c

        def peer(k):
            px = 1 - x if (k >> 2) & 1 else x
            py = 1 - y if (k >> 1) & 1 else y
            pc = 1 - c if k & 1 else c
            return (px, py, pc), 4 * px + 2 * py + pc

        def remote(i, k, src, dst):
            return pltpu.make_async_remote_copy(
                src_ref=src, dst_ref=dst, send_sem=send_sems.at[i, k - 1], recv_sem=recv_sems.at[i, k - 1],
                device_id=peer(k)[0], device_id_type=pl.DeviceIdType.MESH)

        own = []
        for i in range(n):
            cp = pltpu.make_async_copy(ins[i].at[me] if scatter else ins[i], outs[i].at[me], local_sems.at[i])
            cp.start()
            own.append(cp)
        sent = []
        for k in range(1, N_DEV):
            for i in range(n):
                cp = remote(i, k, ins[i].at[peer(k)[1]] if scatter else ins[i], outs[i].at[me])
                cp.start()
                sent.append(cp)
        for k in range(1, N_DEV):
            for i in range(n):
                landing = outs[i].at[peer(k)[1]]
                remote(i, k, landing, landing).wait_recv()
        for cp in sent:
            cp.wait_send()
        for cp in own:
            cp.wait()

    any_spec = pl.BlockSpec(memory_space=pl.ANY)
    out_shape = [jax.ShapeDtypeStruct(a.shape if scatter else (N_DEV,) + a.shape, a.dtype) for a in arrs]
    return pl.pallas_call(
        body, name=name, out_shape=out_shape, in_specs=[any_spec] * n, out_specs=[any_spec] * n,
        scratch_shapes=[pltpu.SemaphoreType.DMA((n, N_DEV - 1)), pltpu.SemaphoreType.DMA((n, N_DEV - 1)),
                        pltpu.SemaphoreType.DMA((n,))],
    )(*arrs)


def _comm_call(body, arrs, out_shape, sem_shapes, name):
    any_spec = pl.BlockSpec(memory_space=pl.ANY)
    return pl.pallas_call(
        body, name=name, out_shape=out_shape, in_specs=[any_spec] * len(arrs), out_specs=[any_spec] * len(out_shape),
        scratch_shapes=[pltpu.SemaphoreType.DMA(s) for s in sem_shapes],
    )(*arrs)


def _gather_two_level(arrs, *, name):
    n = len(arrs)

    def body(*refs):
        ins, outs = refs[:n], refs[n:2 * n]
        send_sems, recv_sems, local_sems = refs[2 * n:]
        x, y, c = (lax.axis_index(a) for a in AXES)
        index = lambda px, py, pc: 4 * px + 2 * py + pc
        me, sibling = index(x, y, c), (x, y, 1 - c)
        chips = [(1 - x, y), (x, 1 - y), (1 - x, 1 - y)]

        def copy(i, k, block, to, src=None):
            return pltpu.make_async_remote_copy(
                src_ref=outs[i].at[block] if src is None else src, dst_ref=outs[i].at[block],
                send_sem=send_sems.at[i, k], recv_sem=recv_sems.at[i, k], device_id=to,
                device_id_type=pl.DeviceIdType.MESH)

        own = [pltpu.make_async_copy(ins[i], outs[i].at[me], local_sems.at[i]) for i in range(n)]
        first = [copy(i, 1 + j, me, (*chip, c), src=ins[i]) for j, chip in enumerate(chips) for i in range(n)]
        first += [copy(i, 0, me, sibling, src=ins[i]) for i in range(n)]
        for cp in own + first:
            cp.start()
        passed = []
        for j, chip in enumerate(chips):
            block = index(*chip, c)
            for i in range(n):
                copy(i, 1 + j, block, (x, y, c)).wait_recv()
            for i in range(n):
                cp = copy(i, 4 + j, block, sibling)
                cp.start()
                passed.append(cp)
        for i in range(n):
            copy(i, 0, index(x, y, 1 - c), (x, y, c)).wait_recv()
        for j, chip in enumerate(chips):
            for i in range(n):
                copy(i, 4 + j, index(*chip, 1 - c), (x, y, c)).wait_recv()
        for cp in first + passed:
            cp.wait_send()
        for cp in own:
            cp.wait()

    out_shape = [jax.ShapeDtypeStruct((N_DEV,) + a.shape, a.dtype) for a in arrs]
    return _comm_call(body, arrs, out_shape, [(n, 7), (n, 7), (n,)], name)


def _swap_cores(arrs, *, name):
    n = len(arrs)

    def body(*refs):
        ins, kept, got = refs[:n], refs[n:2 * n], refs[2 * n:3 * n]
        send_sems, recv_sems, local_sems = refs[3 * n:]
        x, y, c = (lax.axis_index(a) for a in AXES)
        local, remote = [], []
        for q in range(4):
            for i in range(n):
                local.append(pltpu.make_async_copy(ins[i].at[2 * q + c], kept[i].at[q], local_sems.at[i, q]))
                remote.append(pltpu.make_async_remote_copy(
                    src_ref=ins[i].at[2 * q + 1 - c], dst_ref=got[i].at[q], send_sem=send_sems.at[i, q],
                    recv_sem=recv_sems.at[i, q], device_id=(x, y, 1 - c), device_id_type=pl.DeviceIdType.MESH))
        for cp in local + remote:
            cp.start()
        for cp in remote:
            cp.wait_recv()
        for cp in remote:
            cp.wait_send()
        for cp in local:
            cp.wait()

    half = [jax.ShapeDtypeStruct((4,) + a.shape[1:], a.dtype) for a in arrs]
    res = _comm_call(body, arrs, half + half, [(n, 4), (n, 4), (n, 4)], name)
    return res[:n], res[n:]


def _swap_chips(arrs, *, name):
    n = len(arrs)

    def body(*refs):
        ins, outs = refs[:n], refs[n:2 * n]
        send_sems, recv_sems, local_sems = refs[2 * n:]
        x, y, c = (lax.axis_index(a) for a in AXES)
        mine = 2 * x + y

        def peer(k):
            px = 1 - x if k & 2 else x
            py = 1 - y if k & 1 else y
            return (px, py, c), 2 * px + py

        def remote(i, k, src, dst):
            return pltpu.make_async_remote_copy(
                src_ref=src, dst_ref=dst, send_sem=send_sems.at[i, k - 1], recv_sem=recv_sems.at[i, k - 1],
                device_id=peer(k)[0], device_id_type=pl.DeviceIdType.MESH)

        own = [pltpu.make_async_copy(ins[i].at[mine], outs[i].at[mine], local_sems.at[i]) for i in range(n)]
        sent = [remote(i, k, ins[i].at[peer(k)[1]], outs[i].at[mine]) for k in range(1, 4) for i in range(n)]
        for cp in own + sent:
            cp.start()
        for k in range(1, 4):
            for i in range(n):
                landing = outs[i].at[peer(k)[1]]
                remote(i, k, landing, landing).wait_recv()
        for cp in sent:
            cp.wait_send()
        for cp in own:
            cp.wait()

    out_shape = [jax.ShapeDtypeStruct(a.shape, a.dtype) for a in arrs]
    return _comm_call(body, arrs, out_shape, [(n, 3), (n, 3), (n,)], name)


def _add_pair(a, b, *, name):
    C = a.shape[-1]
    R = a.size // C
    tr = _tile(R, max(16, (1 << 19) // C), 16)

    def body(a_ref, b_ref, o_ref):
        o_ref[...] = (a_ref[...].astype(F32) + b_ref[...].astype(F32)).astype(o_ref.dtype)

    blk = pl.BlockSpec((tr, C), lambda i: (i, 0))
    return pl.pallas_call(
        body, name=name, grid=(R // tr,), in_specs=[blk, blk], out_specs=blk,
        out_shape=jax.ShapeDtypeStruct((R, C), a.dtype), compiler_params=_params("parallel"),
    )(a.reshape(R, C), b.reshape(R, C)).reshape(a.shape)


def _mm(a, b, *, name, nt=False, bias=None, add=None, out_dtype=F32, shard_cols=0):
    M, K = a.shape
    N = b.shape[0] if nt else b.shape[1]
    tm, tn, tk = _tile(M, 1024, 16), shard_cols or _tile(N, 1024, LANE), _tile(K, 1024, LANE)
    nk = K // tk
    dims = (((1,), (1,)), ((), ())) if nt else (((1,), (0,)), ((), ()))

    def body(*refs):
        refs = list(refs)
        a_ref, b_ref = refs.pop(0), refs.pop(0)
        bias_ref = refs.pop(0) if bias is not None else None
        add_ref = refs.pop(0) if add is not None else None
        o_ref = refs.pop(0)
        part = lax.dot_general(a_ref[...], b_ref[...], dims, preferred_element_type=F32)

        def finish(val):
            if bias_ref is not None:
                val = val + bias_ref[...]
            if add_ref is not None:
                val = val + add_ref[...]
            o_ref[...] = val.astype(o_ref.dtype)

        if nk == 1:
            finish(part)
        else:
            acc_ref = refs.pop(0)
            k = pl.program_id(2)

            @pl.when(k == 0)
            def _():
                acc_ref[...] = part

            @pl.when(k > 0)
            def _():
                acc_ref[...] += part

            @pl.when(k == nk - 1)
            def _():
                finish(acc_ref[...])

    in_specs = [pl.BlockSpec((tm, tk), lambda i, j, k: (i, k)),
                pl.BlockSpec((tn, tk), lambda i, j, k: (j, k)) if nt else pl.BlockSpec((tk, tn), lambda i, j, k: (k, j))]
    operands = [a, b]
    if bias is not None:
        in_specs.append(pl.BlockSpec((1, tn), lambda i, j, k: (0, j)))
        operands.append(bias)
    if add is not None:
        in_specs.append(pl.BlockSpec((tm, tn), lambda i, j, k: (i, j)))
        operands.append(add)
    if shard_cols:
        out_shape = jax.ShapeDtypeStruct((N // tn, M, tn), out_dtype)
        out_spec = pl.BlockSpec((None, tm, tn), lambda i, j, k: (j, i, 0))
    else:
        out_shape = jax.ShapeDtypeStruct((M, N), out_dtype)
        out_spec = pl.BlockSpec((tm, tn), lambda i, j, k: (i, j))
    return pl.pallas_call(
        body, name=name, grid=(M // tm, N // tn, nk), in_specs=in_specs, out_specs=out_spec, out_shape=out_shape,
        scratch_shapes=[pltpu.VMEM((tm, tn), F32)] if nk > 1 else [],
        compiler_params=_params("parallel", "parallel", "arbitrary"),
    )(*operands)


def _transpose(x, *, name):
    T, C = x.shape
    tt, tc = _tile(T, 512, LANE), _tile(C, 512, LANE)

    def body(x_ref, o_ref):
        o_ref[...] = x_ref[...].astype(F32).T.astype(BF16)

    return pl.pallas_call(
        body, name=name, grid=(T // tt, C // tc), in_specs=[pl.BlockSpec((tt, tc), lambda i, j: (i, j))],
        out_specs=pl.BlockSpec((tc, tt), lambda i, j: (j, i)), out_shape=jax.ShapeDtypeStruct((C, T), BF16),
        compiler_params=_params("parallel", "parallel"),
    )(x)


def _group_mm(a, w, *, name, nt=False, bias=None, scale=None):
    T, D = a.shape
    G, gd, _ = w.shape
    tm = _tile(T, 1024, 16)
    dims = (((1,), (1,)), ((), ())) if nt else (((1,), (0,)), ((), ()))
    fused = bias is not None

    def body(*refs):
        if fused:
            a_ref, w_ref, b_ref, s_ref, mb_ref, y_ref = refs
        else:
            a_ref, w_ref, y_ref = refs
        val = lax.dot_general(a_ref[...], w_ref[...], dims, preferred_element_type=F32)
        if fused:
            val = val + b_ref[...]
            mb_ref[...] = val
            val = val * s_ref[...]
        y_ref[...] = val

    blk = pl.BlockSpec((tm, gd), lambda i, g: (i, g))
    vec = pl.BlockSpec((1, gd), lambda i, g: (0, g))
    in_specs = [blk, pl.BlockSpec((None, gd, gd), lambda i, g: (g, 0, 0))] + ([vec, vec] if fused else [])
    td = jax.ShapeDtypeStruct((T, D), F32)
    return pl.pallas_call(
        body, name=name, grid=(T // tm, G), in_specs=in_specs, out_specs=[blk, blk] if fused else blk,
        out_shape=[td, td] if fused else td, compiler_params=_params("parallel", "parallel"),
    )(*([a, w, bias, scale] if fused else [a, w]))


def _group_dw(at, b, groups, *, name):
    D, T = at.shape
    gd = D // groups
    tk = _tile(T, 2048, LANE)

    def body(a_ref, b_ref, o_ref):
        @pl.when(pl.program_id(1) == 0)
        def _():
            o_ref[...] = jnp.zeros_like(o_ref)

        o_ref[...] += jnp.dot(a_ref[...], b_ref[...], preferred_element_type=F32)

    return pl.pallas_call(
        body, name=name, grid=(groups, T // tk),
        in_specs=[pl.BlockSpec((gd, tk), lambda g, k: (g, k)), pl.BlockSpec((tk, gd), lambda g, k: (k, g))],
        out_specs=pl.BlockSpec((None, gd, gd), lambda g, k: (g, 0, 0)),
        out_shape=jax.ShapeDtypeStruct((groups, gd, gd), F32), compiler_params=_params("parallel", "arbitrary"),
    )(at, b)


def _mod_fwd(c_all, w, b, *, name):
    L, D, n = w.shape
    B = c_all.shape[0]

    def body(c_ref, w_ref, b_ref, o_ref):
        cv = c_ref[...]
        o_ref[...] = jnp.dot(cv * _sigmoid(cv), w_ref[...], preferred_element_type=F32) + b_ref[...]

    return pl.pallas_call(
        body, name=name, grid=(L,),
        in_specs=[pl.BlockSpec((B, D), lambda l: (0, 0)), pl.BlockSpec((None, D, n), lambda l: (l, 0, 0)),
                  pl.BlockSpec((None, 1, n), lambda l: (l, 0, 0))],
        out_specs=pl.BlockSpec((None, B, n), lambda l: (l, 0, 0)), out_shape=jax.ShapeDtypeStruct((L, B, n), F32),
        compiler_params=_params("parallel"),
    )(c_all, w, b.reshape(L, 1, n))


def _mod_bwd(c_all_t, dmod, *, name):
    L, B, n = dmod.shape
    D = c_all_t.shape[0]

    def body(c_ref, d_ref, o_ref):
        cv = c_ref[...]
        o_ref[...] = jnp.dot(cv * _sigmoid(cv), d_ref[...], preferred_element_type=F32)

    return pl.pallas_call(
        body, name=name, grid=(L,),
        in_specs=[pl.BlockSpec((D, B), lambda l: (0, 0)), pl.BlockSpec((None, B, n), lambda l: (l, 0, 0))],
        out_specs=pl.BlockSpec((None, D, n), lambda l: (l, 0, 0)), out_shape=jax.ShapeDtypeStruct((L, D, n), F32),
        compiler_params=_params("parallel"),
    )(c_all_t, dmod)


def _rowwise(fn, tiles, vecs, out_dtypes, n_sums, *, name):
    T, C = tiles[0].shape
    tr = _tile(T, 256, 16)
    ch = _tile(tr, 32, 16)
    nt_, nv, no = len(tiles), len(vecs), len(out_dtypes)

    def body(*refs):
        t_refs, v_refs = refs[:nt_], refs[nt_:nt_ + nv]
        o_refs, s_refs = refs[nt_ + nv:nt_ + nv + no], refs[nt_ + nv + no:]
        vv = [v[...] for v in v_refs]
        acc = [None] * n_sums
        for r0 in range(0, tr, ch):
            rows = pl.ds(r0, ch)
            o_vals, s_vals = fn([t[rows, :] for t in t_refs], vv)
            for o, val in zip(o_refs, o_vals):
                o[rows, :] = val.astype(o.dtype)
            for i, val in enumerate(s_vals):
                part = jnp.sum(val, axis=0, keepdims=True)
                acc[i] = part if acc[i] is None else acc[i] + part
        if n_sums:
            first = pl.program_id(0) == 0

            @pl.when(first)
            def _():
                for s, val in zip(s_refs, acc):
                    s[...] = val

            @pl.when(jnp.logical_not(first))
            def _():
                for s, val in zip(s_refs, acc):
                    s[...] += val

    tile_spec = pl.BlockSpec((tr, C), lambda i: (i, 0))
    vec_spec = pl.BlockSpec((1, C), lambda i: (0, 0))
    res = pl.pallas_call(
        body, name=name, grid=(T // tr,), in_specs=[tile_spec] * nt_ + [vec_spec] * nv,
        out_specs=[tile_spec] * no + [vec_spec] * n_sums,
        out_shape=[jax.ShapeDtypeStruct((T, C), d) for d in out_dtypes] + [jax.ShapeDtypeStruct((1, C), F32)] * n_sums,
        compiler_params=_params("arbitrary"),
    )(*tiles, *vecs)
    return res


def _rms(v):
    return lax.rsqrt(jnp.mean(v * v, axis=-1, keepdims=True) + RMS_EPS)


def _prenorm_fwd(x, g, sc, sh, out_dtype, *, name):
    def fn(t, v):
        (xv,), (g_, sc_, sh_) = t, v
        return [(xv * _rms(xv)) * g_ * (1.0 + sc_) + sh_], []

    return _rowwise(fn, [x], [g, sc, sh], [out_dtype], 0, name=name)[0]


def _post_pre_fwd(x, y, gp, gate, g, sc, sh, out_dtype, *, name):
    def fn(t, v):
        (xv, yv), (gp_, gate_, g_, sc_, sh_) = t, v
        x1 = xv + gate_ * ((yv * _rms(yv)) * gp_)
        return [x1, (x1 * _rms(x1)) * g_ * (1.0 + sc_) + sh_], []

    return _rowwise(fn, [x, y], [gp, gate, g, sc, sh], [F32, out_dtype], 0, name=name)


def _post_loss(x, y, gp, gate, target, *, name):
    D = x.shape[1]

    def fn(t, v):
        (xv, yv, tv), (gp_, gate_) = t, v
        err = xv + gate_ * ((yv * _rms(yv)) * gp_) - tv
        return [err / D], [err * err]

    return _rowwise(fn, [x, y, target], [gp, gate], [F32], 1, name=name)


def _post_bwd(y, dxo, gp, gate, out_dtype, *, name):
    def fn(t, v):
        (yv, dv), (gp_, gate_) = t, v
        r = _rms(yv)
        yn = yv * r
        dyn = dv * gate_ * gp_
        dy = r * (dyn - yn * jnp.mean(dyn * yn, axis=-1, keepdims=True))
        return [dy], [dv * gate_ * yn, dv * yn * gp_, dy]

    return _rowwise(fn, [y, dxo], [gp, gate], [out_dtype], 3, name=name)


def _pre_bwd(x, dh, dres, g, sc, *, name):
    def fn(t, v):
        (xv, dhv, dr), (g_, sc_) = t, v
        dhv = dhv.astype(F32)
        r = _rms(xv)
        xn = xv * r
        dxn = dhv * g_ * (1.0 + sc_)
        dx = dr + r * (dxn - xn * jnp.mean(dxn * xn, axis=-1, keepdims=True))
        return [dx], [dhv, dhv * xn * g_, dhv * xn * (1.0 + sc_)]

    return _rowwise(fn, [x, dh, dres], [g, sc], [F32], 3, name=name)


def _a_ln_bwd(z, dq, g, b, *, name):
    def fn(t, v):
        (zv, dqv), (g_, b_) = t, v
        mu = jnp.mean(zv, axis=-1, keepdims=True)
        zc = zv - mu
        rstd = lax.rsqrt(jnp.mean(zc * zc, axis=-1, keepdims=True) + LN_EPS)
        zn = zc * rstd
        l = zn * g_ + b_
        s = _sigmoid(l)
        dl = dqv * (s * (1.0 + l * (1.0 - s)))
        dzn = dl * g_
        dz = rstd * (dzn - jnp.mean(dzn, axis=-1, keepdims=True) - zn * jnp.mean(dzn * zn, axis=-1, keepdims=True))
        return [dz], [dl * zn, dl, dz]

    return _rowwise(fn, [z, dq], [g, b], [F32], 3, name=name)


def _a_glu_bwd(ua, ug, dp, *, name):
    def fn(t, v):
        uav, ugv, dpv = t
        s = _sigmoid(ugv)
        dua = dpv * s
        dug = dpv * uav * s * (1.0 - s)
        return [dua, dug], [dua, dug]

    return _rowwise(fn, [ua, ug, dp], [], [BF16, BF16], 2, name=name)


def _b_scale_bwd(dy, mb, scale, *, name):
    def fn(t, v):
        (dyv, mbv), (s_,) = t, v
        dm = dyv * s_
        return [dm], [dyv * mbv, dm]

    return _rowwise(fn, [dy, mb], [scale], [BF16], 2, name=name)


def _c_gate_bwd1(gb, cv, ds, *, name):
    def fn(t, v):
        gbv, cvv, dsv = t
        return [dsv * cvv, dsv * gbv], []

    return _rowwise(fn, [gb, cv, ds], [], [BF16, F32], 0, name=name)


def _c_gate_bwd2(gc, v_, dp, *, name):
    def fn(t, v):
        gcv, vv, dpv = t
        return [dpv * vv, dpv * gcv], []

    return _rowwise(fn, [gc, v_, dp], [], [BF16, BF16], 0, name=name)


def _stage_causal(buf, t, halo, tt):
    @pl.when(t == 0)
    def _():
        buf[pl.ds(0, halo), :] = jnp.zeros((halo, buf.shape[1]), F32)

    @pl.when(t > 0)
    def _():
        buf[pl.ds(0, halo), :] = buf[pl.ds(tt, halo), :]


def _stage_anticausal(buf, t, halo, tt):
    @pl.when(t == 0)
    def _():
        buf[pl.ds(tt, halo), :] = jnp.zeros((halo, buf.shape[1]), F32)

    @pl.when(t > 0)
    def _():
        buf[pl.ds(tt, halo), :] = buf[pl.ds(0, halo), :]


def _causal_taps(buf, w_ref, taps, halo, r0, rows, cols):
    acc = None
    for k in range(taps):
        term = w_ref[pl.ds(k, 1), cols] * buf[pl.ds(halo - (taps - 1) + k + r0, rows), cols]
        acc = term if acc is None else acc + term
    return acc


def _halo_tiles(T, C, halo, col_cap):
    tt = _tile(T, 512 if halo <= 8 else 256, 16)
    assert tt >= halo, (tt, halo)
    return tt, _tile(C, col_cap, LANE), _tile(tt, 128, 16)


def _col_blocks(tc):
    cb = LANE if tc % LANE == 0 else tc
    return [pl.ds(c0, cb) for c0 in range(0, tc, cb)]


def _ffn_act_fwd(ug, uv, wg, wv, bg, bv, *, name):
    T, F = ug.shape
    taps, halo = wg.shape[0], 8
    tt, tc, rc = _halo_tiles(T, F, halo, 1024)

    def body(ug_ref, uv_ref, wg_ref, wv_ref, bg_ref, bv_ref, a_ref, bufg, bufv):
        t = pl.program_id(1)
        for u_ref, buf in ((ug_ref, bufg), (uv_ref, bufv)):
            _stage_causal(buf, t, halo, tt)
            buf[pl.ds(halo, tt), :] = u_ref[...]
        for cols in _col_blocks(tc):
            for r0 in range(0, tt, rc):
                yg = _causal_taps(bufg, wg_ref, taps, halo, r0, rc, cols) + bg_ref[:, cols]
                yv = _causal_taps(bufv, wv_ref, taps, halo, r0, rc, cols) + bv_ref[:, cols]
                a_ref[pl.ds(r0, rc), cols] = (yg * _sigmoid(yg) * yv).astype(BF16)

    blk = pl.BlockSpec((tt, tc), lambda j, t: (t, j))
    wspec = pl.BlockSpec((taps, tc), lambda j, t: (0, j))
    vspec = pl.BlockSpec((1, tc), lambda j, t: (0, j))
    return pl.pallas_call(
        body, name=name, grid=(F // tc, T // tt), in_specs=[blk, blk, wspec, wspec, vspec, vspec], out_specs=blk,
        out_shape=jax.ShapeDtypeStruct((T, F), BF16),
        scratch_shapes=[pltpu.VMEM((halo + tt, tc), F32), pltpu.VMEM((halo + tt, tc), F32)],
        compiler_params=_params("parallel", "arbitrary"),
    )(ug, uv, wg, wv, bg, bv)


def _ffn_act_bwd(ug, uv, wg, wv, bg, bv, da, *, name):
    T, F = ug.shape
    taps, halo = wg.shape[0], 8
    tt, tc, rc = _halo_tiles(T, F, halo, 1024)

    def body(ug_ref, uv_ref, wg_ref, wv_ref, bg_ref, bv_ref, da_ref, dg_ref, dv_ref, sg_ref, sv_ref, bufg, bufv):
        t = pl.program_id(1)

        @pl.when(t == 0)
        def _():
            sg_ref[...] = jnp.zeros_like(sg_ref)
            sv_ref[...] = jnp.zeros_like(sv_ref)

        for u_ref, buf in ((ug_ref, bufg), (uv_ref, bufv)):
            _stage_causal(buf, t, halo, tt)
            buf[pl.ds(halo, tt), :] = u_ref[...]
        for cols in _col_blocks(tc):
            for r0 in range(0, tt, rc):
                rows = pl.ds(r0, rc)
                yg = _causal_taps(bufg, wg_ref, taps, halo, r0, rc, cols) + bg_ref[:, cols]
                yv = _causal_taps(bufv, wv_ref, taps, halo, r0, rc, cols) + bv_ref[:, cols]
                s = _sigmoid(yg)
                d = da_ref[rows, cols]
                dyg = d * yv * (s * (1.0 + yg * (1.0 - s)))
                dyv = d * (yg * s)
                dg_ref[rows, cols] = dyg
                dv_ref[rows, cols] = dyv
                sg_ref[:, cols] += jnp.sum(dyg, axis=0, keepdims=True)
                sv_ref[:, cols] += jnp.sum(dyv, axis=0, keepdims=True)

    blk = pl.BlockSpec((tt, tc), lambda j, t: (t, j))
    wspec = pl.BlockSpec((taps, tc), lambda j, t: (0, j))
    vspec = pl.BlockSpec((1, tc), lambda j, t: (0, j))
    td, vd = jax.ShapeDtypeStruct((T, F), F32), jax.ShapeDtypeStruct((1, F), F32)
    return pl.pallas_call(
        body, name=name, grid=(F // tc, T // tt), in_specs=[blk, blk, wspec, wspec, vspec, vspec, blk],
        out_specs=[blk, blk, vspec, vspec], out_shape=[td, td, vd, vd],
        scratch_shapes=[pltpu.VMEM((halo + tt, tc), F32), pltpu.VMEM((halo + tt, tc), F32)],
        compiler_params=_params("parallel", "arbitrary"),
    )(ug, uv, wg, wv, bg, bv, da)


def _dwconv_bwd(x, dy, w, out_dtype, *, name):
    T, C = x.shape
    taps = w.shape[0]
    halo = _round_up(taps - 1, 8)
    tt, tc, rc = _halo_tiles(T, C, halo, 512)
    nt_ = T // tt

    def body(x_ref, dy_ref, w_ref, dx_ref, dw_ref, buf):
        t = pl.program_id(1)

        @pl.when(t == 0)
        def _():
            dw_ref[...] = jnp.zeros_like(dw_ref)

        _stage_anticausal(buf, t, halo, tt)
        buf[pl.ds(0, tt), :] = dy_ref[...]
        for cols in _col_blocks(tc):
            for r0 in range(0, tt, rc):
                rows = pl.ds(r0, rc)
                xv = x_ref[rows, cols]
                acc = None
                for j in range(taps):
                    k = taps - 1 - j
                    ahead = buf[pl.ds(r0 + j, rc), cols]
                    term = w_ref[pl.ds(k, 1), cols] * ahead
                    acc = term if acc is None else acc + term
                    dw_ref[pl.ds(k, 1), cols] += jnp.sum(xv * ahead, axis=0, keepdims=True)
                dx_ref[rows, cols] = acc.astype(dx_ref.dtype)

    blk = pl.BlockSpec((tt, tc), lambda j, t: (nt_ - 1 - t, j))
    wspec = pl.BlockSpec((taps, tc), lambda j, t: (0, j))
    return pl.pallas_call(
        body, name=name, grid=(C // tc, nt_), in_specs=[blk, blk, wspec], out_specs=[blk, wspec],
        out_shape=[jax.ShapeDtypeStruct((T, C), out_dtype), jax.ShapeDtypeStruct((taps, C), F32)],
        scratch_shapes=[pltpu.VMEM((tt + halo, tc), F32)], compiler_params=_params("parallel", "arbitrary"),
    )(x, dy, w)


def _a_conv_fwd(ua, ug, w, b, ln_g, ln_b, *, name):
    T, D = ua.shape
    taps = w.shape[0]
    halo = _round_up(taps - 1, 8)
    tt = _tile(T, 128, 16)
    assert tt >= halo
    rc, ch = _tile(tt, 128, 16), _tile(tt, 32, 16)

    def body(ua_ref, ug_ref, w_ref, b_ref, g_ref, lb_ref, p_ref, z_ref, q_ref, buf):
        t = pl.program_id(0)
        _stage_causal(buf, t, halo, tt)
        for r0 in range(0, tt, ch):
            rows = pl.ds(r0, ch)
            pv = ua_ref[rows, :] * _sigmoid(ug_ref[rows, :])
            p_ref[rows, :] = pv
            buf[pl.ds(halo + r0, ch), :] = pv
        for cols in _col_blocks(D):
            for r0 in range(0, tt, rc):
                z_ref[pl.ds(r0, rc), cols] = _causal_taps(buf, w_ref, taps, halo, r0, rc, cols) + b_ref[:, cols]
        for r0 in range(0, tt, ch):
            rows = pl.ds(r0, ch)
            zv = z_ref[rows, :]
            zc = zv - jnp.mean(zv, axis=-1, keepdims=True)
            l = zc * lax.rsqrt(jnp.mean(zc * zc, axis=-1, keepdims=True) + LN_EPS) * g_ref[...] + lb_ref[...]
            q_ref[rows, :] = (l * _sigmoid(l)).astype(BF16)

    blk = pl.BlockSpec((tt, D), lambda t: (t, 0))
    vec = pl.BlockSpec((1, D), lambda t: (0, 0))
    td = jax.ShapeDtypeStruct((T, D), F32)
    return pl.pallas_call(
        body, name=name, grid=(T // tt,),
        in_specs=[blk, blk, pl.BlockSpec((taps, D), lambda t: (0, 0)), vec, vec, vec], out_specs=[blk, blk, blk],
        out_shape=[td, td, jax.ShapeDtypeStruct((T, D), BF16)],
        scratch_shapes=[pltpu.VMEM((halo + tt, D), F32)], compiler_params=_params("arbitrary"),
    )(ua, ug, w, b, ln_g, ln_b)


def _c_gate_fwd(gb, gc, v, w, *, name):
    T, D = gb.shape
    taps, halo = w.shape[0], 8
    tt, tc, rc = _halo_tiles(T, D, halo, 1024)

    def body(gb_ref, gc_ref, v_ref, w_ref, p_ref, cv_ref, s_ref, buf):
        t = pl.program_id(1)
        _stage_causal(buf, t, halo, tt)
        pv = gc_ref[...] * v_ref[...]
        p_ref[...] = pv
        buf[pl.ds(halo, tt), :] = pv
        for cols in _col_blocks(tc):
            for r0 in range(0, tt, rc):
                rows = pl.ds(r0, rc)
                cv = _causal_taps(buf, w_ref, taps, halo, r0, rc, cols)
                cv_ref[rows, cols] = cv
                s_ref[rows, cols] = (gb_ref[rows, cols] * cv).astype(BF16)

    blk = pl.BlockSpec((tt, tc), lambda j, t: (t, j))
    td = jax.ShapeDtypeStruct((T, D), F32)
    return pl.pallas_call(
        body, name=name, grid=(D // tc, T // tt),
        in_specs=[blk, blk, blk, pl.BlockSpec((taps, tc), lambda j, t: (0, j))], out_specs=[blk, blk, blk],
        out_shape=[td, td, jax.ShapeDtypeStruct((T, D), BF16)],
        scratch_shapes=[pltpu.VMEM((halo + tt, tc), F32)], compiler_params=_params("parallel", "arbitrary"),
    )(gb, gc, v, w)


def _pool_count(base, r0, rows, width, window):
    pos = lax.broadcasted_iota(jnp.int32, (rows, width), 0) + (base + r0 + 1)
    return jnp.minimum(pos, window).astype(F32)


def _pool_fwd(h, *, name):
    T, D = h.shape
    halo = _round_up(max(POOL_WINDOWS), 8)
    gd = D // len(POOL_WINDOWS)
    tt = _tile(T, 256, 16)
    assert tt >= halo
    rc = _tile(tt, 128, 16)

    def body(h_ref, o_ref, buf):
        t = pl.program_id(0)
        _stage_causal(buf, t, halo, tt)
        buf[pl.ds(halo, tt), :] = h_ref[...]
        for g, window in enumerate(POOL_WINDOWS):
            for cols_in_group in _col_blocks(gd):
                cols = pl.ds(g * gd + cols_in_group.start, cols_in_group.size)
                for r0 in range(0, tt, rc):
                    acc = None
                    for j in range(window):
                        term = buf[pl.ds(halo - j + r0, rc), cols]
                        acc = term if acc is None else acc + term
                    cnt = _pool_count(t * tt, r0, rc, cols.size, window)
                    o_ref[pl.ds(r0, rc), cols] = (acc / cnt - buf[pl.ds(halo + r0, rc), cols]).astype(BF16)

    blk = pl.BlockSpec((tt, D), lambda t: (t, 0))
    return pl.pallas_call(
        body, name=name, grid=(T // tt,), in_specs=[blk], out_specs=blk, out_shape=jax.ShapeDtypeStruct((T, D), BF16),
        scratch_shapes=[pltpu.VMEM((halo + tt, D), F32)], compiler_params=_params("arbitrary"),
    )(h)


def _pool_bwd(dp, *, name):
    T, D = dp.shape
    halo = _round_up(max(POOL_WINDOWS), 8)
    gd = D // len(POOL_WINDOWS)
    tt = _tile(T, 256, 16)
    assert tt >= halo
    rc = _tile(tt, 128, 16)
    nt_ = T // tt

    def body(dp_ref, o_ref, buf):
        t = pl.program_id(0)
        base = (nt_ - 1 - t) * tt
        _stage_anticausal(buf, t, halo, tt)
        for g, window in enumerate(POOL_WINDOWS):
            for cols_in_group in _col_blocks(gd):
                cols = pl.ds(g * gd + cols_in_group.start, cols_in_group.size)
                for r0 in range(0, tt, rc):
                    rows = pl.ds(r0, rc)
                    buf[rows, cols] = dp_ref[rows, cols] / _pool_count(base, r0, rc, cols.size, window)
                for r0 in range(0, tt, rc):
                    acc = None
                    for j in range(window):
                        term = buf[pl.ds(r0 + j, rc), cols]
                        acc = term if acc is None else acc + term
                    o_ref[pl.ds(r0, rc), cols] = acc - dp_ref[pl.ds(r0, rc), cols]

    blk = pl.BlockSpec((tt, D), lambda t: (nt_ - 1 - t, 0))
    return pl.pallas_call(
        body, name=name, grid=(nt_,), in_specs=[blk], out_specs=blk, out_shape=jax.ShapeDtypeStruct((T, D), F32),
        scratch_shapes=[pltpu.VMEM((tt + halo, D), F32)], compiler_params=_params("arbitrary"),
    )(dp)


def _sum_slabs(slabs, *, name):
    S, R, C = slabs.shape

    def body(s_ref, o_ref):
        acc = s_ref[0]
        for p in range(1, S):
            acc = acc + s_ref[p]
        o_ref[...] = acc

    return pl.pallas_call(body, name=name, out_shape=jax.ShapeDtypeStruct((R, C), F32),
                          compiler_params=_params())(slabs)


def _adamw(w, slabs, m, v, *, name):
    S, R, C = slabs.shape
    sub = 16 if slabs.dtype == BF16 else 8
    tr = _tile(R, max(sub, (1 << 18) // (C * S)), sub)
    c1, c2 = 1.0 - ADAM_B1 ** ADAM_STEP, 1.0 - ADAM_B2 ** ADAM_STEP

    def body(w_ref, s_ref, m_ref, v_ref, g_ref, d_ref, nm_ref, nv_ref):
        g = s_ref[0].astype(F32)
        for p in range(1, S):
            g = g + s_ref[p].astype(F32)
        nm = ADAM_B1 * m_ref[...] + (1.0 - ADAM_B1) * g
        nv = ADAM_B2 * v_ref[...] + (1.0 - ADAM_B2) * (g * g)
        g_ref[...] = g
        nm_ref[...] = nm
        nv_ref[...] = nv
        d_ref[...] = -ADAM_LR * ((nm / c1) / (jnp.sqrt(nv / c2) + ADAM_EPS) + ADAM_WD * w_ref[...])

    blk = pl.BlockSpec((tr, C), lambda i: (i, 0))
    sd = jax.ShapeDtypeStruct((R, C), F32)
    return pl.pallas_call(
        body, name=name, grid=(R // tr,), in_specs=[blk, pl.BlockSpec((S, tr, C), lambda i: (0, i, 0)), blk, blk],
        out_specs=[blk] * 4, out_shape=[sd] * 4, compiler_params=_params("parallel"),
    )(w, slabs, m, v)


def _cols_natural(g):
    g = jnp.moveaxis(g, 0, -2)
    return g.reshape(g.shape[:-2] + (g.shape[-2] * g.shape[-1],))


def _rows_natural(g):
    g = jnp.moveaxis(g, 0, 1)
    return g.reshape((g.shape[0], g.shape[1] * g.shape[2]) + g.shape[3:])


def _col_slabs(full, n_shards=N_DEV):
    n = full.shape[-1] // n_shards
    return jnp.moveaxis(full.reshape(full.shape[:-1] + (n_shards, n)), -2, 0)


def _pad_last(a, n):
    return jnp.pad(a, [(0, 0)] * (a.ndim - 1) + [(0, n - a.shape[-1])])


def kernel(x, c, mod_w, mod_b, norm_pre_mix, norm_post_mix, norm_pre_ffn, norm_post_ffn, a_pw1_w, a_pw1_b, a_dw_w, a_dw_b, a_ln_g, a_ln_b, a_pw2_w, a_pw2_b, b_group_w, b_group_b, b_scale, c_in_w, c_conv_w, c_out_w, f_up_w, f_dw_w, f_dw_b, f_down_w, loss_target, m_mod_w, m_mod_b, m_norm_pre_mix, m_norm_post_mix, m_norm_pre_ffn, m_norm_post_ffn, m_a_pw1_w, m_a_pw1_b, m_a_dw_w, m_a_dw_b, m_a_ln_g, m_a_ln_b, m_a_pw2_w, m_a_pw2_b, m_b_group_w, m_b_group_b, m_b_scale, m_c_in_w, m_c_conv_w, m_c_out_w, m_f_up_w, m_f_dw_w, m_f_dw_b, m_f_down_w, v_mod_w, v_mod_b, v_norm_pre_mix, v_norm_post_mix, v_norm_pre_ffn, v_norm_post_ffn, v_a_pw1_w, v_a_pw1_b, v_a_dw_w, v_a_dw_b, v_a_ln_g, v_a_ln_b, v_a_pw2_w, v_a_pw2_b, v_b_group_w, v_b_group_b, v_b_scale, v_c_in_w, v_c_conv_w, v_c_out_w, v_f_up_w, v_f_dw_w, v_f_dw_b, v_f_down_w):
    p = dict(locals())
    assert list(p) == INPUTS
    x0 = p['x'][0]
    target = p['loss_target'][0]
    T, D = x0.shape
    L = p['mod_w'].shape[0]
    G = len(POOL_WINDOWS)
    gd = D // G
    ns = p['f_up_w'].shape[-1]
    npad = _round_up(ns, LANE)
    Fp = 4 * npad
    me = 4 * lax.axis_index("x") + 2 * lax.axis_index("y") + lax.axis_index("c")
    row = lambda a, i: a[i:i + 1]

    big = ['a_pw1_w', 'a_pw2_w', 'b_group_w', 'c_in_w', 'c_out_w', 'f_down_w']
    small = ['a_pw1_b', 'a_dw_w', 'a_dw_b', 'a_ln_g', 'a_ln_b', 'a_pw2_b', 'c_conv_w']
    send = [p[k].astype(BF16) for k in big] + [_pad_last(p['f_up_w'], npad).astype(BF16)]
    send += [p[k] for k in small] + [_pad_last(p['f_dw_w'], npad), p['c']]
    got = _gather_two_level(send, name="gather_weights")
    gw = dict(zip(big + ['f_up_w'] + small + ['f_dw_w', 'c'], got))

    w1 = _cols_natural(gw['a_pw1_w'])
    b1 = _cols_natural(gw['a_pw1_b'])
    a_dw_w = _cols_natural(gw['a_dw_w'])
    a_dw_b, a_ln_g, a_ln_b, b2 = (_cols_natural(gw[k]) for k in ('a_dw_b', 'a_ln_g', 'a_ln_b', 'a_pw2_b'))
    w2 = _rows_natural(gw['a_pw2_w'])
    bgw = jnp.moveaxis(gw['b_group_w'], 0, 2).reshape((-1, G, gd, gd))
    cin = _cols_natural(gw['c_in_w'])
    cconv = _cols_natural(gw['c_conv_w'])
    cout = _rows_natural(gw['c_out_w'])
    wup_g, wup_v = _cols_natural(gw['f_up_w'][:4]), _cols_natural(gw['f_up_w'][4:])
    fdw_g, fdw_v = _cols_natural(gw['f_dw_w'][:4]), _cols_natural(gw['f_dw_w'][4:])
    pad_ff = lambda a: _pad_last(a.reshape(a.shape[:-1] + (4, ns)), npad).reshape(a.shape[:-1] + (Fp,))
    fdb_g, fdb_v = pad_ff(p['f_dw_b'][:, :4 * ns]), pad_ff(p['f_dw_b'][:, 4 * ns:])
    wdown = _rows_natural(gw['f_down_w'])
    wdown = jnp.pad(wdown.reshape(L, 4, ns, D), ((0, 0), (0, 0), (0, npad - ns), (0, 0))).reshape(L, Fp, D)

    c_all = gw['c'].reshape(N_DEV, D)
    n6 = p['mod_w'].shape[-1]
    mod_part = _mod_fwd(c_all, p['mod_w'], lax.dynamic_slice_in_dim(p['mod_b'], me * n6, n6, axis=1), name="mod_fwd")
    mod_all = _exchange([mod_part], scatter=False, name="gather_mod")[0]
    mod_all = jnp.moveaxis(mod_all, 0, 2).reshape(L, N_DEV, N_DEV * n6)
    mod = lax.dynamic_index_in_dim(mod_all, me, axis=1, keepdims=False).reshape(L, 6, D)

    saved = []
    x = x0
    h = _prenorm_fwd(x, row(p['norm_pre_mix'], 0), row(mod[0], 1), row(mod[0], 0), BF16, name="prenorm_first")
    dx = loss_cols = None
    for l in range(L):
        kind, slot = l % N_MIXERS, l // N_MIXERS
        sh_m, sc_m, gt_m, sh_f, sc_f, gt_f = (row(mod[l], i) for i in range(6))
        s = dict(x=x, h=h)
        if kind == 0:
            ua = _mm(h, w1[slot, :, :D], bias=b1[slot:slot + 1, :D], name=f"a_pw1_a_{l}")
            ug = _mm(h, w1[slot, :, D:], bias=b1[slot:slot + 1, D:], name=f"a_pw1_g_{l}")
            pa, z, q = _a_conv_fwd(ua, ug, a_dw_w[slot], a_dw_b[slot:slot + 1], a_ln_g[slot:slot + 1],
                                   a_ln_b[slot:slot + 1], name=f"a_conv_fwd_{l}")
            y = _mm(q, w2[slot], bias=b2[slot:slot + 1], name=f"a_pw2_{l}")
            s.update(ua=ua, ug=ug, p=pa, z=z, q=q)
        elif kind == 1:
            pooled = _pool_fwd(h, name=f"pool_fwd_{l}")
            mb, y = _group_mm(pooled, bgw[slot], bias=p['b_group_b'][slot:slot + 1],
                              scale=p['b_scale'][slot:slot + 1], name=f"b_mix_{l}")
            s.update(pooled=pooled, mb=mb)
        else:
            gb, gc, v = (_mm(h, cin[slot, :, i * D:(i + 1) * D], name=f"c_in_{i}_{l}") for i in range(3))
            pc, cv, sg = _c_gate_fwd(gb, gc, v, cconv[slot], name=f"c_gate_fwd_{l}")
            y = _mm(sg, cout[slot], name=f"c_out_{l}")
            s.update(gb=gb, gc=gc, v=v, p=pc, cv=cv, sg=sg)
        x1, h2 = _post_pre_fwd(x, y, row(p['norm_post_mix'], l), gt_m, row(p['norm_pre_ffn'], l), sc_f, sh_f, BF16,
                               name=f"post_mix_{l}")
        fug = _mm(h2, wup_g[l], name=f"f_up_g_{l}")
        fuv = _mm(h2, wup_v[l], name=f"f_up_v_{l}")
        act = _ffn_act_fwd(fug, fuv, fdw_g[l], fdw_v[l], fdb_g[l:l + 1], fdb_v[l:l + 1], name=f"ffn_act_fwd_{l}")
        y2 = _mm(act, wdown[l], name=f"f_down_{l}")
        s.update(y=y, x1=x1, h2=h2, fug=fug, fuv=fuv, act=act, y2=y2)
        saved.append(s)
        if l + 1 < L:
            nxt_dtype = F32 if (l + 1) % N_MIXERS == 1 else BF16
            x, h = _post_pre_fwd(x1, y2, row(p['norm_post_ffn'], l), gt_f, row(p['norm_pre_mix'], l + 1),
                                 row(mod[l + 1], 1), row(mod[l + 1], 0), nxt_dtype, name=f"post_ffn_{l}")
        else:
            dx, loss_cols = _post_loss(x1, y2, row(p['norm_post_ffn'], l), gt_f, target, name="post_loss")
    loss = lax.psum(0.5 * jnp.sum(loss_cols) / D, AXES)

    gfull = {}
    for k in ('a_pw1_w', 'a_pw1_b', 'a_dw_w', 'a_dw_b', 'a_ln_g', 'a_ln_b', 'a_pw2_w', 'a_pw2_b', 'b_group_w',
              'c_in_w', 'c_conv_w', 'c_out_w', 'f_up_w', 'f_dw_w', 'f_down_w'):
        gfull[k] = {}
    dmod, dnorm, d_fdb = {}, {}, {}
    d_bgb = d_bscale = None
    for l in reversed(range(L)):
        kind, slot = l % N_MIXERS, l // N_MIXERS
        s = saved[l]
        sh_m, sc_m, gt_m, sh_f, sc_f, gt_f = (row(mod[l], i) for i in range(6))
        dy2, dg4, dgt_f, _ = _post_bwd(s['y2'], dx, row(p['norm_post_ffn'], l), gt_f, BF16, name=f"post_ffn_bwd_{l}")
        da = _mm(dy2, wdown[l], nt=True, name=f"f_down_dx_{l}")
        dwd = _mm(_transpose(dy2, name=f"dy2_t_{l}"), s['act'], out_dtype=BF16, name=f"f_down_dw_{l}").T
        gfull['f_down_w'][l] = dwd.reshape(4, npad, D)[:, :ns].reshape(N_DEV, ns // 2, D)
        dcg, dcv, dbg, dbv = _ffn_act_bwd(s['fug'], s['fuv'], fdw_g[l], fdw_v[l], fdb_g[l:l + 1], fdb_v[l:l + 1], da,
                                          name=f"ffn_act_bwd_{l}")
        dug, dwg = _dwconv_bwd(s['fug'], dcg, fdw_g[l], BF16, name=f"f_dw_bwd_g_{l}")
        duv, dwv = _dwconv_bwd(s['fuv'], dcv, fdw_v[l], BF16, name=f"f_dw_bwd_v_{l}")
        dh2 = _mm(dug, wup_g[l], nt=True, name=f"f_up_dx_g_{l}")
        dh2 = _mm(duv, wup_v[l], nt=True, add=dh2, name=f"f_up_dx_v_{l}")
        h2t = _transpose(s['h2'], name=f"h2_t_{l}")
        gfull['f_up_w'][l] = jnp.concatenate(
            [_mm(h2t, dug, shard_cols=npad, out_dtype=BF16, name=f"f_up_dw_g_{l}"),
             _mm(h2t, duv, shard_cols=npad, out_dtype=BF16, name=f"f_up_dw_v_{l}")], axis=0)
        gfull['f_dw_w'][l] = jnp.concatenate([_col_slabs(dwg, 4), _col_slabs(dwv, 4)], axis=0)
        unpad = lambda a: a.reshape(4, npad)[:, :ns].reshape(4 * ns)
        d_fdb[l] = jnp.concatenate([unpad(dbg), unpad(dbv)])
        dx1, dsh_f, dsc_f, dg3 = _pre_bwd(s['x1'], dh2, dx, row(p['norm_pre_ffn'], l), sc_f, name=f"pre_ffn_bwd_{l}")
        dy_dtype = F32 if kind == 1 else BF16
        dy, dg2, dgt_m, dy_sum = _post_bwd(s['y'], dx1, row(p['norm_post_mix'], l), gt_m, dy_dtype,
                                           name=f"post_mix_bwd_{l}")
        if kind == 0:
            dq = _mm(dy, w2[slot], nt=True, name=f"a_pw2_dx_{l}")
            gfull['a_pw2_w'][slot] = _mm(_transpose(s['q'], name=f"q_t_{l}"), dy, out_dtype=BF16,
                                         name=f"a_pw2_dw_{l}").reshape(N_DEV, D // N_DEV, D)
            gfull['a_pw2_b'][slot] = dy_sum.reshape(N_DEV, D // N_DEV)
            dz, dlg, dlb, dzs = _a_ln_bwd(s['z'], dq, a_ln_g[slot:slot + 1], a_ln_b[slot:slot + 1], name=f"a_ln_bwd_{l}")
            gfull['a_ln_g'][slot], gfull['a_ln_b'][slot], gfull['a_dw_b'][slot] = (
                t.reshape(N_DEV, D // N_DEV) for t in (dlg, dlb, dzs))
            dp, ddw = _dwconv_bwd(s['p'], dz, a_dw_w[slot], F32, name=f"a_dw_bwd_{l}")
            gfull['a_dw_w'][slot] = _col_slabs(ddw)
            dua, dug_a, sua, sug = _a_glu_bwd(s['ua'], s['ug'], dp, name=f"a_glu_bwd_{l}")
            gfull['a_pw1_b'][slot] = jnp.concatenate([sua, sug], axis=1).reshape(N_DEV, 2 * D // N_DEV)
            dh = _mm(dua, w1[slot, :, :D], nt=True, name=f"a_pw1_dx_a_{l}")
            dh = _mm(dug_a, w1[slot, :, D:], nt=True, add=dh, name=f"a_pw1_dx_g_{l}")
            ht = _transpose(s['h'], name=f"h_t_{l}")
            n1 = 2 * D // N_DEV
            gfull['a_pw1_w'][slot] = jnp.concatenate(
                [_mm(ht, dua, shard_cols=n1, out_dtype=BF16, name=f"a_pw1_dw_a_{l}"),
                 _mm(ht, dug_a, shard_cols=n1, out_dtype=BF16, name=f"a_pw1_dw_g_{l}")], axis=0)
        elif kind == 1:
            dm, d_bscale, d_bgb = _b_scale_bwd(dy, s['mb'], p['b_scale'][slot:slot + 1], name=f"b_scale_bwd_{l}")
            dpool = _group_mm(dm, bgw[slot], nt=True, name=f"b_mix_dx_{l}")
            dgw = _group_dw(_transpose(s['pooled'], name=f"pooled_t_{l}"), dm, G, name=f"b_mix_dw_{l}")
            gfull['b_group_w'][slot] = jnp.moveaxis(dgw.astype(BF16).reshape(G, N_DEV, gd // N_DEV, gd), 1, 0)
            dh = _pool_bwd(dpool, name=f"pool_bwd_{l}")
        else:
            dsg = _mm(dy, cout[slot], nt=True, name=f"c_out_dx_{l}")
            gfull['c_out_w'][slot] = _mm(_transpose(s['sg'], name=f"sg_t_{l}"), dy, out_dtype=BF16,
                                         name=f"c_out_dw_{l}").reshape(N_DEV, D // N_DEV, D)
            dgb, dcv_c = _c_gate_bwd1(s['gb'], s['cv'], dsg, name=f"c_gate_bwd1_{l}")
            dpc, dcw = _dwconv_bwd(s['p'], dcv_c, cconv[slot], F32, name=f"c_conv_bwd_{l}")
            gfull['c_conv_w'][slot] = _col_slabs(dcw)
            dgc, dv = _c_gate_bwd2(s['gc'], s['v'], dpc, name=f"c_gate_bwd2_{l}")
            ht = _transpose(s['h'], name=f"h_t_{l}")
            dh, parts = None, []
            for i, d_i in enumerate((dgb, dgc, dv)):
                dh = _mm(d_i, cin[slot, :, i * D:(i + 1) * D], nt=True, add=dh, name=f"c_in_dx_{i}_{l}")
                parts.append(_mm(ht, d_i, out_dtype=BF16, name=f"c_in_dw_{i}_{l}"))
            gfull['c_in_w'][slot] = _col_slabs(jnp.concatenate(parts, axis=1))
        dx, dsh_m, dsc_m, dg1 = _pre_bwd(s['x'], dh, dx1, row(p['norm_pre_mix'], l), sc_m, name=f"pre_mix_bwd_{l}")
        dmod[l] = jnp.concatenate([dsh_m, dsc_m, dgt_m, dsh_f, dsc_f, dgt_f], axis=0)
        dnorm[l] = (dg1, dg2, dg3, dg4)

    zero_row = jnp.zeros((1, D), F32)
    rows_ = [dmod[l] for l in range(L)]
    rows_ += [dnorm[l][i] for i in range(4) for l in range(L)]
    rows_ += [d_bgb if d_bgb is not None else zero_row, d_bscale if d_bscale is not None else zero_row]
    fdb_flat = jnp.concatenate([d_fdb[l] for l in range(L)])
    rows_ += [_pad_last(fdb_flat, _round_up(fdb_flat.size, D)).reshape(-1, D)]
    pack = jnp.concatenate(rows_, axis=0)
    pack = jnp.pad(pack, ((0, _round_up(pack.shape[0], 8) - pack.shape[0]), (0, 0)))
    pack_all = _exchange([pack], scatter=False, name="gather_small_grads")[0]
    red = _sum_slabs(pack_all, name="sum_small_grads")
    g_rep = {'mod_b': red[:6 * L].reshape(L, 6 * D)}
    for i, k in enumerate(('norm_pre_mix', 'norm_post_mix', 'norm_pre_ffn', 'norm_post_ffn')):
        g_rep[k] = red[6 * L + i * L:6 * L + (i + 1) * L]
    o = 6 * L + 4 * L
    g_rep['b_group_b'], g_rep['b_scale'] = red[o:o + 1], red[o + 1:o + 2]
    g_rep['f_dw_b'] = red[o + 2:].reshape(-1)[:L * 8 * ns].reshape(L, 8 * ns)

    dmod_all = pack_all[:, :6 * L].reshape(N_DEV, L, 6 * D)
    dmod_mine = jnp.moveaxis(lax.dynamic_slice_in_dim(dmod_all, me * n6, n6, axis=2), 0, 1)
    g_mod_w = _mod_bwd(c_all.T, dmod_mine, name="mod_bwd")

    stack = lambda d: jnp.stack([d[i] for i in sorted(d)], axis=1)
    big_g = [k for k in gfull if gfull[k] and k.endswith('_w') and k not in ('a_dw_w', 'c_conv_w', 'f_dw_w')]
    small_g = [k for k in gfull if gfull[k] and k not in big_g]
    kept, got = _swap_cores([stack(gfull[k]) for k in big_g], name="scatter_grads_cores")
    partial = [_add_pair(a, b, name=f"add_core_slabs_{k}") for k, a, b in zip(big_g, kept, got)]
    slabs = dict(zip(big_g, _swap_chips(partial, name="scatter_grads_chips")))
    slabs.update(zip(small_g, _exchange([stack(gfull[k]) for k in small_g], scatter=True, name="scatter_small_grads")))

    out = {}
    for k in WEIGHTS:
        w, m, v = p[k], p['m_' + k], p['v_' + k]
        if k in g_rep:
            g = g_rep[k].reshape((1,) + w.shape)
        elif k == 'mod_w':
            g = g_mod_w.reshape((1,) + w.shape)
        else:
            g = slabs[k]
        if k in ('f_up_w', 'f_dw_w'):
            w, m, v = (_pad_last(t, npad) for t in (w, m, v))
        shape = w.shape
        two_d = (-1, shape[-1])
        res = _adamw(w.reshape(two_d), g.reshape((g.shape[0],) + (w.size // shape[-1], shape[-1])), m.reshape(two_d),
                     v.reshape(two_d), name=f"adamw_{k}")
        res = [t.reshape(shape) for t in res]
        if k in ('f_up_w', 'f_dw_w'):
            res = [t[..., :ns] for t in res]
        out[k] = res

    grads, deltas, new_m, new_v = ([out[k][i] for k in WEIGHTS] for i in range(4))
    return (loss, dx.reshape(1, T, D), *grads, *deltas, *new_m, *new_v)
```

```python
import functools

import jax
import jax.numpy as jnp
from jax import lax
from jax.experimental import pallas as pl
from jax.experimental.pallas import tpu as pltpu

F32, BF16 = jnp.float32, jnp.bfloat16
AXES = ("x", "y", "c")
N_DEV = 8
LANE = 128
VMEM_LIMIT = 48 * 1024 * 1024
RMS_EPS, LN_EPS = 1e-6, 1e-5
POOL_WINDOWS = (2, 4, 8, 16)
N_MIXERS = 3
ADAM_LR, ADAM_B1, ADAM_B2, ADAM_EPS, ADAM_WD, ADAM_STEP = 0.001, 0.9, 0.999, 1e-08, 0.01, 10

WEIGHTS = ['mod_w', 'mod_b', 'norm_pre_mix', 'norm_post_mix', 'norm_pre_ffn', 'norm_post_ffn', 'a_pw1_w', 'a_pw1_b',
           'a_dw_w', 'a_dw_b', 'a_ln_g', 'a_ln_b', 'a_pw2_w', 'a_pw2_b', 'b_group_w', 'b_group_b', 'b_scale', 'c_in_w',
           'c_conv_w', 'c_out_w', 'f_up_w', 'f_dw_w', 'f_dw_b', 'f_down_w']
INPUTS = (['x', 'c'] + WEIGHTS + ['loss_target'] + ['m_' + w for w in WEIGHTS] + ['v_' + w for w in WEIGHTS])


def _tile(n, cap, mult):
    for t in range(min(n, cap), 0, -1):
        if n % t == 0 and t % mult == 0:
            return t
    return n


def _round_up(n, m):
    return -(-n // m) * m


def _params(*dims):
    return pltpu.CompilerParams(dimension_semantics=dims or None, vmem_limit_bytes=VMEM_LIMIT)


def _sigmoid(v):
    return jax.nn.sigmoid(v)


def _exchange(arrs, *, scatter, name):
    n = len(arrs)

    def body(*refs):
        ins, outs = refs[:n], refs[n:2 * n]
        send_sems, recv_sems, local_sems = refs[2 * n:]
        x, y, c = (lax.axis_index(a) for a in AXES)
        me = 4 * x + 2 * y + c

        def peer(k):
            px = 1 - x if (k >> 2) & 1 else x
            py = 1 - y if (k >> 1) & 1 else y
            pc = 1 - c if k & 1 else c
            return (px, py, pc), 4 * px + 2 * py + pc

        def remote(i, k, src, dst):
            return pltpu.make_async_remote_copy(
                src_ref=src, dst_ref=dst, send_sem=send_sems.at[i, k - 1], recv_sem=recv_sems.at[i, k - 1],
                device_id=peer(k)[0], device_id_type=pl.DeviceIdType.MESH)

        own = []
        for i in range(n):
            cp = pltpu.make_async_copy(ins[i].at[me] if scatter else ins[i], outs[i].at[me], local_sems.at[i])
            cp.start()
            own.append(cp)
        sent = []
        for k in range(1, N_DEV):
            for i in range(n):
                cp = remote(i, k, ins[i].at[peer(k)[1]] if scatter else ins[i], outs[i].at[me])
                cp.start()
                sent.append(cp)
        for k in range(1, N_DEV):
            for i in range(n):
                landing = outs[i].at[peer(k)[1]]
                remote(i, k, landing, landing).wait_recv()
        for cp in sent:
            cp.wait_send()
        for cp in own:
            cp.wait()

    any_spec = pl.BlockSpec(memory_space=pl.ANY)
    out_shape = [jax.ShapeDtypeStruct(a.shape if scatter else (N_DEV,) + a.shape, a.dtype) for a in arrs]
    return pl.pallas_call(
        body, name=name, out_shape=out_shape, in_specs=[any_spec] * n, out_specs=[any_spec] * n,
        scratch_shapes=[pltpu.SemaphoreType.DMA((n, N_DEV - 1)), pltpu.SemaphoreType.DMA((n, N_DEV - 1)),
                        pltpu.SemaphoreType.DMA((n,))],
    )(*arrs)


def _comm_call(body, arrs, out_shape, sem_shapes, name):
    any_spec = pl.BlockSpec(memory_space=pl.ANY)
    return pl.pallas_call(
        body, name=name, out_shape=out_shape, in_specs=[any_spec] * len(arrs), out_specs=[any_spec] * len(out_shape),
        scratch_shapes=[pltpu.SemaphoreType.DMA(s) for s in sem_shapes],
    )(*arrs)


def _gather_two_level(arrs, *, name):
    n = len(arrs)

    def body(*refs):
        ins, outs = refs[:n], refs[n:2 * n]
        send_sems, recv_sems, local_sems = refs[2 * n:]
        x, y, c = (lax.axis_index(a) for a in AXES)
        index = lambda px, py, pc: 4 * px + 2 * py + pc
        me, sibling = index(x, y, c), (x, y, 1 - c)
        chips = [(1 - x, y), (x, 1 - y), (1 - x, 1 - y)]

        def copy(i, k, block, to, src=None):
            return pltpu.make_async_remote_copy(
                src_ref=outs[i].at[block] if src is None else src, dst_ref=outs[i].at[block],
                send_sem=send_sems.at[i, k], recv_sem=recv_sems.at[i, k], device_id=to,
                device_id_type=pl.DeviceIdType.MESH)

        own = [pltpu.make_async_copy(ins[i], outs[i].at[me], local_sems.at[i]) for i in range(n)]
        first = [copy(i, 1 + j, me, (*chip, c), src=ins[i]) for j, chip in enumerate(chips) for i in range(n)]
        first += [copy(i, 0, me, sibling, src=ins[i]) for i in range(n)]
        for cp in own + first:
            cp.start()
        passed = []
        for j, chip in enumerate(chips):
            block = index(*chip, c)
            for i in range(n):
                copy(i, 1 + j, block, (x, y, c)).wait_recv()
            for i in range(n):
                cp = copy(i, 4 + j, block, sibling)
                cp.start()
                passed.append(cp)
        for i in range(n):
            copy(i, 0, index(x, y, 1 - c), (x, y, c)).wait_recv()
        for j, chip in enumerate(chips):
            for i in range(n):
                copy(i, 4 + j, index(*chip, 1 - c), (x, y, c)).wait_recv()
        for cp in first + passed:
            cp.wait_send()
        for cp in own:
            cp.wait()

    out_shape = [jax.ShapeDtypeStruct((N_DEV,) + a.shape, a.dtype) for a in arrs]
    return _comm_call(body, arrs, out_shape, [(n, 7), (n, 7), (n,)], name)


def _swap_cores(arrs, *, name):
    n = len(arrs)

    def body(*refs):
        ins, got = refs[:n], refs[n:2 * n]
        send_sems, recv_sems = refs[2 * n:]
        x, y, c = (lax.axis_index(a) for a in AXES)
        remote = [pltpu.make_async_remote_copy(
            src_ref=ins[i].at[2 * q + 1 - c], dst_ref=got[i].at[q], send_sem=send_sems.at[i, q],
            recv_sem=recv_sems.at[i, q], device_id=(x, y, 1 - c), device_id_type=pl.DeviceIdType.MESH)
            for q in range(4) for i in range(n)]
        for cp in remote:
            cp.start()
        for cp in remote:
            cp.wait_recv()
        for cp in remote:
            cp.wait_send()

    half = [jax.ShapeDtypeStruct((4,) + a.shape[1:], a.dtype) for a in arrs]
    return _comm_call(body, arrs, half, [(n, 4), (n, 4)], name)


def _swap_chips(arrs, *, name):
    n = len(arrs)

    def body(*refs):
        ins, outs = refs[:n], refs[n:2 * n]
        send_sems, recv_sems, local_sems = refs[2 * n:]
        x, y, c = (lax.axis_index(a) for a in AXES)
        mine = 2 * x + y

        def peer(k):
            px = 1 - x if k & 2 else x
            py = 1 - y if k & 1 else y
            return (px, py, c), 2 * px + py

        def remote(i, k, src, dst):
            return pltpu.make_async_remote_copy(
                src_ref=src, dst_ref=dst, send_sem=send_sems.at[i, k - 1], recv_sem=recv_sems.at[i, k - 1],
                device_id=peer(k)[0], device_id_type=pl.DeviceIdType.MESH)

        own = [pltpu.make_async_copy(ins[i].at[mine], outs[i].at[mine], local_sems.at[i]) for i in range(n)]
        sent = [remote(i, k, ins[i].at[peer(k)[1]], outs[i].at[mine]) for k in range(1, 4) for i in range(n)]
        for cp in own + sent:
            cp.start()
        for k in range(1, 4):
            for i in range(n):
                landing = outs[i].at[peer(k)[1]]
                remote(i, k, landing, landing).wait_recv()
        for cp in sent:
            cp.wait_send()
        for cp in own:
            cp.wait()

    out_shape = [jax.ShapeDtypeStruct(a.shape, a.dtype) for a in arrs]
    return _comm_call(body, arrs, out_shape, [(n, 3), (n, 3), (n,)], name)


def _add_core_slabs(full, got, core, *, name):
    C = got.shape[-1]
    R = got[0].size // C
    tr = _tile(R, max(16, (1 << 19) // C), 16)

    def body(core_ref, a_ref, b_ref, o_ref):
        o_ref[...] = (a_ref[...].astype(F32) + b_ref[...].astype(F32)).astype(o_ref.dtype)

    blk = pl.BlockSpec((None, tr, C), lambda q, i, core_ref: (q, i, 0))
    grid_spec = pltpu.PrefetchScalarGridSpec(
        num_scalar_prefetch=1, grid=(4, R // tr),
        in_specs=[pl.BlockSpec((None, None, tr, C), lambda q, i, core_ref: (q, core_ref[0], i, 0)), blk], out_specs=blk)
    return pl.pallas_call(
        body, name=name, grid_spec=grid_spec, out_shape=jax.ShapeDtypeStruct((4, R, C), got.dtype),
        compiler_params=_params("parallel", "parallel"),
    )(core, full.reshape(4, 2, R, C), got.reshape(4, R, C)).reshape(got.shape)


def _mm(a, b, *, name, nt=False, bias=None, add=None, out_dtype=F32, shard_cols=0):
    M, K = a.shape
    N = b.shape[0] if nt else b.shape[1]
    tm, tn, tk = _tile(M, 1024, 16), shard_cols or _tile(N, 1024, LANE), _tile(K, 1024, LANE)
    nk = K // tk
    dims = (((1,), (1,)), ((), ())) if nt else (((1,), (0,)), ((), ()))

    def body(*refs):
        refs = list(refs)
        a_ref, b_ref = refs.pop(0), refs.pop(0)
        bias_ref = refs.pop(0) if bias is not None else None
        add_ref = refs.pop(0) if add is not None else None
        o_ref = refs.pop(0)
        part = lax.dot_general(a_ref[...], b_ref[...], dims, preferred_element_type=F32)

        def finish(val):
            if bias_ref is not None:
                val = val + bias_ref[...]
            if add_ref is not None:
                val = val + add_ref[...]
            o_ref[...] = val.astype(o_ref.dtype)

        if nk == 1:
            finish(part)
        else:
            acc_ref = refs.pop(0)
            k = pl.program_id(2)

            @pl.when(k == 0)
            def _():
                acc_ref[...] = part

            @pl.when(k > 0)
            def _():
                acc_ref[...] += part

            @pl.when(k == nk - 1)
            def _():
                finish(acc_ref[...])

    in_specs = [pl.BlockSpec((tm, tk), lambda i, j, k: (i, k)),
                pl.BlockSpec((tn, tk), lambda i, j, k: (j, k)) if nt else pl.BlockSpec((tk, tn), lambda i, j, k: (k, j))]
    operands = [a, b]
    if bias is not None:
        in_specs.append(pl.BlockSpec((1, tn), lambda i, j, k: (0, j)))
        operands.append(bias)
    if add is not None:
        in_specs.append(pl.BlockSpec((tm, tn), lambda i, j, k: (i, j)))
        operands.append(add)
    if shard_cols:
        out_shape = jax.ShapeDtypeStruct((N // tn, M, tn), out_dtype)
        out_spec = pl.BlockSpec((None, tm, tn), lambda i, j, k: (j, i, 0))
    else:
        out_shape = jax.ShapeDtypeStruct((M, N), out_dtype)
        out_spec = pl.BlockSpec((tm, tn), lambda i, j, k: (i, j))
    return pl.pallas_call(
        body, name=name, grid=(M // tm, N // tn, nk), in_specs=in_specs, out_specs=out_spec, out_shape=out_shape,
        scratch_shapes=[pltpu.VMEM((tm, tn), F32)] if nk > 1 else [],
        compiler_params=_params("parallel", "parallel", "arbitrary"),
    )(*operands)


def _transpose(x, *, name):
    T, C = x.shape
    tt, tc = _tile(T, 512, LANE), _tile(C, 512, LANE)

    def body(x_ref, o_ref):
        o_ref[...] = x_ref[...].astype(F32).T.astype(BF16)

    return pl.pallas_call(
        body, name=name, grid=(T // tt, C // tc), in_specs=[pl.BlockSpec((tt, tc), lambda i, j: (i, j))],
        out_specs=pl.BlockSpec((tc, tt), lambda i, j: (j, i)), out_shape=jax.ShapeDtypeStruct((C, T), BF16),
        compiler_params=_params("parallel", "parallel"),
    )(x)


def _group_mm(a, w, *, name, nt=False, bias=None, scale=None):
    T, D = a.shape
    G, gd, _ = w.shape
    tm = _tile(T, 1024, 16)
    dims = (((1,), (1,)), ((), ())) if nt else (((1,), (0,)), ((), ()))
    fused = bias is not None

    def body(*refs):
        if fused:
            a_ref, w_ref, b_ref, s_ref, mb_ref, y_ref = refs
        else:
            a_ref, w_ref, y_ref = refs
        val = lax.dot_general(a_ref[...], w_ref[...], dims, preferred_element_type=F32)
        if fused:
            val = val + b_ref[...]
            mb_ref[...] = val
            val = val * s_ref[...]
        y_ref[...] = val

    blk = pl.BlockSpec((tm, gd), lambda i, g: (i, g))
    vec = pl.BlockSpec((1, gd), lambda i, g: (0, g))
    in_specs = [blk, pl.BlockSpec((None, gd, gd), lambda i, g: (g, 0, 0))] + ([vec, vec] if fused else [])
    td = jax.ShapeDtypeStruct((T, D), F32)
    return pl.pallas_call(
        body, name=name, grid=(T // tm, G), in_specs=in_specs, out_specs=[blk, blk] if fused else blk,
        out_shape=[td, td] if fused else td, compiler_params=_params("parallel", "parallel"),
    )(*([a, w, bias, scale] if fused else [a, w]))


def _group_dw(at, b, groups, *, name):
    D, T = at.shape
    gd = D // groups
    tk = _tile(T, 2048, LANE)

    def body(a_ref, b_ref, o_ref):
        @pl.when(pl.program_id(1) == 0)
        def _():
            o_ref[...] = jnp.zeros_like(o_ref)

        o_ref[...] += jnp.dot(a_ref[...], b_ref[...], preferred_element_type=F32)

    return pl.pallas_call(
        body, name=name, grid=(groups, T // tk),
        in_specs=[pl.BlockSpec((gd, tk), lambda g, k: (g, k)), pl.BlockSpec((tk, gd), lambda g, k: (k, g))],
        out_specs=pl.BlockSpec((None, gd, gd), lambda g, k: (g, 0, 0)),
        out_shape=jax.ShapeDtypeStruct((groups, gd, gd), F32), compiler_params=_params("parallel", "arbitrary"),
    )(at, b)


def _mod_fwd(c_all, w, b, *, name):
    L, D, n = w.shape
    B = c_all.shape[0]

    def body(c_ref, w_ref, b_ref, o_ref):
        cv = c_ref[...]
        o_ref[...] = jnp.dot(cv * _sigmoid(cv), w_ref[...], preferred_element_type=F32) + b_ref[...]

    return pl.pallas_call(
        body, name=name, grid=(L,),
        in_specs=[pl.BlockSpec((B, D), lambda l: (0, 0)), pl.BlockSpec((None, D, n), lambda l: (l, 0, 0)),
                  pl.BlockSpec((None, 1, n), lambda l: (l, 0, 0))],
        out_specs=pl.BlockSpec((None, B, n), lambda l: (l, 0, 0)), out_shape=jax.ShapeDtypeStruct((L, B, n), F32),
        compiler_params=_params("parallel"),
    )(c_all, w, b.reshape(L, 1, n))


def _mod_bwd(c_all_t, dmod, *, name):
    L, B, n = dmod.shape
    D = c_all_t.shape[0]

    def body(c_ref, d_ref, o_ref):
        cv = c_ref[...]
        o_ref[...] = jnp.dot(cv * _sigmoid(cv), d_ref[...], preferred_element_type=F32)

    return pl.pallas_call(
        body, name=name, grid=(L,),
        in_specs=[pl.BlockSpec((D, B), lambda l: (0, 0)), pl.BlockSpec((None, B, n), lambda l: (l, 0, 0))],
        out_specs=pl.BlockSpec((None, D, n), lambda l: (l, 0, 0)), out_shape=jax.ShapeDtypeStruct((L, D, n), F32),
        compiler_params=_params("parallel"),
    )(c_all_t, dmod)


def _rowwise(fn, tiles, vecs, out_dtypes, n_sums, *, name):
    T, C = tiles[0].shape
    tr = _tile(T, 256, 16)
    ch = _tile(tr, 32, 16)
    nt_, nv, no = len(tiles), len(vecs), len(out_dtypes)

    def body(*refs):
        t_refs, v_refs = refs[:nt_], refs[nt_:nt_ + nv]
        o_refs, s_refs = refs[nt_ + nv:nt_ + nv + no], refs[nt_ + nv + no:]
        vv = [v[...] for v in v_refs]
        acc = [None] * n_sums
        for r0 in range(0, tr, ch):
            rows = pl.ds(r0, ch)
            o_vals, s_vals = fn([t[rows, :] for t in t_refs], vv)
            for o, val in zip(o_refs, o_vals):
                o[rows, :] = val.astype(o.dtype)
            for i, val in enumerate(s_vals):
                part = jnp.sum(val, axis=0, keepdims=True)
                acc[i] = part if acc[i] is None else acc[i] + part
        if n_sums:
            first = pl.program_id(0) == 0

            @pl.when(first)
            def _():
                for s, val in zip(s_refs, acc):
                    s[...] = val

            @pl.when(jnp.logical_not(first))
            def _():
                for s, val in zip(s_refs, acc):
                    s[...] += val

    tile_spec = pl.BlockSpec((tr, C), lambda i: (i, 0))
    vec_spec = pl.BlockSpec((1, C), lambda i: (0, 0))
    res = pl.pallas_call(
        body, name=name, grid=(T // tr,), in_specs=[tile_spec] * nt_ + [vec_spec] * nv,
        out_specs=[tile_spec] * no + [vec_spec] * n_sums,
        out_shape=[jax.ShapeDtypeStruct((T, C), d) for d in out_dtypes] + [jax.ShapeDtypeStruct((1, C), F32)] * n_sums,
        compiler_params=_params("arbitrary"),
    )(*tiles, *vecs)
    return res


def _rms(v):
    return lax.rsqrt(jnp.mean(v * v, axis=-1, keepdims=True) + RMS_EPS)


def _prenorm_fwd(x, g, sc, sh, out_dtype, *, name):
    def fn(t, v):
        (xv,), (g_, sc_, sh_) = t, v
        return [(xv * _rms(xv)) * g_ * (1.0 + sc_) + sh_], []

    return _rowwise(fn, [x], [g, sc, sh], [out_dtype], 0, name=name)[0]


def _post_pre_fwd(x, y, gp, gate, g, sc, sh, out_dtype, *, name):
    def fn(t, v):
        (xv, yv), (gp_, gate_, g_, sc_, sh_) = t, v
        x1 = xv + gate_ * ((yv * _rms(yv)) * gp_)
        return [x1, (x1 * _rms(x1)) * g_ * (1.0 + sc_) + sh_], []

    return _rowwise(fn, [x, y], [gp, gate, g, sc, sh], [F32, out_dtype], 0, name=name)


def _post_loss(x, y, gp, gate, target, *, name):
    D = x.shape[1]

    def fn(t, v):
        (xv, yv, tv), (gp_, gate_) = t, v
        err = xv + gate_ * ((yv * _rms(yv)) * gp_) - tv
        return [err / D], [err * err]

    return _rowwise(fn, [x, y, target], [gp, gate], [F32], 1, name=name)


def _post_bwd(y, dxo, gp, gate, out_dtype, *, name):
    def fn(t, v):
        (yv, dv), (gp_, gate_) = t, v
        r = _rms(yv)
        yn = yv * r
        dyn = dv * gate_ * gp_
        dy = r * (dyn - yn * jnp.mean(dyn * yn, axis=-1, keepdims=True))
        return [dy], [dv * gate_ * yn, dv * yn * gp_, dy]

    return _rowwise(fn, [y, dxo], [gp, gate], [out_dtype], 3, name=name)


def _pre_bwd(x, dh, dres, g, sc, *, name):
    def fn(t, v):
        (xv, dhv, dr), (g_, sc_) = t, v
        dhv = dhv.astype(F32)
        r = _rms(xv)
        xn = xv * r
        dxn = dhv * g_ * (1.0 + sc_)
        dx = dr + r * (dxn - xn * jnp.mean(dxn * xn, axis=-1, keepdims=True))
        return [dx], [dhv, dhv * xn * g_, dhv * xn * (1.0 + sc_)]

    return _rowwise(fn, [x, dh, dres], [g, sc], [F32], 3, name=name)


def _a_ln_bwd(z, dq, g, b, *, name):
    def fn(t, v):
        (zv, dqv), (g_, b_) = t, v
        mu = jnp.mean(zv, axis=-1, keepdims=True)
        zc = zv - mu
        rstd = lax.rsqrt(jnp.mean(zc * zc, axis=-1, keepdims=True) + LN_EPS)
        zn = zc * rstd
        l = zn * g_ + b_
        s = _sigmoid(l)
        dl = dqv * (s * (1.0 + l * (1.0 - s)))
        dzn = dl * g_
        dz = rstd * (dzn - jnp.mean(dzn, axis=-1, keepdims=True) - zn * jnp.mean(dzn * zn, axis=-1, keepdims=True))
        return [dz], [dl * zn, dl, dz]

    return _rowwise(fn, [z, dq], [g, b], [F32], 3, name=name)


def _a_glu_bwd(ua, ug, dp, *, name):
    def fn(t, v):
        uav, ugv, dpv = t
        s = _sigmoid(ugv)
        dua = dpv * s
        dug = dpv * uav * s * (1.0 - s)
        return [dua, dug], [dua, dug]

    return _rowwise(fn, [ua, ug, dp], [], [BF16, BF16], 2, name=name)


def _b_scale_bwd(dy, mb, scale, *, name):
    def fn(t, v):
        (dyv, mbv), (s_,) = t, v
        dm = dyv * s_
        return [dm], [dyv * mbv, dm]

    return _rowwise(fn, [dy, mb], [scale], [BF16], 2, name=name)


def _c_gate_bwd1(gb, cv, ds, *, name):
    def fn(t, v):
        gbv, cvv, dsv = t
        return [dsv * cvv, dsv * gbv], []

    return _rowwise(fn, [gb, cv, ds], [], [BF16, F32], 0, name=name)


def _c_gate_bwd2(gc, v_, dp, *, name):
    def fn(t, v):
        gcv, vv, dpv = t
        return [dpv * vv, dpv * gcv], []

    return _rowwise(fn, [gc, v_, dp], [], [BF16, BF16], 0, name=name)


def _stage_causal(buf, t, halo, tt):
    @pl.when(t == 0)
    def _():
        buf[pl.ds(0, halo), :] = jnp.zeros((halo, buf.shape[1]), F32)

    @pl.when(t > 0)
    def _():
        buf[pl.ds(0, halo), :] = buf[pl.ds(tt, halo), :]


def _stage_anticausal(buf, t, halo, tt):
    @pl.when(t == 0)
    def _():
        buf[pl.ds(tt, halo), :] = jnp.zeros((halo, buf.shape[1]), F32)

    @pl.when(t > 0)
    def _():
        buf[pl.ds(tt, halo), :] = buf[pl.ds(0, halo), :]


def _causal_taps(buf, w_ref, taps, halo, r0, rows, cols):
    acc = None
    for k in range(taps):
        term = w_ref[pl.ds(k, 1), cols] * buf[pl.ds(halo - (taps - 1) + k + r0, rows), cols]
        acc = term if acc is None else acc + term
    return acc


def _halo_tiles(T, C, halo, col_cap):
    tt = _tile(T, 512 if halo <= 8 else 256, 16)
    assert tt >= halo, (tt, halo)
    return tt, _tile(C, col_cap, LANE), _tile(tt, 128, 16)


def _col_blocks(tc):
    cb = LANE if tc % LANE == 0 else tc
    return [pl.ds(c0, cb) for c0 in range(0, tc, cb)]


def _ffn_act_fwd(ug, uv, wg, wv, bg, bv, *, name):
    T, F = ug.shape
    taps, halo = wg.shape[0], 8
    tt, tc, rc = _halo_tiles(T, F, halo, 1024)

    def body(ug_ref, uv_ref, wg_ref, wv_ref, bg_ref, bv_ref, a_ref, bufg, bufv):
        t = pl.program_id(1)
        for u_ref, buf in ((ug_ref, bufg), (uv_ref, bufv)):
            _stage_causal(buf, t, halo, tt)
            buf[pl.ds(halo, tt), :] = u_ref[...]
        for cols in _col_blocks(tc):
            for r0 in range(0, tt, rc):
                yg = _causal_taps(bufg, wg_ref, taps, halo, r0, rc, cols) + bg_ref[:, cols]
                yv = _causal_taps(bufv, wv_ref, taps, halo, r0, rc, cols) + bv_ref[:, cols]
                a_ref[pl.ds(r0, rc), cols] = (yg * _sigmoid(yg) * yv).astype(BF16)

    blk = pl.BlockSpec((tt, tc), lambda j, t: (t, j))
    wspec = pl.BlockSpec((taps, tc), lambda j, t: (0, j))
    vspec = pl.BlockSpec((1, tc), lambda j, t: (0, j))
    return pl.pallas_call(
        body, name=name, grid=(F // tc, T // tt), in_specs=[blk, blk, wspec, wspec, vspec, vspec], out_specs=blk,
        out_shape=jax.ShapeDtypeStruct((T, F), BF16),
        scratch_shapes=[pltpu.VMEM((halo + tt, tc), F32), pltpu.VMEM((halo + tt, tc), F32)],
        compiler_params=_params("parallel", "arbitrary"),
    )(ug, uv, wg, wv, bg, bv)


def _ffn_act_bwd(ug, uv, wg, wv, bg, bv, da, *, name):
    T, F = ug.shape
    taps, halo = wg.shape[0], 8
    tt, tc, rc = _halo_tiles(T, F, halo, 1024)

    def body(ug_ref, uv_ref, wg_ref, wv_ref, bg_ref, bv_ref, da_ref, dg_ref, dv_ref, sg_ref, sv_ref, bufg, bufv):
        t = pl.program_id(1)

        @pl.when(t == 0)
        def _():
            sg_ref[...] = jnp.zeros_like(sg_ref)
            sv_ref[...] = jnp.zeros_like(sv_ref)

        for u_ref, buf in ((ug_ref, bufg), (uv_ref, bufv)):
            _stage_causal(buf, t, halo, tt)
            buf[pl.ds(halo, tt), :] = u_ref[...]
        for cols in _col_blocks(tc):
            for r0 in range(0, tt, rc):
                rows = pl.ds(r0, rc)
                yg = _causal_taps(bufg, wg_ref, taps, halo, r0, rc, cols) + bg_ref[:, cols]
                yv = _causal_taps(bufv, wv_ref, taps, halo, r0, rc, cols) + bv_ref[:, cols]
                s = _sigmoid(yg)
                d = da_ref[rows, cols]
                dyg = d * yv * (s * (1.0 + yg * (1.0 - s)))
                dyv = d * (yg * s)
                dg_ref[rows, cols] = dyg
                dv_ref[rows, cols] = dyv
                sg_ref[:, cols] += jnp.sum(dyg, axis=0, keepdims=True)
                sv_ref[:, cols] += jnp.sum(dyv, axis=0, keepdims=True)

    blk = pl.BlockSpec((tt, tc), lambda j, t: (t, j))
    wspec = pl.BlockSpec((taps, tc), lambda j, t: (0, j))
    vspec = pl.BlockSpec((1, tc), lambda j, t: (0, j))
    td, vd = jax.ShapeDtypeStruct((T, F), F32), jax.ShapeDtypeStruct((1, F), F32)
    return pl.pallas_call(
        body, name=name, grid=(F // tc, T // tt), in_specs=[blk, blk, wspec, wspec, vspec, vspec, blk],
        out_specs=[blk, blk, vspec, vspec], out_shape=[td, td, vd, vd],
        scratch_shapes=[pltpu.VMEM((halo + tt, tc), F32), pltpu.VMEM((halo + tt, tc), F32)],
        compiler_params=_params("parallel", "arbitrary"),
    )(ug, uv, wg, wv, bg, bv, da)


def _dwconv_bwd(x, dy, w, out_dtype, *, name):
    T, C = x.shape
    taps = w.shape[0]
    halo = _round_up(taps - 1, 8)
    tt, tc, rc = _halo_tiles(T, C, halo, 512)
    nt_ = T // tt

    def body(x_ref, dy_ref, w_ref, dx_ref, dw_ref, buf):
        t = pl.program_id(1)

        @pl.when(t == 0)
        def _():
            dw_ref[...] = jnp.zeros_like(dw_ref)

        _stage_anticausal(buf, t, halo, tt)
        buf[pl.ds(0, tt), :] = dy_ref[...]
        for cols in _col_blocks(tc):
            for r0 in range(0, tt, rc):
                rows = pl.ds(r0, rc)
                xv = x_ref[rows, cols]
                acc = None
                for j in range(taps):
                    k = taps - 1 - j
                    ahead = buf[pl.ds(r0 + j, rc), cols]
                    term = w_ref[pl.ds(k, 1), cols] * ahead
                    acc = term if acc is None else acc + term
                    dw_ref[pl.ds(k, 1), cols] += jnp.sum(xv * ahead, axis=0, keepdims=True)
                dx_ref[rows, cols] = acc.astype(dx_ref.dtype)

    blk = pl.BlockSpec((tt, tc), lambda j, t: (nt_ - 1 - t, j))
    wspec = pl.BlockSpec((taps, tc), lambda j, t: (0, j))
    return pl.pallas_call(
        body, name=name, grid=(C // tc, nt_), in_specs=[blk, blk, wspec], out_specs=[blk, wspec],
        out_shape=[jax.ShapeDtypeStruct((T, C), out_dtype), jax.ShapeDtypeStruct((taps, C), F32)],
        scratch_shapes=[pltpu.VMEM((tt + halo, tc), F32)], compiler_params=_params("parallel", "arbitrary"),
    )(x, dy, w)


def _a_conv_fwd(ua, ug, w, b, ln_g, ln_b, *, name):
    T, D = ua.shape
    taps = w.shape[0]
    halo = _round_up(taps - 1, 8)
    tt = _tile(T, 128, 16)
    assert tt >= halo
    rc, ch = _tile(tt, 128, 16), _tile(tt, 32, 16)

    def body(ua_ref, ug_ref, w_ref, b_ref, g_ref, lb_ref, p_ref, z_ref, q_ref, buf):
        t = pl.program_id(0)
        _stage_causal(buf, t, halo, tt)
        for r0 in range(0, tt, ch):
            rows = pl.ds(r0, ch)
            pv = ua_ref[rows, :] * _sigmoid(ug_ref[rows, :])
            p_ref[rows, :] = pv
            buf[pl.ds(halo + r0, ch), :] = pv
        for cols in _col_blocks(D):
            for r0 in range(0, tt, rc):
                z_ref[pl.ds(r0, rc), cols] = _causal_taps(buf, w_ref, taps, halo, r0, rc, cols) + b_ref[:, cols]
        for r0 in range(0, tt, ch):
            rows = pl.ds(r0, ch)
            zv = z_ref[rows, :]
            zc = zv - jnp.mean(zv, axis=-1, keepdims=True)
            l = zc * lax.rsqrt(jnp.mean(zc * zc, axis=-1, keepdims=True) + LN_EPS) * g_ref[...] + lb_ref[...]
            q_ref[rows, :] = (l * _sigmoid(l)).astype(BF16)

    blk = pl.BlockSpec((tt, D), lambda t: (t, 0))
    vec = pl.BlockSpec((1, D), lambda t: (0, 0))
    td = jax.ShapeDtypeStruct((T, D), F32)
    return pl.pallas_call(
        body, name=name, grid=(T // tt,),
        in_specs=[blk, blk, pl.BlockSpec((taps, D), lambda t: (0, 0)), vec, vec, vec], out_specs=[blk, blk, blk],
        out_shape=[td, td, jax.ShapeDtypeStruct((T, D), BF16)],
        scratch_shapes=[pltpu.VMEM((halo + tt, D), F32)], compiler_params=_params("arbitrary"),
    )(ua, ug, w, b, ln_g, ln_b)


def _c_gate_fwd(gb, gc, v, w, *, name):
    T, D = gb.shape
    taps, halo = w.shape[0], 8
    tt, tc, rc = _halo_tiles(T, D, halo, 1024)

    def body(gb_ref, gc_ref, v_ref, w_ref, p_ref, cv_ref, s_ref, buf):
        t = pl.program_id(1)
        _stage_causal(buf, t, halo, tt)
        pv = gc_ref[...] * v_ref[...]
        p_ref[...] = pv
        buf[pl.ds(halo, tt), :] = pv
        for cols in _col_blocks(tc):
            for r0 in range(0, tt, rc):
                rows = pl.ds(r0, rc)
                cv = _causal_taps(buf, w_ref, taps, halo, r0, rc, cols)
                cv_ref[rows, cols] = cv
                s_ref[rows, cols] = (gb_ref[rows, cols] * cv).astype(BF16)

    blk = pl.BlockSpec((tt, tc), lambda j, t: (t, j))
    td = jax.ShapeDtypeStruct((T, D), F32)
    return pl.pallas_call(
        body, name=name, grid=(D // tc, T // tt),
        in_specs=[blk, blk, blk, pl.BlockSpec((taps, tc), lambda j, t: (0, j))], out_specs=[blk, blk, blk],
        out_shape=[td, td, jax.ShapeDtypeStruct((T, D), BF16)],
        scratch_shapes=[pltpu.VMEM((halo + tt, tc), F32)], compiler_params=_params("parallel", "arbitrary"),
    )(gb, gc, v, w)


def _pool_count(base, r0, rows, width, window):
    pos = lax.broadcasted_iota(jnp.int32, (rows, width), 0) + (base + r0 + 1)
    return jnp.minimum(pos, window).astype(F32)


def _pool_fwd(h, *, name):
    T, D = h.shape
    halo = _round_up(max(POOL_WINDOWS), 8)
    gd = D // len(POOL_WINDOWS)
    tt = _tile(T, 256, 16)
    assert tt >= halo
    rc = _tile(tt, 128, 16)

    def body(h_ref, o_ref, buf):
        t = pl.program_id(0)
        _stage_causal(buf, t, halo, tt)
        buf[pl.ds(halo, tt), :] = h_ref[...]
        for g, window in enumerate(POOL_WINDOWS):
            for cols_in_group in _col_blocks(gd):
                cols = pl.ds(g * gd + cols_in_group.start, cols_in_group.size)
                for r0 in range(0, tt, rc):
                    acc = None
                    for j in range(window):
                        term = buf[pl.ds(halo - j + r0, rc), cols]
                        acc = term if acc is None else acc + term
                    cnt = _pool_count(t * tt, r0, rc, cols.size, window)
                    o_ref[pl.ds(r0, rc), cols] = (acc / cnt - buf[pl.ds(halo + r0, rc), cols]).astype(BF16)

    blk = pl.BlockSpec((tt, D), lambda t: (t, 0))
    return pl.pallas_call(
        body, name=name, grid=(T // tt,), in_specs=[blk], out_specs=blk, out_shape=jax.ShapeDtypeStruct((T, D), BF16),
        scratch_shapes=[pltpu.VMEM((halo + tt, D), F32)], compiler_params=_params("arbitrary"),
    )(h)


def _pool_bwd(dp, *, name):
    T, D = dp.shape
    halo = _round_up(max(POOL_WINDOWS), 8)
    gd = D // len(POOL_WINDOWS)
    tt = _tile(T, 256, 16)
    assert tt >= halo
    rc = _tile(tt, 128, 16)
    nt_ = T // tt

    def body(dp_ref, o_ref, buf):
        t = pl.program_id(0)
        base = (nt_ - 1 - t) * tt
        _stage_anticausal(buf, t, halo, tt)
        for g, window in enumerate(POOL_WINDOWS):
            for cols_in_group in _col_blocks(gd):
                cols = pl.ds(g * gd + cols_in_group.start, cols_in_group.size)
                for r0 in range(0, tt, rc):
                    rows = pl.ds(r0, rc)
                    buf[rows, cols] = dp_ref[rows, cols] / _pool_count(base, r0, rc, cols.size, window)
                for r0 in range(0, tt, rc):
                    acc = None
                    for j in range(window):
                        term = buf[pl.ds(r0 + j, rc), cols]
                        acc = term if acc is None else acc + term
                    o_ref[pl.ds(r0, rc), cols] = acc - dp_ref[pl.ds(r0, rc), cols]

    blk = pl.BlockSpec((tt, D), lambda t: (nt_ - 1 - t, 0))
    return pl.pallas_call(
        body, name=name, grid=(nt_,), in_specs=[blk], out_specs=blk, out_shape=jax.ShapeDtypeStruct((T, D), F32),
        scratch_shapes=[pltpu.VMEM((tt + halo, D), F32)], compiler_params=_params("arbitrary"),
    )(dp)


def _sum_slabs(slabs, *, name):
    S, R, C = slabs.shape

    def body(s_ref, o_ref):
        acc = s_ref[0]
        for p in range(1, S):
            acc = acc + s_ref[p]
        o_ref[...] = acc

    return pl.pallas_call(body, name=name, out_shape=jax.ShapeDtypeStruct((R, C), F32),
                          compiler_params=_params())(slabs)


def _adamw(w, slabs, m, v, *, name):
    S, R, C = slabs.shape
    sub = 16 if slabs.dtype == BF16 else 8
    tr = _tile(R, max(sub, (1 << 18) // (C * S)), sub)
    c1, c2 = 1.0 - ADAM_B1 ** ADAM_STEP, 1.0 - ADAM_B2 ** ADAM_STEP

    def body(w_ref, s_ref, m_ref, v_ref, g_ref, d_ref, nm_ref, nv_ref):
        g = s_ref[0].astype(F32)
        for p in range(1, S):
            g = g + s_ref[p].astype(F32)
        nm = ADAM_B1 * m_ref[...] + (1.0 - ADAM_B1) * g
        nv = ADAM_B2 * v_ref[...] + (1.0 - ADAM_B2) * (g * g)
        g_ref[...] = g
        nm_ref[...] = nm
        nv_ref[...] = nv
        d_ref[...] = -ADAM_LR * ((nm / c1) / (jnp.sqrt(nv / c2) + ADAM_EPS) + ADAM_WD * w_ref[...])

    blk = pl.BlockSpec((tr, C), lambda i: (i, 0))
    sd = jax.ShapeDtypeStruct((R, C), F32)
    return pl.pallas_call(
        body, name=name, grid=(R // tr,), in_specs=[blk, pl.BlockSpec((S, tr, C), lambda i: (0, i, 0)), blk, blk],
        out_specs=[blk] * 4, out_shape=[sd] * 4, compiler_params=_params("parallel"),
    )(w, slabs, m, v)


def _cols_natural(g):
    g = jnp.moveaxis(g, 0, -2)
    return g.reshape(g.shape[:-2] + (g.shape[-2] * g.shape[-1],))


def _rows_natural(g):
    g = jnp.moveaxis(g, 0, 1)
    return g.reshape((g.shape[0], g.shape[1] * g.shape[2]) + g.shape[3:])


def _col_slabs(full, n_shards=N_DEV):
    n = full.shape[-1] // n_shards
    return jnp.moveaxis(full.reshape(full.shape[:-1] + (n_shards, n)), -2, 0)


def _pad_last(a, n):
    return jnp.pad(a, [(0, 0)] * (a.ndim - 1) + [(0, n - a.shape[-1])])


def kernel(x, c, mod_w, mod_b, norm_pre_mix, norm_post_mix, norm_pre_ffn, norm_post_ffn, a_pw1_w, a_pw1_b, a_dw_w, a_dw_b, a_ln_g, a_ln_b, a_pw2_w, a_pw2_b, b_group_w, b_group_b, b_scale, c_in_w, c_conv_w, c_out_w, f_up_w, f_dw_w, f_dw_b, f_down_w, loss_target, m_mod_w, m_mod_b, m_norm_pre_mix, m_norm_post_mix, m_norm_pre_ffn, m_norm_post_ffn, m_a_pw1_w, m_a_pw1_b, m_a_dw_w, m_a_dw_b, m_a_ln_g, m_a_ln_b, m_a_pw2_w, m_a_pw2_b, m_b_group_w, m_b_group_b, m_b_scale, m_c_in_w, m_c_conv_w, m_c_out_w, m_f_up_w, m_f_dw_w, m_f_dw_b, m_f_down_w, v_mod_w, v_mod_b, v_norm_pre_mix, v_norm_post_mix, v_norm_pre_ffn, v_norm_post_ffn, v_a_pw1_w, v_a_pw1_b, v_a_dw_w, v_a_dw_b, v_a_ln_g, v_a_ln_b, v_a_pw2_w, v_a_pw2_b, v_b_group_w, v_b_group_b, v_b_scale, v_c_in_w, v_c_conv_w, v_c_out_w, v_f_up_w, v_f_dw_w, v_f_dw_b, v_f_down_w):
    p = dict(locals())
    assert list(p) == INPUTS
    x0 = p['x'][0]
    target = p['loss_target'][0]
    T, D = x0.shape
    L = p['mod_w'].shape[0]
    G = len(POOL_WINDOWS)
    gd = D // G
    ns = p['f_up_w'].shape[-1]
    npad = _round_up(ns, LANE)
    Fp = 4 * npad
    me = 4 * lax.axis_index("x") + 2 * lax.axis_index("y") + lax.axis_index("c")
    row = lambda a, i: a[i:i + 1]

    big = ['a_pw1_w', 'a_pw2_w', 'b_group_w', 'c_in_w', 'c_out_w', 'f_down_w']
    small = ['a_pw1_b', 'a_dw_w', 'a_dw_b', 'a_ln_g', 'a_ln_b', 'a_pw2_b', 'c_conv_w']
    send = [p[k].astype(BF16) for k in big] + [_pad_last(p['f_up_w'], npad).astype(BF16)]
    send += [p[k] for k in small] + [_pad_last(p['f_dw_w'], npad), p['c']]
    got = _gather_two_level(send, name="gather_weights")
    gw = dict(zip(big + ['f_up_w'] + small + ['f_dw_w', 'c'], got))

    w1 = _cols_natural(gw['a_pw1_w'])
    b1 = _cols_natural(gw['a_pw1_b'])
    a_dw_w = _cols_natural(gw['a_dw_w'])
    a_dw_b, a_ln_g, a_ln_b, b2 = (_cols_natural(gw[k]) for k in ('a_dw_b', 'a_ln_g', 'a_ln_b', 'a_pw2_b'))
    w2 = _rows_natural(gw['a_pw2_w'])
    bgw = jnp.moveaxis(gw['b_group_w'], 0, 2).reshape((-1, G, gd, gd))
    cin = _cols_natural(gw['c_in_w'])
    cconv = _cols_natural(gw['c_conv_w'])
    cout = _rows_natural(gw['c_out_w'])
    wup_g, wup_v = _cols_natural(gw['f_up_w'][:4]), _cols_natural(gw['f_up_w'][4:])
    fdw_g, fdw_v = _cols_natural(gw['f_dw_w'][:4]), _cols_natural(gw['f_dw_w'][4:])
    pad_ff = lambda a: _pad_last(a.reshape(a.shape[:-1] + (4, ns)), npad).reshape(a.shape[:-1] + (Fp,))
    fdb_g, fdb_v = pad_ff(p['f_dw_b'][:, :4 * ns]), pad_ff(p['f_dw_b'][:, 4 * ns:])
    wdown = _rows_natural(gw['f_down_w'])
    wdown = jnp.pad(wdown.reshape(L, 4, ns, D), ((0, 0), (0, 0), (0, npad - ns), (0, 0))).reshape(L, Fp, D)

    c_all = gw['c'].reshape(N_DEV, D)
    n6 = p['mod_w'].shape[-1]
    mod_part = _mod_fwd(c_all, p['mod_w'], lax.dynamic_slice_in_dim(p['mod_b'], me * n6, n6, axis=1), name="mod_fwd")
    mod_all = _exchange([mod_part], scatter=False, name="gather_mod")[0]
    mod_all = jnp.moveaxis(mod_all, 0, 2).reshape(L, N_DEV, N_DEV * n6)
    mod = lax.dynamic_index_in_dim(mod_all, me, axis=1, keepdims=False).reshape(L, 6, D)

    saved = []
    x = x0
    h = _prenorm_fwd(x, row(p['norm_pre_mix'], 0), row(mod[0], 1), row(mod[0], 0), BF16, name="prenorm_first")
    dx = loss_cols = None
    for l in range(L):
        kind, slot = l % N_MIXERS, l // N_MIXERS
        sh_m, sc_m, gt_m, sh_f, sc_f, gt_f = (row(mod[l], i) for i in range(6))
        s = dict(x=x, h=h)
        if kind == 0:
            ua = _mm(h, w1[slot, :, :D], bias=b1[slot:slot + 1, :D], name=f"a_pw1_a_{l}")
            ug = _mm(h, w1[slot, :, D:], bias=b1[slot:slot + 1, D:], name=f"a_pw1_g_{l}")
            pa, z, q = _a_conv_fwd(ua, ug, a_dw_w[slot], a_dw_b[slot:slot + 1], a_ln_g[slot:slot + 1],
                                   a_ln_b[slot:slot + 1], name=f"a_conv_fwd_{l}")
            y = _mm(q, w2[slot], bias=b2[slot:slot + 1], name=f"a_pw2_{l}")
            s.update(ua=ua, ug=ug, p=pa, z=z, q=q)
        elif kind == 1:
            pooled = _pool_fwd(h, name=f"pool_fwd_{l}")
            mb, y = _group_mm(pooled, bgw[slot], bias=p['b_group_b'][slot:slot + 1],
                              scale=p['b_scale'][slot:slot + 1], name=f"b_mix_{l}")
            s.update(pooled=pooled, mb=mb)
        else:
            gb, gc, v = (_mm(h, cin[slot, :, i * D:(i + 1) * D], name=f"c_in_{i}_{l}") for i in range(3))
            pc, cv, sg = _c_gate_fwd(gb, gc, v, cconv[slot], name=f"c_gate_fwd_{l}")
            y = _mm(sg, cout[slot], name=f"c_out_{l}")
            s.update(gb=gb, gc=gc, v=v, p=pc, cv=cv, sg=sg)
        x1, h2 = _post_pre_fwd(x, y, row(p['norm_post_mix'], l), gt_m, row(p['norm_pre_ffn'], l), sc_f, sh_f, BF16,
                               name=f"post_mix_{l}")
        fug = _mm(h2, wup_g[l], name=f"f_up_g_{l}")
        fuv = _mm(h2, wup_v[l], name=f"f_up_v_{l}")
        act = _ffn_act_fwd(fug, fuv, fdw_g[l], fdw_v[l], fdb_g[l:l + 1], fdb_v[l:l + 1], name=f"ffn_act_fwd_{l}")
        y2 = _mm(act, wdown[l], name=f"f_down_{l}")
        s.update(y=y, x1=x1, h2=h2, fug=fug, fuv=fuv, act=act, y2=y2)
        saved.append(s)
        if l + 1 < L:
            nxt_dtype = F32 if (l + 1) % N_MIXERS == 1 else BF16
            x, h = _post_pre_fwd(x1, y2, row(p['norm_post_ffn'], l), gt_f, row(p['norm_pre_mix'], l + 1),
                                 row(mod[l + 1], 1), row(mod[l + 1], 0), nxt_dtype, name=f"post_ffn_{l}")
        else:
            dx, loss_cols = _post_loss(x1, y2, row(p['norm_post_ffn'], l), gt_f, target, name="post_loss")
    loss = lax.psum(0.5 * jnp.sum(loss_cols) / D, AXES)

    gfull = {}
    for k in ('a_pw1_w', 'a_pw1_b', 'a_dw_w', 'a_dw_b', 'a_ln_g', 'a_ln_b', 'a_pw2_w', 'a_pw2_b', 'b_group_w',
              'c_in_w', 'c_conv_w', 'c_out_w', 'f_up_w', 'f_dw_w', 'f_down_w'):
        gfull[k] = {}
    dmod, dnorm, d_fdb = {}, {}, {}
    d_bgb = d_bscale = None
    for l in reversed(range(L)):
        kind, slot = l % N_MIXERS, l // N_MIXERS
        s = saved[l]
        sh_m, sc_m, gt_m, sh_f, sc_f, gt_f = (row(mod[l], i) for i in range(6))
        dy2, dg4, dgt_f, _ = _post_bwd(s['y2'], dx, row(p['norm_post_ffn'], l), gt_f, BF16, name=f"post_ffn_bwd_{l}")
        da = _mm(dy2, wdown[l], nt=True, name=f"f_down_dx_{l}")
        dwd = _mm(_transpose(dy2, name=f"dy2_t_{l}"), s['act'], out_dtype=BF16, name=f"f_down_dw_{l}").T
        gfull['f_down_w'][l] = dwd.reshape(4, npad, D)[:, :ns].reshape(N_DEV, ns // 2, D)
        dcg, dcv, dbg, dbv = _ffn_act_bwd(s['fug'], s['fuv'], fdw_g[l], fdw_v[l], fdb_g[l:l + 1], fdb_v[l:l + 1], da,
                                          name=f"ffn_act_bwd_{l}")
        dug, dwg = _dwconv_bwd(s['fug'], dcg, fdw_g[l], BF16, name=f"f_dw_bwd_g_{l}")
        duv, dwv = _dwconv_bwd(s['fuv'], dcv, fdw_v[l], BF16, name=f"f_dw_bwd_v_{l}")
        dh2 = _mm(dug, wup_g[l], nt=True, name=f"f_up_dx_g_{l}")
        dh2 = _mm(duv, wup_v[l], nt=True, add=dh2, name=f"f_up_dx_v_{l}")
        h2t = _transpose(s['h2'], name=f"h2_t_{l}")
        gfull['f_up_w'][l] = jnp.concatenate(
            [_mm(h2t, dug, shard_cols=npad, out_dtype=BF16, name=f"f_up_dw_g_{l}"),
             _mm(h2t, duv, shard_cols=npad, out_dtype=BF16, name=f"f_up_dw_v_{l}")], axis=0)
        gfull['f_dw_w'][l] = jnp.concatenate([_col_slabs(dwg, 4), _col_slabs(dwv, 4)], axis=0)
        unpad = lambda a: a.reshape(4, npad)[:, :ns].reshape(4 * ns)
        d_fdb[l] = jnp.concatenate([unpad(dbg), unpad(dbv)])
        dx1, dsh_f, dsc_f, dg3 = _pre_bwd(s['x1'], dh2, dx, row(p['norm_pre_ffn'], l), sc_f, name=f"pre_ffn_bwd_{l}")
        dy_dtype = F32 if kind == 1 else BF16
        dy, dg2, dgt_m, dy_sum = _post_bwd(s['y'], dx1, row(p['norm_post_mix'], l), gt_m, dy_dtype,
                                           name=f"post_mix_bwd_{l}")
        if kind == 0:
            dq = _mm(dy, w2[slot], nt=True, name=f"a_pw2_dx_{l}")
            gfull['a_pw2_w'][slot] = _mm(_transpose(s['q'], name=f"q_t_{l}"), dy, out_dtype=BF16,
                                         name=f"a_pw2_dw_{l}").reshape(N_DEV, D // N_DEV, D)
            gfull['a_pw2_b'][slot] = dy_sum.reshape(N_DEV, D // N_DEV)
            dz, dlg, dlb, dzs = _a_ln_bwd(s['z'], dq, a_ln_g[slot:slot + 1], a_ln_b[slot:slot + 1], name=f"a_ln_bwd_{l}")
            gfull['a_ln_g'][slot], gfull['a_ln_b'][slot], gfull['a_dw_b'][slot] = (
                t.reshape(N_DEV, D // N_DEV) for t in (dlg, dlb, dzs))
            dp, ddw = _dwconv_bwd(s['p'], dz, a_dw_w[slot], F32, name=f"a_dw_bwd_{l}")
            gfull['a_dw_w'][slot] = _col_slabs(ddw)
            dua, dug_a, sua, sug = _a_glu_bwd(s['ua'], s['ug'], dp, name=f"a_glu_bwd_{l}")
            gfull['a_pw1_b'][slot] = jnp.concatenate([sua, sug], axis=1).reshape(N_DEV, 2 * D // N_DEV)
            dh = _mm(dua, w1[slot, :, :D], nt=True, name=f"a_pw1_dx_a_{l}")
            dh = _mm(dug_a, w1[slot, :, D:], nt=True, add=dh, name=f"a_pw1_dx_g_{l}")
            ht = _transpose(s['h'], name=f"h_t_{l}")
            n1 = 2 * D // N_DEV
            gfull['a_pw1_w'][slot] = jnp.concatenate(
                [_mm(ht, dua, shard_cols=n1, out_dtype=BF16, name=f"a_pw1_dw_a_{l}"),
                 _mm(ht, dug_a, shard_cols=n1, out_dtype=BF16, name=f"a_pw1_dw_g_{l}")], axis=0)
        elif kind == 1:
            dm, d_bscale, d_bgb = _b_scale_bwd(dy, s['mb'], p['b_scale'][slot:slot + 1], name=f"b_scale_bwd_{l}")
            dpool = _group_mm(dm, bgw[slot], nt=True, name=f"b_mix_dx_{l}")
            dgw = _group_dw(_transpose(s['pooled'], name=f"pooled_t_{l}"), dm, G, name=f"b_mix_dw_{l}")
            gfull['b_group_w'][slot] = jnp.moveaxis(dgw.astype(BF16).reshape(G, N_DEV, gd // N_DEV, gd), 1, 0)
            dh = _pool_bwd(dpool, name=f"pool_bwd_{l}")
        else:
            dsg = _mm(dy, cout[slot], nt=True, name=f"c_out_dx_{l}")
            gfull['c_out_w'][slot] = _mm(_transpose(s['sg'], name=f"sg_t_{l}"), dy, out_dtype=BF16,
                                         name=f"c_out_dw_{l}").reshape(N_DEV, D // N_DEV, D)
            dgb, dcv_c = _c_gate_bwd1(s['gb'], s['cv'], dsg, name=f"c_gate_bwd1_{l}")
            dpc, dcw = _dwconv_bwd(s['p'], dcv_c, cconv[slot], F32, name=f"c_conv_bwd_{l}")
            gfull['c_conv_w'][slot] = _col_slabs(dcw)
            dgc, dv = _c_gate_bwd2(s['gc'], s['v'], dpc, name=f"c_gate_bwd2_{l}")
            ht = _transpose(s['h'], name=f"h_t_{l}")
            dh, parts = None, []
            for i, d_i in enumerate((dgb, dgc, dv)):
                dh = _mm(d_i, cin[slot, :, i * D:(i + 1) * D], nt=True, add=dh, name=f"c_in_dx_{i}_{l}")
                parts.append(_mm(ht, d_i, out_dtype=BF16, name=f"c_in_dw_{i}_{l}"))
            gfull['c_in_w'][slot] = _col_slabs(jnp.concatenate(parts, axis=1))
        dx, dsh_m, dsc_m, dg1 = _pre_bwd(s['x'], dh, dx1, row(p['norm_pre_mix'], l), sc_m, name=f"pre_mix_bwd_{l}")
        dmod[l] = jnp.concatenate([dsh_m, dsc_m, dgt_m, dsh_f, dsc_f, dgt_f], axis=0)
        dnorm[l] = (dg1, dg2, dg3, dg4)

    zero_row = jnp.zeros((1, D), F32)
    rows_ = [dmod[l] for l in range(L)]
    rows_ += [dnorm[l][i] for i in range(4) for l in range(L)]
    rows_ += [d_bgb if d_bgb is not None else zero_row, d_bscale if d_bscale is not None else zero_row]
    fdb_flat = jnp.concatenate([d_fdb[l] for l in range(L)])
    rows_ += [_pad_last(fdb_flat, _round_up(fdb_flat.size, D)).reshape(-1, D)]
    pack = jnp.concatenate(rows_, axis=0)
    pack = jnp.pad(pack, ((0, _round_up(pack.shape[0], 8) - pack.shape[0]), (0, 0)))
    pack_all = _exchange([pack], scatter=False, name="gather_small_grads")[0]
    red = _sum_slabs(pack_all, name="sum_small_grads")
    g_rep = {'mod_b': red[:6 * L].reshape(L, 6 * D)}
    for i, k in enumerate(('norm_pre_mix', 'norm_post_mix', 'norm_pre_ffn', 'norm_post_ffn')):
        g_rep[k] = red[6 * L + i * L:6 * L + (i + 1) * L]
    o = 6 * L + 4 * L
    g_rep['b_group_b'], g_rep['b_scale'] = red[o:o + 1], red[o + 1:o + 2]
    g_rep['f_dw_b'] = red[o + 2:].reshape(-1)[:L * 8 * ns].reshape(L, 8 * ns)

    dmod_all = pack_all[:, :6 * L].reshape(N_DEV, L, 6 * D)
    dmod_mine = jnp.moveaxis(lax.dynamic_slice_in_dim(dmod_all, me * n6, n6, axis=2), 0, 1)
    g_mod_w = _mod_bwd(c_all.T, dmod_mine, name="mod_bwd")

    stack = lambda d: jnp.stack([d[i] for i in sorted(d)], axis=1)
    big_g = [k for k in gfull if gfull[k] and k.endswith('_w') and k not in ('a_dw_w', 'c_conv_w', 'f_dw_w')]
    small_g = [k for k in gfull if gfull[k] and k not in big_g]
    full = [stack(gfull[k]) for k in big_g]
    got = _swap_cores(full, name="scatter_grads_cores")
    core = lax.axis_index("c").astype(jnp.int32).reshape(1)
    partial = [_add_core_slabs(a, b, core, name=f"add_core_slabs_{k}") for k, a, b in zip(big_g, full, got)]
    slabs = dict(zip(big_g, _swap_chips(partial, name="scatter_grads_chips")))
    slabs.update(zip(small_g, _exchange([stack(gfull[k]) for k in small_g], scatter=True, name="scatter_small_grads")))

    out = {}
    for k in WEIGHTS:
        w, m, v = p[k], p['m_' + k], p['v_' + k]
        if k in g_rep:
            g = g_rep[k].reshape((1,) + w.shape)
        elif k == 'mod_w':
            g = g_mod_w.reshape((1,) + w.shape)
        else:
            g = slabs[k]
        if k in ('f_up_w', 'f_dw_w'):
            w, m, v = (_pad_last(t, npad) for t in (w, m, v))
        shape = w.shape
        two_d = (-1, shape[-1])
        res = _adamw(w.reshape(two_d), g.reshape((g.shape[0],) + (w.size // shape[-1], shape[-1])), m.reshape(two_d),
                     v.reshape(two_d), name=f"adamw_{k}")
        res = [t.reshape(shape) for t in res]
        if k in ('f_up_w', 'f_dw_w'):
            res = [t[..., :ns] for t in res]
        out[k] = res

    grads, deltas, new_m, new_v = ([out[k][i] for k in WEIGHTS] for i in range(4))
    return (loss, dx.reshape(1, T, D), *grads, *deltas, *new_m, *new_v)
```

```python
import functools

import jax
import jax.numpy as jnp
from jax import lax
from jax.experimental import pallas as pl
from jax.experimental.pallas import tpu as pltpu

F32, BF16 = jnp.float32, jnp.bfloat16
AXES = ("x", "y", "c")
N_DEV = 8
LANE = 128
VMEM_LIMIT = 48 * 1024 * 1024
RMS_EPS, LN_EPS = 1e-6, 1e-5
POOL_WINDOWS = (2, 4, 8, 16)
N_MIXERS = 3
ADAM_LR, ADAM_B1, ADAM_B2, ADAM_EPS, ADAM_WD, ADAM_STEP = 0.001, 0.9, 0.999, 1e-08, 0.01, 10

WEIGHTS = ['mod_w', 'mod_b', 'norm_pre_mix', 'norm_post_mix', 'norm_pre_ffn', 'norm_post_ffn', 'a_pw1_w', 'a_pw1_b',
           'a_dw_w', 'a_dw_b', 'a_ln_g', 'a_ln_b', 'a_pw2_w', 'a_pw2_b', 'b_group_w', 'b_group_b', 'b_scale', 'c_in_w',
           'c_conv_w', 'c_out_w', 'f_up_w', 'f_dw_w', 'f_dw_b', 'f_down_w']
INPUTS = (['x', 'c'] + WEIGHTS + ['loss_target'] + ['m_' + w for w in WEIGHTS] + ['v_' + w for w in WEIGHTS])


def _tile(n, cap, mult):
    for t in range(min(n, cap), 0, -1):
        if n % t == 0 and t % mult == 0:
            return t
    return n


def _round_up(n, m):
    return -(-n // m) * m


def _params(*dims):
    return pltpu.CompilerParams(dimension_semantics=dims or None, vmem_limit_bytes=VMEM_LIMIT)


def _sigmoid(v):
    return jax.nn.sigmoid(v)


def _exchange(arrs, *, scatter, name):
    n = len(arrs)

    def body(*refs):
        ins, outs = refs[:n], refs[n:2 * n]
        send_sems, recv_sems, local_sems = refs[2 * n:]
        x, y, c = (lax.axis_index(a) for a in AXES)
        me = 4 * x + 2 * y + c

        def peer(k):
            px = 1 - x if (k >> 2) & 1 else x
            py = 1 - y if (k >> 1) & 1 else y
            pc = 1 - c if k & 1 else c
            return (px, py, pc), 4 * px + 2 * py + pc

        def remote(i, k, src, dst):
            return pltpu.make_async_remote_copy(
                src_ref=src, dst_ref=dst, send_sem=send_sems.at[i, k - 1], recv_sem=recv_sems.at[i, k - 1],
                device_id=peer(k)[0], device_id_type=pl.DeviceIdType.MESH)

        own = []
        for i in range(n):
            cp = pltpu.make_async_copy(ins[i].at[me] if scatter else ins[i], outs[i].at[me], local_sems.at[i])
            cp.start()
            own.append(cp)
        sent = []
        for k in range(1, N_DEV):
            for i in range(n):
                cp = remote(i, k, ins[i].at[peer(k)[1]] if scatter else ins[i], outs[i].at[me])
                cp.start()
                sent.append(cp)
        for k in range(1, N_DEV):
            for i in range(n):
                landing = outs[i].at[peer(k)[1]]
                remote(i, k, landing, landing).wait_recv()
        for cp in sent:
            cp.wait_send()
        for cp in own:
            cp.wait()

    any_spec = pl.BlockSpec(memory_space=pl.ANY)
    out_shape = [jax.ShapeDtypeStruct(a.shape if scatter else (N_DEV,) + a.shape, a.dtype) for a in arrs]
    return pl.pallas_call(
        body, name=name, out_shape=out_shape, in_specs=[any_spec] * n, out_specs=[any_spec] * n,
        scratch_shapes=[pltpu.SemaphoreType.DMA((n, N_DEV - 1)), pltpu.SemaphoreType.DMA((n, N_DEV - 1)),
                        pltpu.SemaphoreType.DMA((n,))],
    )(*arrs)


def _comm_call(body, arrs, out_shape, sem_shapes, name):
    any_spec = pl.BlockSpec(memory_space=pl.ANY)
    return pl.pallas_call(
        body, name=name, out_shape=out_shape, in_specs=[any_spec] * len(arrs), out_specs=[any_spec] * len(out_shape),
        scratch_shapes=[pltpu.SemaphoreType.DMA(s) for s in sem_shapes],
    )(*arrs)


def _gather_two_level(arrs, *, name):
    n = len(arrs)

    def body(*refs):
        ins, outs = refs[:n], refs[n:2 * n]
        send_sems, recv_sems, local_sems = refs[2 * n:]
        x, y, c = (lax.axis_index(a) for a in AXES)
        index = lambda px, py, pc: 4 * px + 2 * py + pc
        me, sibling = index(x, y, c), (x, y, 1 - c)
        chips = [(1 - x, y), (x, 1 - y), (1 - x, 1 - y)]

        def copy(i, k, block, to, src=None):
            return pltpu.make_async_remote_copy(
                src_ref=outs[i].at[block] if src is None else src, dst_ref=outs[i].at[block],
                send_sem=send_sems.at[i, k], recv_sem=recv_sems.at[i, k], device_id=to,
                device_id_type=pl.DeviceIdType.MESH)

        own = [pltpu.make_async_copy(ins[i], outs[i].at[me], local_sems.at[i]) for i in range(n)]
        first = [copy(i, 1 + j, me, (*chip, c), src=ins[i]) for j, chip in enumerate(chips) for i in range(n)]
        first += [copy(i, 0, me, sibling, src=ins[i]) for i in range(n)]
        for cp in own + first:
            cp.start()
        passed = []
        for j, chip in enumerate(chips):
            block = index(*chip, c)
            for i in range(n):
                copy(i, 1 + j, block, (x, y, c)).wait_recv()
            for i in range(n):
                cp = copy(i, 4 + j, block, sibling)
                cp.start()
                passed.append(cp)
        for i in range(n):
            copy(i, 0, index(x, y, 1 - c), (x, y, c)).wait_recv()
        for j, chip in enumerate(chips):
            for i in range(n):
                copy(i, 4 + j, index(*chip, 1 - c), (x, y, c)).wait_recv()
        for cp in first + passed:
            cp.wait_send()
        for cp in own:
            cp.wait()

    out_shape = [jax.ShapeDtypeStruct((N_DEV,) + a.shape, a.dtype) for a in arrs]
    return _comm_call(body, arrs, out_shape, [(n, 7), (n, 7), (n,)], name)


def _swap_cores(arrs, *, name):
    n = len(arrs)

    def body(*refs):
        ins, got = refs[:n], refs[n:2 * n]
        send_sems, recv_sems = refs[2 * n:]
        x, y, c = (lax.axis_index(a) for a in AXES)
        remote = [pltpu.make_async_remote_copy(
            src_ref=ins[i].at[2 * q + 1 - c], dst_ref=got[i].at[q], send_sem=send_sems.at[i, q],
            recv_sem=recv_sems.at[i, q], device_id=(x, y, 1 - c), device_id_type=pl.DeviceIdType.MESH)
            for q in range(4) for i in range(n)]
        for cp in remote:
            cp.start()
        for cp in remote:
            cp.wait_recv()
        for cp in remote:
            cp.wait_send()

    half = [jax.ShapeDtypeStruct((4,) + a.shape[1:], a.dtype) for a in arrs]
    return _comm_call(body, arrs, half, [(n, 4), (n, 4)], name)


def _swap_chips(arrs, *, name):
    n = len(arrs)

    def body(*refs):
        ins, outs = refs[:n], refs[n:2 * n]
        send_sems, recv_sems, local_sems = refs[2 * n:]
        x, y, c = (lax.axis_index(a) for a in AXES)
        mine = 2 * x + y

        def peer(k):
            px = 1 - x if k & 2 else x
            py = 1 - y if k & 1 else y
            return (px, py, c), 2 * px + py

        def remote(i, k, src, dst):
            return pltpu.make_async_remote_copy(
                src_ref=src, dst_ref=dst, send_sem=send_sems.at[i, k - 1], recv_sem=recv_sems.at[i, k - 1],
                device_id=peer(k)[0], device_id_type=pl.DeviceIdType.MESH)

        own = [pltpu.make_async_copy(ins[i].at[mine], outs[i].at[mine], local_sems.at[i]) for i in range(n)]
        sent = [remote(i, k, ins[i].at[peer(k)[1]], outs[i].at[mine]) for k in range(1, 4) for i in range(n)]
        for cp in own + sent:
            cp.start()
        for k in range(1, 4):
            for i in range(n):
                landing = outs[i].at[peer(k)[1]]
                remote(i, k, landing, landing).wait_recv()
        for cp in sent:
            cp.wait_send()
        for cp in own:
            cp.wait()

    out_shape = [jax.ShapeDtypeStruct(a.shape, a.dtype) for a in arrs]
    return _comm_call(body, arrs, out_shape, [(n, 3), (n, 3), (n,)], name)


def _add_core_slabs(full, got, core, *, name):
    C = got.shape[-1]
    R = got[0].size // C
    tr = _tile(R, max(16, (1 << 19) // C), 16)

    def body(core_ref, a_ref, b_ref, o_ref):
        o_ref[...] = (a_ref[...].astype(F32) + b_ref[...].astype(F32)).astype(o_ref.dtype)

    blk = pl.BlockSpec((None, tr, C), lambda q, i, core_ref: (q, i, 0))
    grid_spec = pltpu.PrefetchScalarGridSpec(
        num_scalar_prefetch=1, grid=(4, R // tr),
        in_specs=[pl.BlockSpec((None, None, tr, C), lambda q, i, core_ref: (q, core_ref[0], i, 0)), blk], out_specs=blk)
    return pl.pallas_call(
        body, name=name, grid_spec=grid_spec, out_shape=jax.ShapeDtypeStruct((4, R, C), got.dtype),
        compiler_params=_params("parallel", "parallel"),
    )(core, full.reshape(4, 2, R, C), got.reshape(4, R, C)).reshape(got.shape)


def _mm(a, b, *, name, nt=False, bias=None, add=None, out_dtype=F32, shard_cols=0):
    M, K = a.shape
    N = b.shape[0] if nt else b.shape[1]
    tm, tn = _tile(M, 1024, 16), shard_cols or _tile(N, 1024, LANE)

    def vmem_bytes(tk):
        blocks = 2 * (tm * tk + tk * tn) + tm * tn * jnp.dtype(out_dtype).itemsize + (4 * tm * tn if add is not None else 0)
        return 2 * blocks + (4 * tm * tn if tk < K else 0)

    tk = next((t for t in (_tile(K, 3072, LANE), _tile(K, 2048, LANE)) if vmem_bytes(t) <= VMEM_LIMIT - (12 << 20)),
              _tile(K, 1024, LANE))
    nk = K // tk
    dims = (((1,), (1,)), ((), ())) if nt else (((1,), (0,)), ((), ()))

    def body(*refs):
        refs = list(refs)
        a_ref, b_ref = refs.pop(0), refs.pop(0)
        bias_ref = refs.pop(0) if bias is not None else None
        add_ref = refs.pop(0) if add is not None else None
        o_ref = refs.pop(0)
        part = lax.dot_general(a_ref[...], b_ref[...], dims, preferred_element_type=F32)

        def finish(val):
            if bias_ref is not None:
                val = val + bias_ref[...]
            if add_ref is not None:
                val = val + add_ref[...]
            o_ref[...] = val.astype(o_ref.dtype)

        if nk == 1:
            finish(part)
        else:
            acc_ref = refs.pop(0)
            k = pl.program_id(2)

            @pl.when(k == 0)
            def _():
                acc_ref[...] = part

            @pl.when(k > 0)
            def _():
                acc_ref[...] += part

            @pl.when(k == nk - 1)
            def _():
                finish(acc_ref[...])

    in_specs = [pl.BlockSpec((tm, tk), lambda i, j, k: (i, k)),
                pl.BlockSpec((tn, tk), lambda i, j, k: (j, k)) if nt else pl.BlockSpec((tk, tn), lambda i, j, k: (k, j))]
    operands = [a, b]
    if bias is not None:
        in_specs.append(pl.BlockSpec((1, tn), lambda i, j, k: (0, j)))
        operands.append(bias)
    if add is not None:
        in_specs.append(pl.BlockSpec((tm, tn), lambda i, j, k: (i, j)))
        operands.append(add)
    if shard_cols:
        out_shape = jax.ShapeDtypeStruct((N // tn, M, tn), out_dtype)
        out_spec = pl.BlockSpec((None, tm, tn), lambda i, j, k: (j, i, 0))
    else:
        out_shape = jax.ShapeDtypeStruct((M, N), out_dtype)
        out_spec = pl.BlockSpec((tm, tn), lambda i, j, k: (i, j))
    return pl.pallas_call(
        body, name=name, grid=(M // tm, N // tn, nk), in_specs=in_specs, out_specs=out_spec, out_shape=out_shape,
        scratch_shapes=[pltpu.VMEM((tm, tn), F32)] if nk > 1 else [],
        compiler_params=_params("parallel", "parallel", "arbitrary"),
    )(*operands)


def _transpose(x, *, name):
    T, C = x.shape
    tt, tc = _tile(T, 512, LANE), _tile(C, 512, LANE)

    def body(x_ref, o_ref):
        o_ref[...] = x_ref[...].astype(F32).T.astype(BF16)

    return pl.pallas_call(
        body, name=name, grid=(T // tt, C // tc), in_specs=[pl.BlockSpec((tt, tc), lambda i, j: (i, j))],
        out_specs=pl.BlockSpec((tc, tt), lambda i, j: (j, i)), out_shape=jax.ShapeDtypeStruct((C, T), BF16),
        compiler_params=_params("parallel", "parallel"),
    )(x)


def _group_mm(a, w, *, name, nt=False, bias=None, scale=None):
    T, D = a.shape
    G, gd, _ = w.shape
    tm = _tile(T, 1024, 16)
    dims = (((1,), (1,)), ((), ())) if nt else (((1,), (0,)), ((), ()))
    fused = bias is not None

    def body(*refs):
        if fused:
            a_ref, w_ref, b_ref, s_ref, mb_ref, y_ref = refs
        else:
            a_ref, w_ref, y_ref = refs
        val = lax.dot_general(a_ref[...], w_ref[...], dims, preferred_element_type=F32)
        if fused:
            val = val + b_ref[...]
            mb_ref[...] = val
            val = val * s_ref[...]
        y_ref[...] = val

    blk = pl.BlockSpec((tm, gd), lambda i, g: (i, g))
    vec = pl.BlockSpec((1, gd), lambda i, g: (0, g))
    in_specs = [blk, pl.BlockSpec((None, gd, gd), lambda i, g: (g, 0, 0))] + ([vec, vec] if fused else [])
    td = jax.ShapeDtypeStruct((T, D), F32)
    return pl.pallas_call(
        body, name=name, grid=(T // tm, G), in_specs=in_specs, out_specs=[blk, blk] if fused else blk,
        out_shape=[td, td] if fused else td, compiler_params=_params("parallel", "parallel"),
    )(*([a, w, bias, scale] if fused else [a, w]))


def _group_dw(at, b, groups, *, name):
    D, T = at.shape
    gd = D // groups
    tk = _tile(T, 2048, LANE)

    def body(a_ref, b_ref, o_ref):
        @pl.when(pl.program_id(1) == 0)
        def _():
            o_ref[...] = jnp.zeros_like(o_ref)

        o_ref[...] += jnp.dot(a_ref[...], b_ref[...], preferred_element_type=F32)

    return pl.pallas_call(
        body, name=name, grid=(groups, T // tk),
        in_specs=[pl.BlockSpec((gd, tk), lambda g, k: (g, k)), pl.BlockSpec((tk, gd), lambda g, k: (k, g))],
        out_specs=pl.BlockSpec((None, gd, gd), lambda g, k: (g, 0, 0)),
        out_shape=jax.ShapeDtypeStruct((groups, gd, gd), F32), compiler_params=_params("parallel", "arbitrary"),
    )(at, b)


def _mod_fwd(c_all, w, b, *, name):
    L, D, n = w.shape
    B = c_all.shape[0]

    def body(c_ref, w_ref, b_ref, o_ref):
        cv = c_ref[...]
        o_ref[...] = jnp.dot(cv * _sigmoid(cv), w_ref[...], preferred_element_type=F32) + b_ref[...]

    return pl.pallas_call(
        body, name=name, grid=(L,),
        in_specs=[pl.BlockSpec((B, D), lambda l: (0, 0)), pl.BlockSpec((None, D, n), lambda l: (l, 0, 0)),
                  pl.BlockSpec((None, 1, n), lambda l: (l, 0, 0))],
        out_specs=pl.BlockSpec((None, B, n), lambda l: (l, 0, 0)), out_shape=jax.ShapeDtypeStruct((L, B, n), F32),
        compiler_params=_params("parallel"),
    )(c_all, w, b.reshape(L, 1, n))


def _mod_bwd(c_all_t, dmod, *, name):
    L, B, n = dmod.shape
    D = c_all_t.shape[0]

    def body(c_ref, d_ref, o_ref):
        cv = c_ref[...]
        o_ref[...] = jnp.dot(cv * _sigmoid(cv), d_ref[...], preferred_element_type=F32)

    return pl.pallas_call(
        body, name=name, grid=(L,),
        in_specs=[pl.BlockSpec((D, B), lambda l: (0, 0)), pl.BlockSpec((None, B, n), lambda l: (l, 0, 0))],
        out_specs=pl.BlockSpec((None, D, n), lambda l: (l, 0, 0)), out_shape=jax.ShapeDtypeStruct((L, D, n), F32),
        compiler_params=_params("parallel"),
    )(c_all_t, dmod)


def _rowwise(fn, tiles, vecs, out_dtypes, n_sums, *, name):
    T, C = tiles[0].shape
    tr = _tile(T, 512, 16)
    ch = _tile(tr, 32, 16)
    nt_, nv, no = len(tiles), len(vecs), len(out_dtypes)

    def body(*refs):
        t_refs, v_refs = refs[:nt_], refs[nt_:nt_ + nv]
        o_refs, s_refs = refs[nt_ + nv:nt_ + nv + no], refs[nt_ + nv + no:]
        vv = [v[...] for v in v_refs]
        acc = [None] * n_sums
        for r0 in range(0, tr, ch):
            rows = pl.ds(r0, ch)
            o_vals, s_vals = fn([t[rows, :] for t in t_refs], vv)
            for o, val in zip(o_refs, o_vals):
                o[rows, :] = val.astype(o.dtype)
            for i, val in enumerate(s_vals):
                part = jnp.sum(val, axis=0, keepdims=True)
                acc[i] = part if acc[i] is None else acc[i] + part
        if n_sums:
            first = pl.program_id(0) == 0

            @pl.when(first)
            def _():
                for s, val in zip(s_refs, acc):
                    s[...] = val

            @pl.when(jnp.logical_not(first))
            def _():
                for s, val in zip(s_refs, acc):
                    s[...] += val

    tile_spec = pl.BlockSpec((tr, C), lambda i: (i, 0))
    vec_spec = pl.BlockSpec((1, C), lambda i: (0, 0))
    res = pl.pallas_call(
        body, name=name, grid=(T // tr,), in_specs=[tile_spec] * nt_ + [vec_spec] * nv,
        out_specs=[tile_spec] * no + [vec_spec] * n_sums,
        out_shape=[jax.ShapeDtypeStruct((T, C), d) for d in out_dtypes] + [jax.ShapeDtypeStruct((1, C), F32)] * n_sums,
        compiler_params=_params("arbitrary"),
    )(*tiles, *vecs)
    return res


def _rms(v):
    return lax.rsqrt(jnp.mean(v * v, axis=-1, keepdims=True) + RMS_EPS)


def _prenorm_fwd(x, g, sc, sh, out_dtype, *, name):
    def fn(t, v):
        (xv,), (g_, sc_, sh_) = t, v
        return [(xv * _rms(xv)) * g_ * (1.0 + sc_) + sh_], []

    return _rowwise(fn, [x], [g, sc, sh], [out_dtype], 0, name=name)[0]


def _post_pre_fwd(x, y, gp, gate, g, sc, sh, out_dtype, *, name):
    def fn(t, v):
        (xv, yv), (gp_, gate_, g_, sc_, sh_) = t, v
        x1 = xv + gate_ * ((yv * _rms(yv)) * gp_)
        return [x1, (x1 * _rms(x1)) * g_ * (1.0 + sc_) + sh_], []

    return _rowwise(fn, [x, y], [gp, gate, g, sc, sh], [F32, out_dtype], 0, name=name)


def _post_loss(x, y, gp, gate, target, *, name):
    D = x.shape[1]

    def fn(t, v):
        (xv, yv, tv), (gp_, gate_) = t, v
        err = xv + gate_ * ((yv * _rms(yv)) * gp_) - tv
        return [err / D], [err * err]

    return _rowwise(fn, [x, y, target], [gp, gate], [F32], 1, name=name)


def _post_bwd(y, dxo, gp, gate, out_dtype, *, name):
    def fn(t, v):
        (yv, dv), (gp_, gate_) = t, v
        r = _rms(yv)
        yn = yv * r
        dyn = dv * gate_ * gp_
        dy = r * (dyn - yn * jnp.mean(dyn * yn, axis=-1, keepdims=True))
        return [dy], [dv * gate_ * yn, dv * yn * gp_, dy]

    return _rowwise(fn, [y, dxo], [gp, gate], [out_dtype], 3, name=name)


def _pre_bwd(x, dh, dres, g, sc, *, name):
    def fn(t, v):
        (xv, dhv, dr), (g_, sc_) = t, v
        dhv = dhv.astype(F32)
        r = _rms(xv)
        xn = xv * r
        dxn = dhv * g_ * (1.0 + sc_)
        dx = dr + r * (dxn - xn * jnp.mean(dxn * xn, axis=-1, keepdims=True))
        return [dx], [dhv, dhv * xn * g_, dhv * xn * (1.0 + sc_)]

    return _rowwise(fn, [x, dh, dres], [g, sc], [F32], 3, name=name)


def _a_ln_bwd(z, dq, g, b, *, name):
    def fn(t, v):
        (zv, dqv), (g_, b_) = t, v
        mu = jnp.mean(zv, axis=-1, keepdims=True)
        zc = zv - mu
        rstd = lax.rsqrt(jnp.mean(zc * zc, axis=-1, keepdims=True) + LN_EPS)
        zn = zc * rstd
        l = zn * g_ + b_
        s = _sigmoid(l)
        dl = dqv * (s * (1.0 + l * (1.0 - s)))
        dzn = dl * g_
        dz = rstd * (dzn - jnp.mean(dzn, axis=-1, keepdims=True) - zn * jnp.mean(dzn * zn, axis=-1, keepdims=True))
        return [dz], [dl * zn, dl, dz]

    return _rowwise(fn, [z, dq], [g, b], [F32], 3, name=name)


def _a_glu_bwd(ua, ug, dp, *, name):
    def fn(t, v):
        uav, ugv, dpv = t
        s = _sigmoid(ugv)
        dua = dpv * s
        dug = dpv * uav * s * (1.0 - s)
        return [dua, dug], [dua, dug]

    return _rowwise(fn, [ua, ug, dp], [], [BF16, BF16], 2, name=name)


def _b_scale_bwd(dy, mb, scale, *, name):
    def fn(t, v):
        (dyv, mbv), (s_,) = t, v
        dm = dyv * s_
        return [dm], [dyv * mbv, dm]

    return _rowwise(fn, [dy, mb], [scale], [BF16], 2, name=name)


def _c_gate_bwd1(gb, cv, ds, *, name):
    def fn(t, v):
        gbv, cvv, dsv = t
        return [dsv * cvv, dsv * gbv], []

    return _rowwise(fn, [gb, cv, ds], [], [BF16, F32], 0, name=name)


def _c_gate_bwd2(gc, v_, dp, *, name):
    def fn(t, v):
        gcv, vv, dpv = t
        return [dpv * vv, dpv * gcv], []

    return _rowwise(fn, [gc, v_, dp], [], [BF16, BF16], 0, name=name)


def _stage_causal(buf, t, halo, tt):
    @pl.when(t == 0)
    def _():
        buf[pl.ds(0, halo), :] = jnp.zeros((halo, buf.shape[1]), F32)

    @pl.when(t > 0)
    def _():
        buf[pl.ds(0, halo), :] = buf[pl.ds(tt, halo), :]


def _stage_anticausal(buf, t, halo, tt):
    @pl.when(t == 0)
    def _():
        buf[pl.ds(tt, halo), :] = jnp.zeros((halo, buf.shape[1]), F32)

    @pl.when(t > 0)
    def _():
        buf[pl.ds(tt, halo), :] = buf[pl.ds(0, halo), :]


SUBLANES = 8


def _make_shifted(buf, shifted):
    n = shifted.shape[1]
    for b in range(1, SUBLANES):
        shifted[b - 1, :, :] = buf[pl.ds(b, n), :]


def _rows_at(buf, shifted, offset, rows, cols):
    b = offset % SUBLANES
    if shifted is None or b == 0:
        return buf[pl.ds(offset, rows), cols]
    return shifted[b - 1, pl.ds(offset - b, rows), cols]


def _causal_taps(buf, w_ref, taps, halo, r0, rows, cols, shifted=None):
    acc = None
    for k in range(taps):
        term = w_ref[pl.ds(k, 1), cols] * _rows_at(buf, shifted, halo - (taps - 1) + k + r0, rows, cols)
        acc = term if acc is None else acc + term
    return acc


def _halo_tiles(T, C, halo, col_cap):
    tt = _tile(T, 512 if halo <= 8 else 256, 16)
    assert tt >= halo, (tt, halo)
    return tt, _tile(C, col_cap, LANE), _tile(tt, 128, 16)


def _col_blocks(tc):
    cb = LANE if tc % LANE == 0 else tc
    return [pl.ds(c0, cb) for c0 in range(0, tc, cb)]


def _ffn_act_fwd(ug, uv, wg, wv, bg, bv, *, name):
    T, F = ug.shape
    taps, halo = wg.shape[0], 8
    tt, tc, rc = _halo_tiles(T, F, halo, 1024)

    def body(ug_ref, uv_ref, wg_ref, wv_ref, bg_ref, bv_ref, a_ref, bufg, bufv):
        t = pl.program_id(1)
        for u_ref, buf in ((ug_ref, bufg), (uv_ref, bufv)):
            _stage_causal(buf, t, halo, tt)
            buf[pl.ds(halo, tt), :] = u_ref[...]
        for cols in _col_blocks(tc):
            for r0 in range(0, tt, rc):
                yg = _causal_taps(bufg, wg_ref, taps, halo, r0, rc, cols) + bg_ref[:, cols]
                yv = _causal_taps(bufv, wv_ref, taps, halo, r0, rc, cols) + bv_ref[:, cols]
                a_ref[pl.ds(r0, rc), cols] = (yg * _sigmoid(yg) * yv).astype(BF16)

    blk = pl.BlockSpec((tt, tc), lambda j, t: (t, j))
    wspec = pl.BlockSpec((taps, tc), lambda j, t: (0, j))
    vspec = pl.BlockSpec((1, tc), lambda j, t: (0, j))
    return pl.pallas_call(
        body, name=name, grid=(F // tc, T // tt), in_specs=[blk, blk, wspec, wspec, vspec, vspec], out_specs=blk,
        out_shape=jax.ShapeDtypeStruct((T, F), BF16),
        scratch_shapes=[pltpu.VMEM((halo + tt, tc), F32), pltpu.VMEM((halo + tt, tc), F32)],
        compiler_params=_params("parallel", "arbitrary"),
    )(ug, uv, wg, wv, bg, bv)


def _ffn_act_bwd(ug, uv, wg, wv, bg, bv, da, *, name):
    T, F = ug.shape
    taps, halo = wg.shape[0], 8
    tt, tc, rc = _halo_tiles(T, F, halo, 1024)

    def body(ug_ref, uv_ref, wg_ref, wv_ref, bg_ref, bv_ref, da_ref, dg_ref, dv_ref, sg_ref, sv_ref, bufg, bufv):
        t = pl.program_id(1)

        @pl.when(t == 0)
        def _():
            sg_ref[...] = jnp.zeros_like(sg_ref)
            sv_ref[...] = jnp.zeros_like(sv_ref)

        for u_ref, buf in ((ug_ref, bufg), (uv_ref, bufv)):
            _stage_causal(buf, t, halo, tt)
            buf[pl.ds(halo, tt), :] = u_ref[...]
        for cols in _col_blocks(tc):
            for r0 in range(0, tt, rc):
                rows = pl.ds(r0, rc)
                yg = _causal_taps(bufg, wg_ref, taps, halo, r0, rc, cols) + bg_ref[:, cols]
                yv = _causal_taps(bufv, wv_ref, taps, halo, r0, rc, cols) + bv_ref[:, cols]
                s = _sigmoid(yg)
                d = da_ref[rows, cols]
                dyg = d * yv * (s * (1.0 + yg * (1.0 - s)))
                dyv = d * (yg * s)
                dg_ref[rows, cols] = dyg
                dv_ref[rows, cols] = dyv
                sg_ref[:, cols] += jnp.sum(dyg, axis=0, keepdims=True)
                sv_ref[:, cols] += jnp.sum(dyv, axis=0, keepdims=True)

    blk = pl.BlockSpec((tt, tc), lambda j, t: (t, j))
    wspec = pl.BlockSpec((taps, tc), lambda j, t: (0, j))
    vspec = pl.BlockSpec((1, tc), lambda j, t: (0, j))
    td, vd = jax.ShapeDtypeStruct((T, F), F32), jax.ShapeDtypeStruct((1, F), F32)
    return pl.pallas_call(
        body, name=name, grid=(F // tc, T // tt), in_specs=[blk, blk, wspec, wspec, vspec, vspec, blk],
        out_specs=[blk, blk, vspec, vspec], out_shape=[td, td, vd, vd],
        scratch_shapes=[pltpu.VMEM((halo + tt, tc), F32), pltpu.VMEM((halo + tt, tc), F32)],
        compiler_params=_params("parallel", "arbitrary"),
    )(ug, uv, wg, wv, bg, bv, da)


def _dwconv_bwd(x, dy, w, out_dtype, *, name):
    T, C = x.shape
    taps = w.shape[0]
    halo = _round_up(taps - 1, 8)
    tt, tc, rc = _halo_tiles(T, C, halo, 512)
    nt_ = T // tt
    long_conv = taps > SUBLANES

    def body(x_ref, dy_ref, w_ref, dx_ref, dw_ref, buf, *scratch):
        t = pl.program_id(1)
        shifted = scratch[0] if long_conv else None

        @pl.when(t == 0)
        def _():
            dw_ref[...] = jnp.zeros_like(dw_ref)

        _stage_anticausal(buf, t, halo, tt)
        buf[pl.ds(0, tt), :] = dy_ref[...]
        if long_conv:
            _make_shifted(buf, shifted)
        for cols in _col_blocks(tc):
            for r0 in range(0, tt, rc):
                rows = pl.ds(r0, rc)
                xv = x_ref[rows, cols]
                acc = None
                for j in range(taps):
                    k = taps - 1 - j
                    ahead = _rows_at(buf, shifted, r0 + j, rc, cols)
                    term = w_ref[pl.ds(k, 1), cols] * ahead
                    acc = term if acc is None else acc + term
                    dw_ref[pl.ds(k, 1), cols] += jnp.sum(xv * ahead, axis=0, keepdims=True)
                dx_ref[rows, cols] = acc.astype(dx_ref.dtype)

    blk = pl.BlockSpec((tt, tc), lambda j, t: (nt_ - 1 - t, j))
    wspec = pl.BlockSpec((taps, tc), lambda j, t: (0, j))
    return pl.pallas_call(
        body, name=name, grid=(C // tc, nt_), in_specs=[blk, blk, wspec], out_specs=[blk, wspec],
        out_shape=[jax.ShapeDtypeStruct((T, C), out_dtype), jax.ShapeDtypeStruct((taps, C), F32)],
        scratch_shapes=[pltpu.VMEM((tt + halo, tc), F32)] + (
            [pltpu.VMEM((SUBLANES - 1, tt + halo - SUBLANES, tc), F32)] if long_conv else []),
        compiler_params=_params("parallel", "arbitrary"),
    )(x, dy, w)


def _a_conv_fwd(ua, ug, w, b, ln_g, ln_b, *, name):
    T, D = ua.shape
    taps = w.shape[0]
    halo = _round_up(taps - 1, 8)
    tt = _tile(T, 128, 16)
    assert tt >= halo
    rc, ch = _tile(tt, 128, 16), _tile(tt, 32, 16)

    def body(ua_ref, ug_ref, w_ref, b_ref, g_ref, lb_ref, p_ref, z_ref, q_ref, buf, shifted):
        t = pl.program_id(0)
        _stage_causal(buf, t, halo, tt)
        for r0 in range(0, tt, ch):
            rows = pl.ds(r0, ch)
            pv = ua_ref[rows, :] * _sigmoid(ug_ref[rows, :])
            p_ref[rows, :] = pv
            buf[pl.ds(halo + r0, ch), :] = pv
        _make_shifted(buf, shifted)
        for cols in _col_blocks(D):
            for r0 in range(0, tt, rc):
                z_ref[pl.ds(r0, rc), cols] = (
                    _causal_taps(buf, w_ref, taps, halo, r0, rc, cols, shifted) + b_ref[:, cols])
        for r0 in range(0, tt, ch):
            rows = pl.ds(r0, ch)
            zv = z_ref[rows, :]
            zc = zv - jnp.mean(zv, axis=-1, keepdims=True)
            l = zc * lax.rsqrt(jnp.mean(zc * zc, axis=-1, keepdims=True) + LN_EPS) * g_ref[...] + lb_ref[...]
            q_ref[rows, :] = (l * _sigmoid(l)).astype(BF16)

    blk = pl.BlockSpec((tt, D), lambda t: (t, 0))
    vec = pl.BlockSpec((1, D), lambda t: (0, 0))
    td = jax.ShapeDtypeStruct((T, D), F32)
    return pl.pallas_call(
        body, name=name, grid=(T // tt,),
        in_specs=[blk, blk, pl.BlockSpec((taps, D), lambda t: (0, 0)), vec, vec, vec], out_specs=[blk, blk, blk],
        out_shape=[td, td, jax.ShapeDtypeStruct((T, D), BF16)],
        scratch_shapes=[pltpu.VMEM((halo + tt, D), F32), pltpu.VMEM((SUBLANES - 1, halo + tt - SUBLANES, D), F32)],
        compiler_params=_params("arbitrary"),
    )(ua, ug, w, b, ln_g, ln_b)


def _c_gate_fwd(gb, gc, v, w, *, name):
    T, D = gb.shape
    taps, halo = w.shape[0], 8
    tt, tc, rc = _halo_tiles(T, D, halo, 1024)

    def body(gb_ref, gc_ref, v_ref, w_ref, p_ref, cv_ref, s_ref, buf):
        t = pl.program_id(1)
        _stage_causal(buf, t, halo, tt)
        pv = gc_ref[...] * v_ref[...]
        p_ref[...] = pv
        buf[pl.ds(halo, tt), :] = pv
        for cols in _col_blocks(tc):
            for r0 in range(0, tt, rc):
                rows = pl.ds(r0, rc)
                cv = _causal_taps(buf, w_ref, taps, halo, r0, rc, cols)
                cv_ref[rows, cols] = cv
                s_ref[rows, cols] = (gb_ref[rows, cols] * cv).astype(BF16)

    blk = pl.BlockSpec((tt, tc), lambda j, t: (t, j))
    td = jax.ShapeDtypeStruct((T, D), F32)
    return pl.pallas_call(
        body, name=name, grid=(D // tc, T // tt),
        in_specs=[blk, blk, blk, pl.BlockSpec((taps, tc), lambda j, t: (0, j))], out_specs=[blk, blk, blk],
        out_shape=[td, td, jax.ShapeDtypeStruct((T, D), BF16)],
        scratch_shapes=[pltpu.VMEM((halo + tt, tc), F32)], compiler_params=_params("parallel", "arbitrary"),
    )(gb, gc, v, w)


def _pool_count(base, r0, rows, width, window):
    pos = lax.broadcasted_iota(jnp.int32, (rows, width), 0) + (base + r0 + 1)
    return jnp.minimum(pos, window).astype(F32)


def _pool_fwd(h, *, name):
    T, D = h.shape
    halo = _round_up(max(POOL_WINDOWS), 8)
    gd = D // len(POOL_WINDOWS)
    tt = _tile(T, 256, 16)
    assert tt >= halo
    rc = _tile(tt, 128, 16)

    def body(h_ref, o_ref, buf):
        t = pl.program_id(0)
        _stage_causal(buf, t, halo, tt)
        buf[pl.ds(halo, tt), :] = h_ref[...]
        for g, window in enumerate(POOL_WINDOWS):
            for cols_in_group in _col_blocks(gd):
                cols = pl.ds(g * gd + cols_in_group.start, cols_in_group.size)
                for r0 in range(0, tt, rc):
                    acc = None
                    for j in range(window):
                        term = buf[pl.ds(halo - j + r0, rc), cols]
                        acc = term if acc is None else acc + term
                    cnt = _pool_count(t * tt, r0, rc, cols.size, window)
                    o_ref[pl.ds(r0, rc), cols] = (acc / cnt - buf[pl.ds(halo + r0, rc), cols]).astype(BF16)

    blk = pl.BlockSpec((tt, D), lambda t: (t, 0))
    return pl.pallas_call(
        body, name=name, grid=(T // tt,), in_specs=[blk], out_specs=blk, out_shape=jax.ShapeDtypeStruct((T, D), BF16),
        scratch_shapes=[pltpu.VMEM((halo + tt, D), F32)], compiler_params=_params("arbitrary"),
    )(h)


def _pool_bwd(dp, *, name):
    T, D = dp.shape
    halo = _round_up(max(POOL_WINDOWS), 8)
    gd = D // len(POOL_WINDOWS)
    tt = _tile(T, 256, 16)
    assert tt >= halo
    rc = _tile(tt, 128, 16)
    nt_ = T // tt

    def body(dp_ref, o_ref, buf):
        t = pl.program_id(0)
        base = (nt_ - 1 - t) * tt
        _stage_anticausal(buf, t, halo, tt)
        for g, window in enumerate(POOL_WINDOWS):
            for cols_in_group in _col_blocks(gd):
                cols = pl.ds(g * gd + cols_in_group.start, cols_in_group.size)
                for r0 in range(0, tt, rc):
                    rows = pl.ds(r0, rc)
                    buf[rows, cols] = dp_ref[rows, cols] / _pool_count(base, r0, rc, cols.size, window)
                for r0 in range(0, tt, rc):
                    acc = None
                    for j in range(window):
                        term = buf[pl.ds(r0 + j, rc), cols]
                        acc = term if acc is None else acc + term
                    o_ref[pl.ds(r0, rc), cols] = acc - dp_ref[pl.ds(r0, rc), cols]

    blk = pl.BlockSpec((tt, D), lambda t: (nt_ - 1 - t, 0))
    return pl.pallas_call(
        body, name=name, grid=(nt_,), in_specs=[blk], out_specs=blk, out_shape=jax.ShapeDtypeStruct((T, D), F32),
        scratch_shapes=[pltpu.VMEM((tt + halo, D), F32)], compiler_params=_params("arbitrary"),
    )(dp)


def _sum_slabs(slabs, *, name):
    S, R, C = slabs.shape

    def body(s_ref, o_ref):
        acc = s_ref[0]
        for p in range(1, S):
            acc = acc + s_ref[p]
        o_ref[...] = acc

    return pl.pallas_call(body, name=name, out_shape=jax.ShapeDtypeStruct((R, C), F32),
                          compiler_params=_params())(slabs)


def _adamw(w, slabs, m, v, *, name):
    S, R, C = slabs.shape
    sub = 16 if slabs.dtype == BF16 else 8
    tr = _tile(R, max(sub, (1 << 18) // (C * S)), sub)
    c1, c2 = 1.0 - ADAM_B1 ** ADAM_STEP, 1.0 - ADAM_B2 ** ADAM_STEP

    def body(w_ref, s_ref, m_ref, v_ref, g_ref, d_ref, nm_ref, nv_ref):
        g = s_ref[0].astype(F32)
        for p in range(1, S):
            g = g + s_ref[p].astype(F32)
        nm = ADAM_B1 * m_ref[...] + (1.0 - ADAM_B1) * g
        nv = ADAM_B2 * v_ref[...] + (1.0 - ADAM_B2) * (g * g)
        g_ref[...] = g
        nm_ref[...] = nm
        nv_ref[...] = nv
        d_ref[...] = -ADAM_LR * ((nm / c1) / (jnp.sqrt(nv / c2) + ADAM_EPS) + ADAM_WD * w_ref[...])

    blk = pl.BlockSpec((tr, C), lambda i: (i, 0))
    sd = jax.ShapeDtypeStruct((R, C), F32)
    return pl.pallas_call(
        body, name=name, grid=(R // tr,), in_specs=[blk, pl.BlockSpec((S, tr, C), lambda i: (0, i, 0)), blk, blk],
        out_specs=[blk] * 4, out_shape=[sd] * 4, compiler_params=_params("parallel"),
    )(w, slabs, m, v)


def _cols_natural(g):
    g = jnp.moveaxis(g, 0, -2)
    return g.reshape(g.shape[:-2] + (g.shape[-2] * g.shape[-1],))


def _rows_natural(g):
    g = jnp.moveaxis(g, 0, 1)
    return g.reshape((g.shape[0], g.shape[1] * g.shape[2]) + g.shape[3:])


def _col_slabs(full, n_shards=N_DEV):
    n = full.shape[-1] // n_shards
    return jnp.moveaxis(full.reshape(full.shape[:-1] + (n_shards, n)), -2, 0)


def _pad_last(a, n):
    return jnp.pad(a, [(0, 0)] * (a.ndim - 1) + [(0, n - a.shape[-1])])


def kernel(x, c, mod_w, mod_b, norm_pre_mix, norm_post_mix, norm_pre_ffn, norm_post_ffn, a_pw1_w, a_pw1_b, a_dw_w, a_dw_b, a_ln_g, a_ln_b, a_pw2_w, a_pw2_b, b_group_w, b_group_b, b_scale, c_in_w, c_conv_w, c_out_w, f_up_w, f_dw_w, f_dw_b, f_down_w, loss_target, m_mod_w, m_mod_b, m_norm_pre_mix, m_norm_post_mix, m_norm_pre_ffn, m_norm_post_ffn, m_a_pw1_w, m_a_pw1_b, m_a_dw_w, m_a_dw_b, m_a_ln_g, m_a_ln_b, m_a_pw2_w, m_a_pw2_b, m_b_group_w, m_b_group_b, m_b_scale, m_c_in_w, m_c_conv_w, m_c_out_w, m_f_up_w, m_f_dw_w, m_f_dw_b, m_f_down_w, v_mod_w, v_mod_b, v_norm_pre_mix, v_norm_post_mix, v_norm_pre_ffn, v_norm_post_ffn, v_a_pw1_w, v_a_pw1_b, v_a_dw_w, v_a_dw_b, v_a_ln_g, v_a_ln_b, v_a_pw2_w, v_a_pw2_b, v_b_group_w, v_b_group_b, v_b_scale, v_c_in_w, v_c_conv_w, v_c_out_w, v_f_up_w, v_f_dw_w, v_f_dw_b, v_f_down_w):
    p = dict(locals())
    assert list(p) == INPUTS
    x0 = p['x'][0]
    target = p['loss_target'][0]
    T, D = x0.shape
    L = p['mod_w'].shape[0]
    G = len(POOL_WINDOWS)
    gd = D // G
    ns = p['f_up_w'].shape[-1]
    npad = _round_up(ns, LANE)
    Fp = 4 * npad
    me = 4 * lax.axis_index("x") + 2 * lax.axis_index("y") + lax.axis_index("c")
    row = lambda a, i: a[i:i + 1]

    big = ['a_pw1_w', 'a_pw2_w', 'b_group_w', 'c_in_w', 'c_out_w', 'f_down_w']
    small = ['a_pw1_b', 'a_dw_w', 'a_dw_b', 'a_ln_g', 'a_ln_b', 'a_pw2_b', 'c_conv_w']
    send = [p[k].astype(BF16) for k in big] + [_pad_last(p['f_up_w'], npad).astype(BF16)]
    send += [p[k] for k in small] + [_pad_last(p['f_dw_w'], npad), p['c']]
    got = _gather_two_level(send, name="gather_weights")
    gw = dict(zip(big + ['f_up_w'] + small + ['f_dw_w', 'c'], got))

    w1 = _cols_natural(gw['a_pw1_w'])
    b1 = _cols_natural(gw['a_pw1_b'])
    a_dw_w = _cols_natural(gw['a_dw_w'])
    a_dw_b, a_ln_g, a_ln_b, b2 = (_cols_natural(gw[k]) for k in ('a_dw_b', 'a_ln_g', 'a_ln_b', 'a_pw2_b'))
    w2 = _rows_natural(gw['a_pw2_w'])
    bgw = jnp.moveaxis(gw['b_group_w'], 0, 2).reshape((-1, G, gd, gd))
    cin = _cols_natural(gw['c_in_w'])
    cconv = _cols_natural(gw['c_conv_w'])
    cout = _rows_natural(gw['c_out_w'])
    wup_g, wup_v = _cols_natural(gw['f_up_w'][:4]), _cols_natural(gw['f_up_w'][4:])
    fdw_g, fdw_v = _cols_natural(gw['f_dw_w'][:4]), _cols_natural(gw['f_dw_w'][4:])
    pad_ff = lambda a: _pad_last(a.reshape(a.shape[:-1] + (4, ns)), npad).reshape(a.shape[:-1] + (Fp,))
    fdb_g, fdb_v = pad_ff(p['f_dw_b'][:, :4 * ns]), pad_ff(p['f_dw_b'][:, 4 * ns:])
    wdown = _rows_natural(gw['f_down_w'])
    wdown = jnp.pad(wdown.reshape(L, 4, ns, D), ((0, 0), (0, 0), (0, npad - ns), (0, 0))).reshape(L, Fp, D)

    c_all = gw['c'].reshape(N_DEV, D)
    n6 = p['mod_w'].shape[-1]
    mod_part = _mod_fwd(c_all, p['mod_w'], lax.dynamic_slice_in_dim(p['mod_b'], me * n6, n6, axis=1), name="mod_fwd")
    mod_all = _exchange([mod_part], scatter=False, name="gather_mod")[0]
    mod_all = jnp.moveaxis(mod_all, 0, 2).reshape(L, N_DEV, N_DEV * n6)
    mod = lax.dynamic_index_in_dim(mod_all, me, axis=1, keepdims=False).reshape(L, 6, D)

    saved = []
    x = x0
    h = _prenorm_fwd(x, row(p['norm_pre_mix'], 0), row(mod[0], 1), row(mod[0], 0), BF16, name="prenorm_first")
    dx = loss_cols = None
    for l in range(L):
        kind, slot = l % N_MIXERS, l // N_MIXERS
        sh_m, sc_m, gt_m, sh_f, sc_f, gt_f = (row(mod[l], i) for i in range(6))
        s = dict(x=x, h=h)
        if kind == 0:
            ua = _mm(h, w1[slot, :, :D], bias=b1[slot:slot + 1, :D], name=f"a_pw1_a_{l}")
            ug = _mm(h, w1[slot, :, D:], bias=b1[slot:slot + 1, D:], name=f"a_pw1_g_{l}")
            pa, z, q = _a_conv_fwd(ua, ug, a_dw_w[slot], a_dw_b[slot:slot + 1], a_ln_g[slot:slot + 1],
                                   a_ln_b[slot:slot + 1], name=f"a_conv_fwd_{l}")
            y = _mm(q, w2[slot], bias=b2[slot:slot + 1], name=f"a_pw2_{l}")
            s.update(ua=ua, ug=ug, p=pa, z=z, q=q)
        elif kind == 1:
            pooled = _pool_fwd(h, name=f"pool_fwd_{l}")
            mb, y = _group_mm(pooled, bgw[slot], bias=p['b_group_b'][slot:slot + 1],
                              scale=p['b_scale'][slot:slot + 1], name=f"b_mix_{l}")
            s.update(pooled=pooled, mb=mb)
        else:
            gb, gc, v = (_mm(h, cin[slot, :, i * D:(i + 1) * D], name=f"c_in_{i}_{l}") for i in range(3))
            pc, cv, sg = _c_gate_fwd(gb, gc, v, cconv[slot], name=f"c_gate_fwd_{l}")
            y = _mm(sg, cout[slot], name=f"c_out_{l}")
            s.update(gb=gb, gc=gc, v=v, p=pc, cv=cv, sg=sg)
        x1, h2 = _post_pre_fwd(x, y, row(p['norm_post_mix'], l), gt_m, row(p['norm_pre_ffn'], l), sc_f, sh_f, BF16,
                               name=f"post_mix_{l}")
        fug = _mm(h2, wup_g[l], name=f"f_up_g_{l}")
        fuv = _mm(h2, wup_v[l], name=f"f_up_v_{l}")
        act = _ffn_act_fwd(fug, fuv, fdw_g[l], fdw_v[l], fdb_g[l:l + 1], fdb_v[l:l + 1], name=f"ffn_act_fwd_{l}")
        y2 = _mm(act, wdown[l], name=f"f_down_{l}")
        s.update(y=y, x1=x1, h2=h2, fug=fug, fuv=fuv, act=act, y2=y2)
        saved.append(s)
        if l + 1 < L:
            nxt_dtype = F32 if (l + 1) % N_MIXERS == 1 else BF16
            x, h = _post_pre_fwd(x1, y2, row(p['norm_post_ffn'], l), gt_f, row(p['norm_pre_mix'], l + 1),
                                 row(mod[l + 1], 1), row(mod[l + 1], 0), nxt_dtype, name=f"post_ffn_{l}")
        else:
            dx, loss_cols = _post_loss(x1, y2, row(p['norm_post_ffn'], l), gt_f, target, name="post_loss")
    loss = lax.psum(0.5 * jnp.sum(loss_cols) / D, AXES)

    gfull = {}
    for k in ('a_pw1_w', 'a_pw1_b', 'a_dw_w', 'a_dw_b', 'a_ln_g', 'a_ln_b', 'a_pw2_w', 'a_pw2_b', 'b_group_w',
              'c_in_w', 'c_conv_w', 'c_out_w', 'f_up_w', 'f_dw_w', 'f_down_w'):
        gfull[k] = {}
    dmod, dnorm, d_fdb = {}, {}, {}
    d_bgb = d_bscale = None
    for l in reversed(range(L)):
        kind, slot = l % N_MIXERS, l // N_MIXERS
        s = saved[l]
        sh_m, sc_m, gt_m, sh_f, sc_f, gt_f = (row(mod[l], i) for i in range(6))
        dy2, dg4, dgt_f, _ = _post_bwd(s['y2'], dx, row(p['norm_post_ffn'], l), gt_f, BF16, name=f"post_ffn_bwd_{l}")
        da = _mm(dy2, wdown[l], nt=True, name=f"f_down_dx_{l}")
        dwd = _mm(_transpose(dy2, name=f"dy2_t_{l}"), s['act'], out_dtype=BF16, name=f"f_down_dw_{l}").T
        gfull['f_down_w'][l] = dwd.reshape(4, npad, D)[:, :ns].reshape(N_DEV, ns // 2, D)
        dcg, dcv, dbg, dbv = _ffn_act_bwd(s['fug'], s['fuv'], fdw_g[l], fdw_v[l], fdb_g[l:l + 1], fdb_v[l:l + 1], da,
                                          name=f"ffn_act_bwd_{l}")
        dug, dwg = _dwconv_bwd(s['fug'], dcg, fdw_g[l], BF16, name=f"f_dw_bwd_g_{l}")
        duv, dwv = _dwconv_bwd(s['fuv'], dcv, fdw_v[l], BF16, name=f"f_dw_bwd_v_{l}")
        dh2 = _mm(dug, wup_g[l], nt=True, name=f"f_up_dx_g_{l}")
        dh2 = _mm(duv, wup_v[l], nt=True, add=dh2, name=f"f_up_dx_v_{l}")
        h2t = _transpose(s['h2'], name=f"h2_t_{l}")
        gfull['f_up_w'][l] = jnp.concatenate(
            [_mm(h2t, dug, shard_cols=npad, out_dtype=BF16, name=f"f_up_dw_g_{l}"),
             _mm(h2t, duv, shard_cols=npad, out_dtype=BF16, name=f"f_up_dw_v_{l}")], axis=0)
        gfull['f_dw_w'][l] = jnp.concatenate([_col_slabs(dwg, 4), _col_slabs(dwv, 4)], axis=0)
        unpad = lambda a: a.reshape(4, npad)[:, :ns].reshape(4 * ns)
        d_fdb[l] = jnp.concatenate([unpad(dbg), unpad(dbv)])
        dx1, dsh_f, dsc_f, dg3 = _pre_bwd(s['x1'], dh2, dx, row(p['norm_pre_ffn'], l), sc_f, name=f"pre_ffn_bwd_{l}")
        dy_dtype = F32 if kind == 1 else BF16
        dy, dg2, dgt_m, dy_sum = _post_bwd(s['y'], dx1, row(p['norm_post_mix'], l), gt_m, dy_dtype,
                                           name=f"post_mix_bwd_{l}")
        if kind == 0:
            dq = _mm(dy, w2[slot], nt=True, name=f"a_pw2_dx_{l}")
            gfull['a_pw2_w'][slot] = _mm(_transpose(s['q'], name=f"q_t_{l}"), dy, out_dtype=BF16,
                                         name=f"a_pw2_dw_{l}").reshape(N_DEV, D // N_DEV, D)
            gfull['a_pw2_b'][slot] = dy_sum.reshape(N_DEV, D // N_DEV)
            dz, dlg, dlb, dzs = _a_ln_bwd(s['z'], dq, a_ln_g[slot:slot + 1], a_ln_b[slot:slot + 1], name=f"a_ln_bwd_{l}")
            gfull['a_ln_g'][slot], gfull['a_ln_b'][slot], gfull['a_dw_b'][slot] = (
                t.reshape(N_DEV, D // N_DEV) for t in (dlg, dlb, dzs))
            dp, ddw = _dwconv_bwd(s['p'], dz, a_dw_w[slot], F32, name=f"a_dw_bwd_{l}")
            gfull['a_dw_w'][slot] = _col_slabs(ddw)
            dua, dug_a, sua, sug = _a_glu_bwd(s['ua'], s['ug'], dp, name=f"a_glu_bwd_{l}")
            gfull['a_pw1_b'][slot] = jnp.concatenate([sua, sug], axis=1).reshape(N_DEV, 2 * D // N_DEV)
            dh = _mm(dua, w1[slot, :, :D], nt=True, name=f"a_pw1_dx_a_{l}")
            dh = _mm(dug_a, w1[slot, :, D:], nt=True, add=dh, name=f"a_pw1_dx_g_{l}")
            ht = _transpose(s['h'], name=f"h_t_{l}")
            n1 = 2 * D // N_DEV
            gfull['a_pw1_w'][slot] = jnp.concatenate(
                [_mm(ht, dua, shard_cols=n1, out_dtype=BF16, name=f"a_pw1_dw_a_{l}"),
                 _mm(ht, dug_a, shard_cols=n1, out_dtype=BF16, name=f"a_pw1_dw_g_{l}")], axis=0)
        elif kind == 1:
            dm, d_bscale, d_bgb = _b_scale_bwd(dy, s['mb'], p['b_scale'][slot:slot + 1], name=f"b_scale_bwd_{l}")
            dpool = _group_mm(dm, bgw[slot], nt=True, name=f"b_mix_dx_{l}")
            dgw = _group_dw(_transpose(s['pooled'], name=f"pooled_t_{l}"), dm, G, name=f"b_mix_dw_{l}")
            gfull['b_group_w'][slot] = jnp.moveaxis(dgw.astype(BF16).reshape(G, N_DEV, gd // N_DEV, gd), 1, 0)
            dh = _pool_bwd(dpool, name=f"pool_bwd_{l}")
        else:
            dsg = _mm(dy, cout[slot], nt=True, name=f"c_out_dx_{l}")
            gfull['c_out_w'][slot] = _mm(_transpose(s['sg'], name=f"sg_t_{l}"), dy, out_dtype=BF16,
                                         name=f"c_out_dw_{l}").reshape(N_DEV, D // N_DEV, D)
            dgb, dcv_c = _c_gate_bwd1(s['gb'], s['cv'], dsg, name=f"c_gate_bwd1_{l}")
            dpc, dcw = _dwconv_bwd(s['p'], dcv_c, cconv[slot], F32, name=f"c_conv_bwd_{l}")
            gfull['c_conv_w'][slot] = _col_slabs(dcw)
            dgc, dv = _c_gate_bwd2(s['gc'], s['v'], dpc, name=f"c_gate_bwd2_{l}")
            ht = _transpose(s['h'], name=f"h_t_{l}")
            dh, parts = None, []
            for i, d_i in enumerate((dgb, dgc, dv)):
                dh = _mm(d_i, cin[slot, :, i * D:(i + 1) * D], nt=True, add=dh, name=f"c_in_dx_{i}_{l}")
                parts.append(_mm(ht, d_i, out_dtype=BF16, name=f"c_in_dw_{i}_{l}"))
            gfull['c_in_w'][slot] = _col_slabs(jnp.concatenate(parts, axis=1))
        dx, dsh_m, dsc_m, dg1 = _pre_bwd(s['x'], dh, dx1, row(p['norm_pre_mix'], l), sc_m, name=f"pre_mix_bwd_{l}")
        dmod[l] = jnp.concatenate([dsh_m, dsc_m, dgt_m, dsh_f, dsc_f, dgt_f], axis=0)
        dnorm[l] = (dg1, dg2, dg3, dg4)

    zero_row = jnp.zeros((1, D), F32)
    rows_ = [dmod[l] for l in range(L)]
    rows_ += [dnorm[l][i] for i in range(4) for l in range(L)]
    rows_ += [d_bgb if d_bgb is not None else zero_row, d_bscale if d_bscale is not None else zero_row]
    fdb_flat = jnp.concatenate([d_fdb[l] for l in range(L)])
    rows_ += [_pad_last(fdb_flat, _round_up(fdb_flat.size, D)).reshape(-1, D)]
    pack = jnp.concatenate(rows_, axis=0)
    pack = jnp.pad(pack, ((0, _round_up(pack.shape[0], 8) - pack.shape[0]), (0, 0)))
    pack_all = _exchange([pack], scatter=False, name="gather_small_grads")[0]
    red = _sum_slabs(pack_all, name="sum_small_grads")
    g_rep = {'mod_b': red[:6 * L].reshape(L, 6 * D)}
    for i, k in enumerate(('norm_pre_mix', 'norm_post_mix', 'norm_pre_ffn', 'norm_post_ffn')):
        g_rep[k] = red[6 * L + i * L:6 * L + (i + 1) * L]
    o = 6 * L + 4 * L
    g_rep['b_group_b'], g_rep['b_scale'] = red[o:o + 1], red[o + 1:o + 2]
    g_rep['f_dw_b'] = red[o + 2:].reshape(-1)[:L * 8 * ns].reshape(L, 8 * ns)

    dmod_all = pack_all[:, :6 * L].reshape(N_DEV, L, 6 * D)
    dmod_mine = jnp.moveaxis(lax.dynamic_slice_in_dim(dmod_all, me * n6, n6, axis=2), 0, 1)
    g_mod_w = _mod_bwd(c_all.T, dmod_mine, name="mod_bwd")

    stack = lambda d: jnp.stack([d[i] for i in sorted(d)], axis=1)
    big_g = [k for k in gfull if gfull[k] and k.endswith('_w') and k not in ('a_dw_w', 'c_conv_w', 'f_dw_w')]
    small_g = [k for k in gfull if gfull[k] and k not in big_g]
    full = [stack(gfull[k]) for k in big_g]
    got = _swap_cores(full, name="scatter_grads_cores")
    core = lax.axis_index("c").astype(jnp.int32).reshape(1)
    partial = [_add_core_slabs(a, b, core, name=f"add_core_slabs_{k}") for k, a, b in zip(big_g, full, got)]
    slabs = dict(zip(big_g, _swap_chips(partial, name="scatter_grads_chips")))
    slabs.update(zip(small_g, _exchange([stack(gfull[k]) for k in small_g], scatter=True, name="scatter_small_grads")))

    out = {}
    for k in WEIGHTS:
        w, m, v = p[k], p['m_' + k], p['v_' + k]
        if k in g_rep:
            g = g_rep[k].reshape((1,) + w.shape)
        elif k == 'mod_w':
            g = g_mod_w.reshape((1,) + w.shape)
        else:
            g = slabs[k]
        if k in ('f_up_w', 'f_dw_w'):
            w, m, v = (_pad_last(t, npad) for t in (w, m, v))
        shape = w.shape
        two_d = (-1, shape[-1])
        res = _adamw(w.reshape(two_d), g.reshape((g.shape[0],) + (w.size // shape[-1], shape[-1])), m.reshape(two_d),
                     v.reshape(two_d), name=f"adamw_{k}")
        res = [t.reshape(shape) for t in res]
        if k in ('f_up_w', 'f_dw_w'):
            res = [t[..., :ns] for t in res]
        out[k] = res

    grads, deltas, new_m, new_v = ([out[k][i] for k in WEIGHTS] for i in range(4))
    return (loss, dx.reshape(1, T, D), *grads, *deltas, *new_m, *new_v)
```

```python
import functools

import jax
import jax.numpy as jnp
from jax import lax
from jax.experimental import pallas as pl
from jax.experimental.pallas import tpu as pltpu

F32, BF16 = jnp.float32, jnp.bfloat16
AXES = ("x", "y", "c")
N_DEV = 8
LANE = 128
VMEM_LIMIT = 48 * 1024 * 1024
RMS_EPS, LN_EPS = 1e-6, 1e-5
POOL_WINDOWS = (2, 4, 8, 16)
N_MIXERS = 3
ADAM_LR, ADAM_B1, ADAM_B2, ADAM_EPS, ADAM_WD, ADAM_STEP = 0.001, 0.9, 0.999, 1e-08, 0.01, 10

WEIGHTS = ['mod_w', 'mod_b', 'norm_pre_mix', 'norm_post_mix', 'norm_pre_ffn', 'norm_post_ffn', 'a_pw1_w', 'a_pw1_b',
           'a_dw_w', 'a_dw_b', 'a_ln_g', 'a_ln_b', 'a_pw2_w', 'a_pw2_b', 'b_group_w', 'b_group_b', 'b_scale', 'c_in_w',
           'c_conv_w', 'c_out_w', 'f_up_w', 'f_dw_w', 'f_dw_b', 'f_down_w']
INPUTS = (['x', 'c'] + WEIGHTS + ['loss_target'] + ['m_' + w for w in WEIGHTS] + ['v_' + w for w in WEIGHTS])


def _tile(n, cap, mult):
    for t in range(min(n, cap), 0, -1):
        if n % t == 0 and t % mult == 0:
            return t
    return n


def _round_up(n, m):
    return -(-n // m) * m


def _params(*dims):
    return pltpu.CompilerParams(dimension_semantics=dims or None, vmem_limit_bytes=VMEM_LIMIT)


def _sigmoid(v):
    return jax.nn.sigmoid(v)


def _exchange(arrs, *, scatter, name):
    n = len(arrs)

    def body(*refs):
        ins, outs = refs[:n], refs[n:2 * n]
        send_sems, recv_sems, local_sems = refs[2 * n:]
        x, y, c = (lax.axis_index(a) for a in AXES)
        me = 4 * x + 2 * y + c

        def peer(k):
            px = 1 - x if (k >> 2) & 1 else x
            py = 1 - y if (k >> 1) & 1 else y
            pc = 1 - c if k & 1 else c
            return (px, py, pc), 4 * px + 2 * py + pc

        def remote(i, k, src, dst):
            return pltpu.make_async_remote_copy(
                src_ref=src, dst_ref=dst, send_sem=send_sems.at[i, k - 1], recv_sem=recv_sems.at[i, k - 1],
                device_id=peer(k)[0], device_id_type=pl.DeviceIdType.MESH)

        own = []
        for i in range(n):
            cp = pltpu.make_async_copy(ins[i].at[me] if scatter else ins[i], outs[i].at[me], local_sems.at[i])
            cp.start()
            own.append(cp)
        sent = []
        for k in range(1, N_DEV):
            for i in range(n):
                cp = remote(i, k, ins[i].at[peer(k)[1]] if scatter else ins[i], outs[i].at[me])
                cp.start()
                sent.append(cp)
        for k in range(1, N_DEV):
            for i in range(n):
                landing = outs[i].at[peer(k)[1]]
                remote(i, k, landing, landing).wait_recv()
        for cp in sent:
            cp.wait_send()
        for cp in own:
            cp.wait()

    any_spec = pl.BlockSpec(memory_space=pl.ANY)
    out_shape = [jax.ShapeDtypeStruct(a.shape if scatter else (N_DEV,) + a.shape, a.dtype) for a in arrs]
    return pl.pallas_call(
        body, name=name, out_shape=out_shape, in_specs=[any_spec] * n, out_specs=[any_spec] * n,
        scratch_shapes=[pltpu.SemaphoreType.DMA((n, N_DEV - 1)), pltpu.SemaphoreType.DMA((n, N_DEV - 1)),
                        pltpu.SemaphoreType.DMA((n,))],
    )(*arrs)


def _comm_call(body, arrs, out_shape, sem_shapes, name):
    any_spec = pl.BlockSpec(memory_space=pl.ANY)
    return pl.pallas_call(
        body, name=name, out_shape=out_shape, in_specs=[any_spec] * len(arrs), out_specs=[any_spec] * len(out_shape),
        scratch_shapes=[pltpu.SemaphoreType.DMA(s) for s in sem_shapes],
    )(*arrs)


def _gather_two_level(arrs, *, name):
    n = len(arrs)

    def body(*refs):
        ins, outs = refs[:n], refs[n:2 * n]
        send_sems, recv_sems, local_sems = refs[2 * n:]
        x, y, c = (lax.axis_index(a) for a in AXES)
        index = lambda px, py, pc: 4 * px + 2 * py + pc
        me, sibling = index(x, y, c), (x, y, 1 - c)
        chips = [(1 - x, y), (x, 1 - y), (1 - x, 1 - y)]

        def copy(i, k, block, to, src=None):
            return pltpu.make_async_remote_copy(
                src_ref=outs[i].at[block] if src is None else src, dst_ref=outs[i].at[block],
                send_sem=send_sems.at[i, k], recv_sem=recv_sems.at[i, k], device_id=to,
                device_id_type=pl.DeviceIdType.MESH)

        own = [pltpu.make_async_copy(ins[i], outs[i].at[me], local_sems.at[i]) for i in range(n)]
        first = [copy(i, 1 + j, me, (*chip, c), src=ins[i]) for j, chip in enumerate(chips) for i in range(n)]
        first += [copy(i, 0, me, sibling, src=ins[i]) for i in range(n)]
        for cp in own + first:
            cp.start()
        passed = []
        for j, chip in enumerate(chips):
            block = index(*chip, c)
            for i in range(n):
                copy(i, 1 + j, block, (x, y, c)).wait_recv()
            for i in range(n):
                cp = copy(i, 4 + j, block, sibling)
                cp.start()
                passed.append(cp)
        for i in range(n):
            copy(i, 0, index(x, y, 1 - c), (x, y, c)).wait_recv()
        for j, chip in enumerate(chips):
            for i in range(n):
                copy(i, 4 + j, index(*chip, 1 - c), (x, y, c)).wait_recv()
        for cp in first + passed:
            cp.wait_send()
        for cp in own:
            cp.wait()

    out_shape = [jax.ShapeDtypeStruct((N_DEV,) + a.shape, a.dtype) for a in arrs]
    return _comm_call(body, arrs, out_shape, [(n, 7), (n, 7), (n,)], name)


def _swap_cores(arrs, *, name):
    n = len(arrs)

    def body(*refs):
        ins, got = refs[:n], refs[n:2 * n]
        send_sems, recv_sems = refs[2 * n:]
        x, y, c = (lax.axis_index(a) for a in AXES)
        remote = [pltpu.make_async_remote_copy(
            src_ref=ins[i].at[2 * q + 1 - c], dst_ref=got[i].at[q], send_sem=send_sems.at[i, q],
            recv_sem=recv_sems.at[i, q], device_id=(x, y, 1 - c), device_id_type=pl.DeviceIdType.MESH)
            for q in range(4) for i in range(n)]
        for cp in remote:
            cp.start()
        for cp in remote:
            cp.wait_recv()
        for cp in remote:
            cp.wait_send()

    half = [jax.ShapeDtypeStruct((4,) + a.shape[1:], a.dtype) for a in arrs]
    return _comm_call(body, arrs, half, [(n, 4), (n, 4)], name)


def _swap_chips(arrs, *, name):
    n = len(arrs)

    def body(*refs):
        ins, outs = refs[:n], refs[n:2 * n]
        send_sems, recv_sems, local_sems = refs[2 * n:]
        x, y, c = (lax.axis_index(a) for a in AXES)
        mine = 2 * x + y

        def peer(k):
            px = 1 - x if k & 2 else x
            py = 1 - y if k & 1 else y
            return (px, py, c), 2 * px + py

        def remote(i, k, src, dst):
            return pltpu.make_async_remote_copy(
                src_ref=src, dst_ref=dst, send_sem=send_sems.at[i, k - 1], recv_sem=recv_sems.at[i, k - 1],
                device_id=peer(k)[0], device_id_type=pl.DeviceIdType.MESH)

        own = [pltpu.make_async_copy(ins[i].at[mine], outs[i].at[mine], local_sems.at[i]) for i in range(n)]
        sent = [remote(i, k, ins[i].at[peer(k)[1]], outs[i].at[mine]) for k in range(1, 4) for i in range(n)]
        for cp in own + sent:
            cp.start()
        for k in range(1, 4):
            for i in range(n):
                landing = outs[i].at[peer(k)[1]]
                remote(i, k, landing, landing).wait_recv()
        for cp in sent:
            cp.wait_send()
        for cp in own:
            cp.wait()

    out_shape = [jax.ShapeDtypeStruct(a.shape, a.dtype) for a in arrs]
    return _comm_call(body, arrs, out_shape, [(n, 3), (n, 3), (n,)], name)


def _add_core_slabs(full, got, core, *, name):
    C = got.shape[-1]
    R = got[0].size // C
    tr = _tile(R, max(16, (1 << 19) // C), 16)

    def body(core_ref, a_ref, b_ref, o_ref):
        o_ref[...] = (a_ref[...].astype(F32) + b_ref[...].astype(F32)).astype(o_ref.dtype)

    blk = pl.BlockSpec((None, tr, C), lambda q, i, core_ref: (q, i, 0))
    grid_spec = pltpu.PrefetchScalarGridSpec(
        num_scalar_prefetch=1, grid=(4, R // tr),
        in_specs=[pl.BlockSpec((None, None, tr, C), lambda q, i, core_ref: (q, core_ref[0], i, 0)), blk], out_specs=blk)
    return pl.pallas_call(
        body, name=name, grid_spec=grid_spec, out_shape=jax.ShapeDtypeStruct((4, R, C), got.dtype),
        compiler_params=_params("parallel", "parallel"),
    )(core, full.reshape(4, 2, R, C), got.reshape(4, R, C)).reshape(got.shape)


def _mm(a, b, *, name, nt=False, bias=None, add=None, out_dtype=F32, shard_cols=0):
    M, K = a.shape
    N = b.shape[0] if nt else b.shape[1]
    tm, tn = _tile(M, 1024, 16), shard_cols or _tile(N, 1024, LANE)

    def vmem_bytes(tk):
        blocks = 2 * (tm * tk + tk * tn) + tm * tn * jnp.dtype(out_dtype).itemsize + (4 * tm * tn if add is not None else 0)
        return 2 * blocks + (4 * tm * tn if tk < K else 0)

    tk = next((t for t in (_tile(K, 3072, LANE), _tile(K, 2048, LANE)) if vmem_bytes(t) <= VMEM_LIMIT - (12 << 20)),
              _tile(K, 1024, LANE))
    nk = K // tk
    dims = (((1,), (1,)), ((), ())) if nt else (((1,), (0,)), ((), ()))

    def body(*refs):
        refs = list(refs)
        a_ref, b_ref = refs.pop(0), refs.pop(0)
        bias_ref = refs.pop(0) if bias is not None else None
        add_ref = refs.pop(0) if add is not None else None
        o_ref = refs.pop(0)
        part = lax.dot_general(a_ref[...], b_ref[...], dims, preferred_element_type=F32)

        def finish(val):
            if bias_ref is not None:
                val = val + bias_ref[...]
            if add_ref is not None:
                val = val + add_ref[...]
            o_ref[...] = val.astype(o_ref.dtype)

        if nk == 1:
            finish(part)
        else:
            acc_ref = refs.pop(0)
            k = pl.program_id(2)

            @pl.when(k == 0)
            def _():
                acc_ref[...] = part

            @pl.when(k > 0)
            def _():
                acc_ref[...] += part

            @pl.when(k == nk - 1)
            def _():
                finish(acc_ref[...])

    in_specs = [pl.BlockSpec((tm, tk), lambda i, j, k: (i, k)),
                pl.BlockSpec((tn, tk), lambda i, j, k: (j, k)) if nt else pl.BlockSpec((tk, tn), lambda i, j, k: (k, j))]
    operands = [a, b]
    if bias is not None:
        in_specs.append(pl.BlockSpec((1, tn), lambda i, j, k: (0, j)))
        operands.append(bias)
    if add is not None:
        in_specs.append(pl.BlockSpec((tm, tn), lambda i, j, k: (i, j)))
        operands.append(add)
    if shard_cols:
        out_shape = jax.ShapeDtypeStruct((N // tn, M, tn), out_dtype)
        out_spec = pl.BlockSpec((None, tm, tn), lambda i, j, k: (j, i, 0))
    else:
        out_shape = jax.ShapeDtypeStruct((M, N), out_dtype)
        out_spec = pl.BlockSpec((tm, tn), lambda i, j, k: (i, j))
    return pl.pallas_call(
        body, name=name, grid=(M // tm, N // tn, nk), in_specs=in_specs, out_specs=out_spec, out_shape=out_shape,
        scratch_shapes=[pltpu.VMEM((tm, tn), F32)] if nk > 1 else [],
        compiler_params=_params("parallel", "parallel", "arbitrary"),
    )(*operands)


def _transpose(x, *, name):
    T, C = x.shape
    tt, tc = _tile(T, 512, LANE), _tile(C, 512, LANE)

    def body(x_ref, o_ref):
        o_ref[...] = x_ref[...].astype(F32).T.astype(BF16)

    return pl.pallas_call(
        body, name=name, grid=(T // tt, C // tc), in_specs=[pl.BlockSpec((tt, tc), lambda i, j: (i, j))],
        out_specs=pl.BlockSpec((tc, tt), lambda i, j: (j, i)), out_shape=jax.ShapeDtypeStruct((C, T), BF16),
        compiler_params=_params("parallel", "parallel"),
    )(x)


def _group_mm(a, w, *, name, nt=False, bias=None, scale=None):
    T, D = a.shape
    G, gd, _ = w.shape
    tm = _tile(T, 1024, 16)
    dims = (((1,), (1,)), ((), ())) if nt else (((1,), (0,)), ((), ()))
    fused = bias is not None

    def body(*refs):
        if fused:
            a_ref, w_ref, b_ref, s_ref, mb_ref, y_ref = refs
        else:
            a_ref, w_ref, y_ref = refs
        val = lax.dot_general(a_ref[...], w_ref[...], dims, preferred_element_type=F32)
        if fused:
            val = val + b_ref[...]
            mb_ref[...] = val
            val = val * s_ref[...]
        y_ref[...] = val

    blk = pl.BlockSpec((tm, gd), lambda i, g: (i, g))
    vec = pl.BlockSpec((1, gd), lambda i, g: (0, g))
    in_specs = [blk, pl.BlockSpec((None, gd, gd), lambda i, g: (g, 0, 0))] + ([vec, vec] if fused else [])
    td = jax.ShapeDtypeStruct((T, D), F32)
    return pl.pallas_call(
        body, name=name, grid=(T // tm, G), in_specs=in_specs, out_specs=[blk, blk] if fused else blk,
        out_shape=[td, td] if fused else td, compiler_params=_params("parallel", "parallel"),
    )(*([a, w, bias, scale] if fused else [a, w]))


def _group_dw(at, b, groups, *, name):
    D, T = at.shape
    gd = D // groups
    tk = _tile(T, 2048, LANE)

    def body(a_ref, b_ref, o_ref):
        @pl.when(pl.program_id(1) == 0)
        def _():
            o_ref[...] = jnp.zeros_like(o_ref)

        o_ref[...] += jnp.dot(a_ref[...], b_ref[...], preferred_element_type=F32)

    return pl.pallas_call(
        body, name=name, grid=(groups, T // tk),
        in_specs=[pl.BlockSpec((gd, tk), lambda g, k: (g, k)), pl.BlockSpec((tk, gd), lambda g, k: (k, g))],
        out_specs=pl.BlockSpec((None, gd, gd), lambda g, k: (g, 0, 0)),
        out_shape=jax.ShapeDtypeStruct((groups, gd, gd), F32), compiler_params=_params("parallel", "arbitrary"),
    )(at, b)


def _mod_fwd(c_all, w, b, *, name):
    L, D, n = w.shape
    B = c_all.shape[0]

    def body(c_ref, w_ref, b_ref, o_ref):
        cv = c_ref[...]
        o_ref[...] = jnp.dot(cv * _sigmoid(cv), w_ref[...], preferred_element_type=F32) + b_ref[...]

    return pl.pallas_call(
        body, name=name, grid=(L,),
        in_specs=[pl.BlockSpec((B, D), lambda l: (0, 0)), pl.BlockSpec((None, D, n), lambda l: (l, 0, 0)),
                  pl.BlockSpec((None, 1, n), lambda l: (l, 0, 0))],
        out_specs=pl.BlockSpec((None, B, n), lambda l: (l, 0, 0)), out_shape=jax.ShapeDtypeStruct((L, B, n), F32),
        compiler_params=_params("parallel"),
    )(c_all, w, b.reshape(L, 1, n))


def _mod_bwd(c_all_t, dmod, *, name):
    L, B, n = dmod.shape
    D = c_all_t.shape[0]

    def body(c_ref, d_ref, o_ref):
        cv = c_ref[...]
        o_ref[...] = jnp.dot(cv * _sigmoid(cv), d_ref[...], preferred_element_type=F32)

    return pl.pallas_call(
        body, name=name, grid=(L,),
        in_specs=[pl.BlockSpec((D, B), lambda l: (0, 0)), pl.BlockSpec((None, B, n), lambda l: (l, 0, 0))],
        out_specs=pl.BlockSpec((None, D, n), lambda l: (l, 0, 0)), out_shape=jax.ShapeDtypeStruct((L, D, n), F32),
        compiler_params=_params("parallel"),
    )(c_all_t, dmod)


def _rowwise(fn, tiles, vecs, out_dtypes, n_sums, *, name):
    T, C = tiles[0].shape
    tr = _tile(T, 512, 16)
    ch = _tile(tr, 32, 16)
    nt_, nv, no = len(tiles), len(vecs), len(out_dtypes)

    def body(*refs):
        t_refs, v_refs = refs[:nt_], refs[nt_:nt_ + nv]
        o_refs, s_refs = refs[nt_ + nv:nt_ + nv + no], refs[nt_ + nv + no:]
        vv = [v[...] for v in v_refs]
        acc = [None] * n_sums
        for r0 in range(0, tr, ch):
            rows = pl.ds(r0, ch)
            o_vals, s_vals = fn([t[rows, :] for t in t_refs], vv)
            for o, val in zip(o_refs, o_vals):
                o[rows, :] = val.astype(o.dtype)
            for i, val in enumerate(s_vals):
                part = jnp.sum(val, axis=0, keepdims=True)
                acc[i] = part if acc[i] is None else acc[i] + part
        if n_sums:
            first = pl.program_id(0) == 0

            @pl.when(first)
            def _():
                for s, val in zip(s_refs, acc):
                    s[...] = val

            @pl.when(jnp.logical_not(first))
            def _():
                for s, val in zip(s_refs, acc):
                    s[...] += val

    tile_spec = pl.BlockSpec((tr, C), lambda i: (i, 0))
    vec_spec = pl.BlockSpec((1, C), lambda i: (0, 0))
    res = pl.pallas_call(
        body, name=name, grid=(T // tr,), in_specs=[tile_spec] * nt_ + [vec_spec] * nv,
        out_specs=[tile_spec] * no + [vec_spec] * n_sums,
        out_shape=[jax.ShapeDtypeStruct((T, C), d) for d in out_dtypes] + [jax.ShapeDtypeStruct((1, C), F32)] * n_sums,
        compiler_params=_params("arbitrary"),
    )(*tiles, *vecs)
    return res


def _rms(v):
    return lax.rsqrt(jnp.mean(v * v, axis=-1, keepdims=True) + RMS_EPS)


def _prenorm_fwd(x, g, sc, sh, out_dtype, *, name):
    def fn(t, v):
        (xv,), (g_, sc_, sh_) = t, v
        return [(xv * _rms(xv)) * g_ * (1.0 + sc_) + sh_], []

    return _rowwise(fn, [x], [g, sc, sh], [out_dtype], 0, name=name)[0]


def _post_pre_fwd(x, y, gp, gate, g, sc, sh, out_dtype, *, name):
    def fn(t, v):
        (xv, yv), (gp_, gate_, g_, sc_, sh_) = t, v
        x1 = xv + gate_ * ((yv * _rms(yv)) * gp_)
        return [x1, (x1 * _rms(x1)) * g_ * (1.0 + sc_) + sh_], []

    return _rowwise(fn, [x, y], [gp, gate, g, sc, sh], [F32, out_dtype], 0, name=name)


def _post_loss(x, y, gp, gate, target, *, name):
    D = x.shape[1]

    def fn(t, v):
        (xv, yv, tv), (gp_, gate_) = t, v
        err = xv + gate_ * ((yv * _rms(yv)) * gp_) - tv
        return [err / D], [err * err]

    return _rowwise(fn, [x, y, target], [gp, gate], [F32], 1, name=name)


def _post_bwd(y, dxo, gp, gate, out_dtype, *, name):
    def fn(t, v):
        (yv, dv), (gp_, gate_) = t, v
        r = _rms(yv)
        yn = yv * r
        dyn = dv * gate_ * gp_
        dy = r * (dyn - yn * jnp.mean(dyn * yn, axis=-1, keepdims=True))
        return [dy], [dv * gate_ * yn, dv * yn * gp_, dy]

    return _rowwise(fn, [y, dxo], [gp, gate], [out_dtype], 3, name=name)


def _pre_bwd(x, dh, dres, g, sc, *, name):
    def fn(t, v):
        (xv, dhv, dr), (g_, sc_) = t, v
        dhv = dhv.astype(F32)
        r = _rms(xv)
        xn = xv * r
        dxn = dhv * g_ * (1.0 + sc_)
        dx = dr + r * (dxn - xn * jnp.mean(dxn * xn, axis=-1, keepdims=True))
        return [dx], [dhv, dhv * xn * g_, dhv * xn * (1.0 + sc_)]

    return _rowwise(fn, [x, dh, dres], [g, sc], [F32], 3, name=name)


def _a_ln_bwd(z, dq, g, b, *, name):
    def fn(t, v):
        (zv, dqv), (g_, b_) = t, v
        mu = jnp.mean(zv, axis=-1, keepdims=True)
        zc = zv - mu
        rstd = lax.rsqrt(jnp.mean(zc * zc, axis=-1, keepdims=True) + LN_EPS)
        zn = zc * rstd
        l = zn * g_ + b_
        s = _sigmoid(l)
        dl = dqv * (s * (1.0 + l * (1.0 - s)))
        dzn = dl * g_
        dz = rstd * (dzn - jnp.mean(dzn, axis=-1, keepdims=True) - zn * jnp.mean(dzn * zn, axis=-1, keepdims=True))
        return [dz], [dl * zn, dl, dz]

    return _rowwise(fn, [z, dq], [g, b], [F32], 3, name=name)


def _a_glu_bwd(ua, ug, dp, *, name):
    def fn(t, v):
        uav, ugv, dpv = t
        s = _sigmoid(ugv)
        dua = dpv * s
        dug = dpv * uav * s * (1.0 - s)
        return [dua, dug], [dua, dug]

    return _rowwise(fn, [ua, ug, dp], [], [BF16, BF16], 2, name=name)


def _b_scale_bwd(dy, mb, scale, *, name):
    def fn(t, v):
        (dyv, mbv), (s_,) = t, v
        dm = dyv * s_
        return [dm], [dyv * mbv, dm]

    return _rowwise(fn, [dy, mb], [scale], [BF16], 2, name=name)


def _c_gate_bwd1(gb, cv, ds, *, name):
    def fn(t, v):
        gbv, cvv, dsv = t
        return [dsv * cvv, dsv * gbv], []

    return _rowwise(fn, [gb, cv, ds], [], [BF16, F32], 0, name=name)


def _c_gate_bwd2(gc, v_, dp, *, name):
    def fn(t, v):
        gcv, vv, dpv = t
        return [dpv * vv, dpv * gcv], []

    return _rowwise(fn, [gc, v_, dp], [], [BF16, BF16], 0, name=name)


def _stage_causal(buf, t, halo, tt):
    @pl.when(t == 0)
    def _():
        buf[pl.ds(0, halo), :] = jnp.zeros((halo, buf.shape[1]), F32)

    @pl.when(t > 0)
    def _():
        buf[pl.ds(0, halo), :] = buf[pl.ds(tt, halo), :]


def _stage_anticausal(buf, t, halo, tt):
    @pl.when(t == 0)
    def _():
        buf[pl.ds(tt, halo), :] = jnp.zeros((halo, buf.shape[1]), F32)

    @pl.when(t > 0)
    def _():
        buf[pl.ds(tt, halo), :] = buf[pl.ds(0, halo), :]


SUBLANES = 8


def _make_shifted(buf, shifted):
    n = shifted.shape[1]
    for b in range(1, SUBLANES):
        shifted[b - 1, :, :] = buf[pl.ds(b, n), :]


def _rows_at(buf, shifted, offset, rows, cols):
    b = offset % SUBLANES
    if shifted is None or b == 0:
        return buf[pl.ds(offset, rows), cols]
    return shifted[b - 1, pl.ds(offset - b, rows), cols]


def _causal_taps(buf, w_ref, taps, halo, r0, rows, cols, shifted=None):
    acc = None
    for k in range(taps):
        term = w_ref[pl.ds(k, 1), cols] * _rows_at(buf, shifted, halo - (taps - 1) + k + r0, rows, cols)
        acc = term if acc is None else acc + term
    return acc


def _halo_tiles(T, C, halo, col_cap):
    tt = _tile(T, 512 if halo <= 8 else 256, 16)
    assert tt >= halo, (tt, halo)
    return tt, _tile(C, col_cap, LANE), _tile(tt, 128, 16)


def _col_blocks(tc):
    cb = LANE if tc % LANE == 0 else tc
    return [pl.ds(c0, cb) for c0 in range(0, tc, cb)]


def _ffn_act_fwd(ug, uv, wg, wv, bg, bv, *, name):
    T, F = ug.shape
    taps, halo = wg.shape[0], 8
    tt, tc, rc = _halo_tiles(T, F, halo, 1024)

    def body(ug_ref, uv_ref, wg_ref, wv_ref, bg_ref, bv_ref, a_ref, yg_ref, yv_ref, bufg, bufv):
        t = pl.program_id(1)
        for u_ref, buf in ((ug_ref, bufg), (uv_ref, bufv)):
            _stage_causal(buf, t, halo, tt)
            buf[pl.ds(halo, tt), :] = u_ref[...]
        for cols in _col_blocks(tc):
            for r0 in range(0, tt, rc):
                rows = pl.ds(r0, rc)
                yg = _causal_taps(bufg, wg_ref, taps, halo, r0, rc, cols) + bg_ref[:, cols]
                yv = _causal_taps(bufv, wv_ref, taps, halo, r0, rc, cols) + bv_ref[:, cols]
                a_ref[rows, cols] = (yg * _sigmoid(yg) * yv).astype(BF16)
                yg_ref[rows, cols] = yg.astype(BF16)
                yv_ref[rows, cols] = yv.astype(BF16)

    blk = pl.BlockSpec((tt, tc), lambda j, t: (t, j))
    wspec = pl.BlockSpec((taps, tc), lambda j, t: (0, j))
    vspec = pl.BlockSpec((1, tc), lambda j, t: (0, j))
    return pl.pallas_call(
        body, name=name, grid=(F // tc, T // tt), in_specs=[blk, blk, wspec, wspec, vspec, vspec],
        out_specs=[blk, blk, blk], out_shape=[jax.ShapeDtypeStruct((T, F), BF16)] * 3,
        scratch_shapes=[pltpu.VMEM((halo + tt, tc), F32), pltpu.VMEM((halo + tt, tc), F32)],
        compiler_params=_params("parallel", "arbitrary"),
    )(ug, uv, wg, wv, bg, bv)


def _ffn_act_bwd(yg, yv, da, *, name):
    T, F = yg.shape
    tt, tc = _tile(T, 512, 16), _tile(F, 1024, LANE)
    rc = _tile(tt, 128, 16)

    def body(yg_ref, yv_ref, da_ref, dg_ref, dv_ref, sg_ref, sv_ref):
        @pl.when(pl.program_id(1) == 0)
        def _():
            sg_ref[...] = jnp.zeros_like(sg_ref)
            sv_ref[...] = jnp.zeros_like(sv_ref)

        for cols in _col_blocks(tc):
            sum_g = sum_v = None
            for r0 in range(0, tt, rc):
                rows = pl.ds(r0, rc)
                g, v = yg_ref[rows, cols].astype(F32), yv_ref[rows, cols].astype(F32)
                s = _sigmoid(g)
                d = da_ref[rows, cols]
                dyg = d * v * (s * (1.0 + g * (1.0 - s)))
                dyv = d * (g * s)
                dg_ref[rows, cols] = dyg
                dv_ref[rows, cols] = dyv
                part_g, part_v = jnp.sum(dyg, axis=0, keepdims=True), jnp.sum(dyv, axis=0, keepdims=True)
                sum_g = part_g if sum_g is None else sum_g + part_g
                sum_v = part_v if sum_v is None else sum_v + part_v
            sg_ref[:, cols] += sum_g
            sv_ref[:, cols] += sum_v

    blk = pl.BlockSpec((tt, tc), lambda j, t: (t, j))
    vspec = pl.BlockSpec((1, tc), lambda j, t: (0, j))
    td, vd = jax.ShapeDtypeStruct((T, F), F32), jax.ShapeDtypeStruct((1, F), F32)
    return pl.pallas_call(
        body, name=name, grid=(F // tc, T // tt), in_specs=[blk, blk, blk], out_specs=[blk, blk, vspec, vspec],
        out_shape=[td, td, vd, vd], compiler_params=_params("parallel", "arbitrary"),
    )(yg, yv, da)


def _dwconv_bwd(x, dy, w, out_dtype, *, name):
    T, C = x.shape
    taps = w.shape[0]
    halo = _round_up(taps - 1, 8)
    tt, tc, rc = _halo_tiles(T, C, halo, 512)
    nt_ = T // tt
    long_conv = taps > SUBLANES

    def body(x_ref, dy_ref, w_ref, dx_ref, dw_ref, buf, *scratch):
        t = pl.program_id(1)
        shifted = scratch[0] if long_conv else None

        @pl.when(t == 0)
        def _():
            dw_ref[...] = jnp.zeros_like(dw_ref)

        _stage_anticausal(buf, t, halo, tt)
        buf[pl.ds(0, tt), :] = dy_ref[...]
        if long_conv:
            _make_shifted(buf, shifted)
        for cols in _col_blocks(tc):
            for r0 in range(0, tt, rc):
                rows = pl.ds(r0, rc)
                xv = x_ref[rows, cols]
                acc = None
                for j in range(taps):
                    k = taps - 1 - j
                    ahead = _rows_at(buf, shifted, r0 + j, rc, cols)
                    term = w_ref[pl.ds(k, 1), cols] * ahead
                    acc = term if acc is None else acc + term
                    dw_ref[pl.ds(k, 1), cols] += jnp.sum(xv * ahead, axis=0, keepdims=True)
                dx_ref[rows, cols] = acc.astype(dx_ref.dtype)

    blk = pl.BlockSpec((tt, tc), lambda j, t: (nt_ - 1 - t, j))
    wspec = pl.BlockSpec((taps, tc), lambda j, t: (0, j))
    return pl.pallas_call(
        body, name=name, grid=(C // tc, nt_), in_specs=[blk, blk, wspec], out_specs=[blk, wspec],
        out_shape=[jax.ShapeDtypeStruct((T, C), out_dtype), jax.ShapeDtypeStruct((taps, C), F32)],
        scratch_shapes=[pltpu.VMEM((tt + halo, tc), F32)] + (
            [pltpu.VMEM((SUBLANES - 1, tt + halo - SUBLANES, tc), F32)] if long_conv else []),
        compiler_params=_params("parallel", "arbitrary"),
    )(x, dy, w)


def _a_conv_fwd(ua, ug, w, b, ln_g, ln_b, *, name):
    T, D = ua.shape
    taps = w.shape[0]
    halo = _round_up(taps - 1, 8)
    tt = _tile(T, 128, 16)
    assert tt >= halo
    rc, ch = _tile(tt, 128, 16), _tile(tt, 32, 16)

    def body(ua_ref, ug_ref, w_ref, b_ref, g_ref, lb_ref, p_ref, z_ref, q_ref, buf, shifted):
        t = pl.program_id(0)
        _stage_causal(buf, t, halo, tt)
        for r0 in range(0, tt, ch):
            rows = pl.ds(r0, ch)
            pv = ua_ref[rows, :] * _sigmoid(ug_ref[rows, :])
            p_ref[rows, :] = pv
            buf[pl.ds(halo + r0, ch), :] = pv
        _make_shifted(buf, shifted)
        for cols in _col_blocks(D):
            for r0 in range(0, tt, rc):
                z_ref[pl.ds(r0, rc), cols] = (
                    _causal_taps(buf, w_ref, taps, halo, r0, rc, cols, shifted) + b_ref[:, cols])
        for r0 in range(0, tt, ch):
            rows = pl.ds(r0, ch)
            zv = z_ref[rows, :]
            zc = zv - jnp.mean(zv, axis=-1, keepdims=True)
            l = zc * lax.rsqrt(jnp.mean(zc * zc, axis=-1, keepdims=True) + LN_EPS) * g_ref[...] + lb_ref[...]
            q_ref[rows, :] = (l * _sigmoid(l)).astype(BF16)

    blk = pl.BlockSpec((tt, D), lambda t: (t, 0))
    vec = pl.BlockSpec((1, D), lambda t: (0, 0))
    td = jax.ShapeDtypeStruct((T, D), F32)
    return pl.pallas_call(
        body, name=name, grid=(T // tt,),
        in_specs=[blk, blk, pl.BlockSpec((taps, D), lambda t: (0, 0)), vec, vec, vec], out_specs=[blk, blk, blk],
        out_shape=[td, td, jax.ShapeDtypeStruct((T, D), BF16)],
        scratch_shapes=[pltpu.VMEM((halo + tt, D), F32), pltpu.VMEM((SUBLANES - 1, halo + tt - SUBLANES, D), F32)],
        compiler_params=_params("arbitrary"),
    )(ua, ug, w, b, ln_g, ln_b)


def _c_gate_fwd(gb, gc, v, w, *, name):
    T, D = gb.shape
    taps, halo = w.shape[0], 8
    tt, tc, rc = _halo_tiles(T, D, halo, 1024)

    def body(gb_ref, gc_ref, v_ref, w_ref, p_ref, cv_ref, s_ref, buf):
        t = pl.program_id(1)
        _stage_causal(buf, t, halo, tt)
        pv = gc_ref[...] * v_ref[...]
        p_ref[...] = pv
        buf[pl.ds(halo, tt), :] = pv
        for cols in _col_blocks(tc):
            for r0 in range(0, tt, rc):
                rows = pl.ds(r0, rc)
                cv = _causal_taps(buf, w_ref, taps, halo, r0, rc, cols)
                cv_ref[rows, cols] = cv
                s_ref[rows, cols] = (gb_ref[rows, cols] * cv).astype(BF16)

    blk = pl.BlockSpec((tt, tc), lambda j, t: (t, j))
    td = jax.ShapeDtypeStruct((T, D), F32)
    return pl.pallas_call(
        body, name=name, grid=(D // tc, T // tt),
        in_specs=[blk, blk, blk, pl.BlockSpec((taps, tc), lambda j, t: (0, j))], out_specs=[blk, blk, blk],
        out_shape=[td, td, jax.ShapeDtypeStruct((T, D), BF16)],
        scratch_shapes=[pltpu.VMEM((halo + tt, tc), F32)], compiler_params=_params("parallel", "arbitrary"),
    )(gb, gc, v, w)


def _pool_count(base, r0, rows, width, window):
    pos = lax.broadcasted_iota(jnp.int32, (rows, width), 0) + (base + r0 + 1)
    return jnp.minimum(pos, window).astype(F32)


def _pool_fwd(h, *, name):
    T, D = h.shape
    halo = _round_up(max(POOL_WINDOWS), 8)
    gd = D // len(POOL_WINDOWS)
    tt = _tile(T, 256, 16)
    assert tt >= halo
    rc = _tile(tt, 128, 16)

    def body(h_ref, o_ref, buf):
        t = pl.program_id(0)
        _stage_causal(buf, t, halo, tt)
        buf[pl.ds(halo, tt), :] = h_ref[...]
        for g, window in enumerate(POOL_WINDOWS):
            for cols_in_group in _col_blocks(gd):
                cols = pl.ds(g * gd + cols_in_group.start, cols_in_group.size)
                for r0 in range(0, tt, rc):
                    acc = None
                    for j in range(window):
                        term = buf[pl.ds(halo - j + r0, rc), cols]
                        acc = term if acc is None else acc + term
                    cnt = _pool_count(t * tt, r0, rc, cols.size, window)
                    o_ref[pl.ds(r0, rc), cols] = (acc / cnt - buf[pl.ds(halo + r0, rc), cols]).astype(BF16)

    blk = pl.BlockSpec((tt, D), lambda t: (t, 0))
    return pl.pallas_call(
        body, name=name, grid=(T // tt,), in_specs=[blk], out_specs=blk, out_shape=jax.ShapeDtypeStruct((T, D), BF16),
        scratch_shapes=[pltpu.VMEM((halo + tt, D), F32)], compiler_params=_params("arbitrary"),
    )(h)


def _pool_bwd(dp, *, name):
    T, D = dp.shape
    halo = _round_up(max(POOL_WINDOWS), 8)
    gd = D // len(POOL_WINDOWS)
    tt = _tile(T, 256, 16)
    assert tt >= halo
    rc = _tile(tt, 128, 16)
    nt_ = T // tt

    def body(dp_ref, o_ref, buf):
        t = pl.program_id(0)
        base = (nt_ - 1 - t) * tt
        _stage_anticausal(buf, t, halo, tt)
        for g, window in enumerate(POOL_WINDOWS):
            for cols_in_group in _col_blocks(gd):
                cols = pl.ds(g * gd + cols_in_group.start, cols_in_group.size)
                for r0 in range(0, tt, rc):
                    rows = pl.ds(r0, rc)
                    buf[rows, cols] = dp_ref[rows, cols] / _pool_count(base, r0, rc, cols.size, window)
                for r0 in range(0, tt, rc):
                    acc = None
                    for j in range(window):
                        term = buf[pl.ds(r0 + j, rc), cols]
                        acc = term if acc is None else acc + term
                    o_ref[pl.ds(r0, rc), cols] = acc - dp_ref[pl.ds(r0, rc), cols]

    blk = pl.BlockSpec((tt, D), lambda t: (nt_ - 1 - t, 0))
    return pl.pallas_call(
        body, name=name, grid=(nt_,), in_specs=[blk], out_specs=blk, out_shape=jax.ShapeDtypeStruct((T, D), F32),
        scratch_shapes=[pltpu.VMEM((tt + halo, D), F32)], compiler_params=_params("arbitrary"),
    )(dp)


def _sum_slabs(slabs, *, name):
    S, R, C = slabs.shape

    def body(s_ref, o_ref):
        acc = s_ref[0]
        for p in range(1, S):
            acc = acc + s_ref[p]
        o_ref[...] = acc

    return pl.pallas_call(body, name=name, out_shape=jax.ShapeDtypeStruct((R, C), F32),
                          compiler_params=_params())(slabs)


def _adamw(w, slabs, m, v, *, name):
    S, R, C = slabs.shape
    sub = 16 if slabs.dtype == BF16 else 8
    tr = _tile(R, max(sub, (1 << 18) // (C * S)), sub)
    c1, c2 = 1.0 - ADAM_B1 ** ADAM_STEP, 1.0 - ADAM_B2 ** ADAM_STEP

    def body(w_ref, s_ref, m_ref, v_ref, g_ref, d_ref, nm_ref, nv_ref):
        g = s_ref[0].astype(F32)
        for p in range(1, S):
            g = g + s_ref[p].astype(F32)
        nm = ADAM_B1 * m_ref[...] + (1.0 - ADAM_B1) * g
        nv = ADAM_B2 * v_ref[...] + (1.0 - ADAM_B2) * (g * g)
        g_ref[...] = g
        nm_ref[...] = nm
        nv_ref[...] = nv
        d_ref[...] = -ADAM_LR * ((nm / c1) / (jnp.sqrt(nv / c2) + ADAM_EPS) + ADAM_WD * w_ref[...])

    blk = pl.BlockSpec((tr, C), lambda i: (i, 0))
    sd = jax.ShapeDtypeStruct((R, C), F32)
    return pl.pallas_call(
        body, name=name, grid=(R // tr,), in_specs=[blk, pl.BlockSpec((S, tr, C), lambda i: (0, i, 0)), blk, blk],
        out_specs=[blk] * 4, out_shape=[sd] * 4, compiler_params=_params("parallel"),
    )(w, slabs, m, v)


def _cols_natural(g):
    g = jnp.moveaxis(g, 0, -2)
    return g.reshape(g.shape[:-2] + (g.shape[-2] * g.shape[-1],))


def _rows_natural(g):
    g = jnp.moveaxis(g, 0, 1)
    return g.reshape((g.shape[0], g.shape[1] * g.shape[2]) + g.shape[3:])


def _col_slabs(full, n_shards=N_DEV):
    n = full.shape[-1] // n_shards
    return jnp.moveaxis(full.reshape(full.shape[:-1] + (n_shards, n)), -2, 0)


def _pad_last(a, n):
    return jnp.pad(a, [(0, 0)] * (a.ndim - 1) + [(0, n - a.shape[-1])])


def kernel(x, c, mod_w, mod_b, norm_pre_mix, norm_post_mix, norm_pre_ffn, norm_post_ffn, a_pw1_w, a_pw1_b, a_dw_w, a_dw_b, a_ln_g, a_ln_b, a_pw2_w, a_pw2_b, b_group_w, b_group_b, b_scale, c_in_w, c_conv_w, c_out_w, f_up_w, f_dw_w, f_dw_b, f_down_w, loss_target, m_mod_w, m_mod_b, m_norm_pre_mix, m_norm_post_mix, m_norm_pre_ffn, m_norm_post_ffn, m_a_pw1_w, m_a_pw1_b, m_a_dw_w, m_a_dw_b, m_a_ln_g, m_a_ln_b, m_a_pw2_w, m_a_pw2_b, m_b_group_w, m_b_group_b, m_b_scale, m_c_in_w, m_c_conv_w, m_c_out_w, m_f_up_w, m_f_dw_w, m_f_dw_b, m_f_down_w, v_mod_w, v_mod_b, v_norm_pre_mix, v_norm_post_mix, v_norm_pre_ffn, v_norm_post_ffn, v_a_pw1_w, v_a_pw1_b, v_a_dw_w, v_a_dw_b, v_a_ln_g, v_a_ln_b, v_a_pw2_w, v_a_pw2_b, v_b_group_w, v_b_group_b, v_b_scale, v_c_in_w, v_c_conv_w, v_c_out_w, v_f_up_w, v_f_dw_w, v_f_dw_b, v_f_down_w):
    p = dict(locals())
    assert list(p) == INPUTS
    x0 = p['x'][0]
    target = p['loss_target'][0]
    T, D = x0.shape
    L = p['mod_w'].shape[0]
    G = len(POOL_WINDOWS)
    gd = D // G
    ns = p['f_up_w'].shape[-1]
    npad = _round_up(ns, LANE)
    Fp = 4 * npad
    me = 4 * lax.axis_index("x") + 2 * lax.axis_index("y") + lax.axis_index("c")
    row = lambda a, i: a[i:i + 1]

    big = ['a_pw1_w', 'a_pw2_w', 'b_group_w', 'c_in_w', 'c_out_w', 'f_down_w']
    small = ['a_pw1_b', 'a_dw_w', 'a_dw_b', 'a_ln_g', 'a_ln_b', 'a_pw2_b', 'c_conv_w']
    send = [p[k].astype(BF16) for k in big] + [_pad_last(p['f_up_w'], npad).astype(BF16)]
    send += [p[k] for k in small] + [_pad_last(p['f_dw_w'], npad), p['c']]
    got = _gather_two_level(send, name="gather_weights")
    gw = dict(zip(big + ['f_up_w'] + small + ['f_dw_w', 'c'], got))

    w1 = _cols_natural(gw['a_pw1_w'])
    b1 = _cols_natural(gw['a_pw1_b'])
    a_dw_w = _cols_natural(gw['a_dw_w'])
    a_dw_b, a_ln_g, a_ln_b, b2 = (_cols_natural(gw[k]) for k in ('a_dw_b', 'a_ln_g', 'a_ln_b', 'a_pw2_b'))
    w2 = _rows_natural(gw['a_pw2_w'])
    bgw = jnp.moveaxis(gw['b_group_w'], 0, 2).reshape((-1, G, gd, gd))
    cin = _cols_natural(gw['c_in_w'])
    cconv = _cols_natural(gw['c_conv_w'])
    cout = _rows_natural(gw['c_out_w'])
    wup_g, wup_v = _cols_natural(gw['f_up_w'][:4]), _cols_natural(gw['f_up_w'][4:])
    fdw_g, fdw_v = _cols_natural(gw['f_dw_w'][:4]), _cols_natural(gw['f_dw_w'][4:])
    pad_ff = lambda a: _pad_last(a.reshape(a.shape[:-1] + (4, ns)), npad).reshape(a.shape[:-1] + (Fp,))
    fdb_g, fdb_v = pad_ff(p['f_dw_b'][:, :4 * ns]), pad_ff(p['f_dw_b'][:, 4 * ns:])
    wdown = _rows_natural(gw['f_down_w'])
    wdown = jnp.pad(wdown.reshape(L, 4, ns, D), ((0, 0), (0, 0), (0, npad - ns), (0, 0))).reshape(L, Fp, D)

    c_all = gw['c'].reshape(N_DEV, D)
    n6 = p['mod_w'].shape[-1]
    mod_part = _mod_fwd(c_all, p['mod_w'], lax.dynamic_slice_in_dim(p['mod_b'], me * n6, n6, axis=1), name="mod_fwd")
    mod_all = _exchange([mod_part], scatter=False, name="gather_mod")[0]
    mod_all = jnp.moveaxis(mod_all, 0, 2).reshape(L, N_DEV, N_DEV * n6)
    mod = lax.dynamic_index_in_dim(mod_all, me, axis=1, keepdims=False).reshape(L, 6, D)

    saved = []
    x = x0
    h = _prenorm_fwd(x, row(p['norm_pre_mix'], 0), row(mod[0], 1), row(mod[0], 0), BF16, name="prenorm_first")
    dx = loss_cols = None
    for l in range(L):
        kind, slot = l % N_MIXERS, l // N_MIXERS
        sh_m, sc_m, gt_m, sh_f, sc_f, gt_f = (row(mod[l], i) for i in range(6))
        s = dict(x=x, h=h)
        if kind == 0:
            ua = _mm(h, w1[slot, :, :D], bias=b1[slot:slot + 1, :D], name=f"a_pw1_a_{l}")
            ug = _mm(h, w1[slot, :, D:], bias=b1[slot:slot + 1, D:], name=f"a_pw1_g_{l}")
            pa, z, q = _a_conv_fwd(ua, ug, a_dw_w[slot], a_dw_b[slot:slot + 1], a_ln_g[slot:slot + 1],
                                   a_ln_b[slot:slot + 1], name=f"a_conv_fwd_{l}")
            y = _mm(q, w2[slot], bias=b2[slot:slot + 1], name=f"a_pw2_{l}")
            s.update(ua=ua, ug=ug, p=pa, z=z, q=q)
        elif kind == 1:
            pooled = _pool_fwd(h, name=f"pool_fwd_{l}")
            mb, y = _group_mm(pooled, bgw[slot], bias=p['b_group_b'][slot:slot + 1],
                              scale=p['b_scale'][slot:slot + 1], name=f"b_mix_{l}")
            s.update(pooled=pooled, mb=mb)
        else:
            gb, gc, v = (_mm(h, cin[slot, :, i * D:(i + 1) * D], name=f"c_in_{i}_{l}") for i in range(3))
            pc, cv, sg = _c_gate_fwd(gb, gc, v, cconv[slot], name=f"c_gate_fwd_{l}")
            y = _mm(sg, cout[slot], name=f"c_out_{l}")
            s.update(gb=gb, gc=gc, v=v, p=pc, cv=cv, sg=sg)
        x1, h2 = _post_pre_fwd(x, y, row(p['norm_post_mix'], l), gt_m, row(p['norm_pre_ffn'], l), sc_f, sh_f, BF16,
                               name=f"post_mix_{l}")
        fug = _mm(h2, wup_g[l], name=f"f_up_g_{l}")
        fuv = _mm(h2, wup_v[l], name=f"f_up_v_{l}")
        act, fyg, fyv = _ffn_act_fwd(fug, fuv, fdw_g[l], fdw_v[l], fdb_g[l:l + 1], fdb_v[l:l + 1],
                                     name=f"ffn_act_fwd_{l}")
        y2 = _mm(act, wdown[l], name=f"f_down_{l}")
        s.update(y=y, x1=x1, h2=h2, fug=fug, fuv=fuv, fyg=fyg, fyv=fyv, act=act, y2=y2)
        saved.append(s)
        if l + 1 < L:
            nxt_dtype = F32 if (l + 1) % N_MIXERS == 1 else BF16
            x, h = _post_pre_fwd(x1, y2, row(p['norm_post_ffn'], l), gt_f, row(p['norm_pre_mix'], l + 1),
                                 row(mod[l + 1], 1), row(mod[l + 1], 0), nxt_dtype, name=f"post_ffn_{l}")
        else:
            dx, loss_cols = _post_loss(x1, y2, row(p['norm_post_ffn'], l), gt_f, target, name="post_loss")
    loss = lax.psum(0.5 * jnp.sum(loss_cols) / D, AXES)

    gfull = {}
    for k in ('a_pw1_w', 'a_pw1_b', 'a_dw_w', 'a_dw_b', 'a_ln_g', 'a_ln_b', 'a_pw2_w', 'a_pw2_b', 'b_group_w',
              'c_in_w', 'c_conv_w', 'c_out_w', 'f_up_w', 'f_dw_w', 'f_down_w'):
        gfull[k] = {}
    dmod, dnorm, d_fdb = {}, {}, {}
    d_bgb = d_bscale = None
    for l in reversed(range(L)):
        kind, slot = l % N_MIXERS, l // N_MIXERS
        s = saved[l]
        sh_m, sc_m, gt_m, sh_f, sc_f, gt_f = (row(mod[l], i) for i in range(6))
        dy2, dg4, dgt_f, _ = _post_bwd(s['y2'], dx, row(p['norm_post_ffn'], l), gt_f, BF16, name=f"post_ffn_bwd_{l}")
        da = _mm(dy2, wdown[l], nt=True, name=f"f_down_dx_{l}")
        dwd = _mm(_transpose(dy2, name=f"dy2_t_{l}"), s['act'], out_dtype=BF16, name=f"f_down_dw_{l}").T
        gfull['f_down_w'][l] = dwd.reshape(4, npad, D)[:, :ns].reshape(N_DEV, ns // 2, D)
        dcg, dcv, dbg, dbv = _ffn_act_bwd(s['fyg'], s['fyv'], da, name=f"ffn_act_bwd_{l}")
        dug, dwg = _dwconv_bwd(s['fug'], dcg, fdw_g[l], BF16, name=f"f_dw_bwd_g_{l}")
        duv, dwv = _dwconv_bwd(s['fuv'], dcv, fdw_v[l], BF16, name=f"f_dw_bwd_v_{l}")
        dh2 = _mm(dug, wup_g[l], nt=True, name=f"f_up_dx_g_{l}")
        dh2 = _mm(duv, wup_v[l], nt=True, add=dh2, name=f"f_up_dx_v_{l}")
        h2t = _transpose(s['h2'], name=f"h2_t_{l}")
        gfull['f_up_w'][l] = jnp.concatenate(
            [_mm(h2t, dug, shard_cols=npad, out_dtype=BF16, name=f"f_up_dw_g_{l}"),
             _mm(h2t, duv, shard_cols=npad, out_dtype=BF16, name=f"f_up_dw_v_{l}")], axis=0)
        gfull['f_dw_w'][l] = jnp.concatenate([_col_slabs(dwg, 4), _col_slabs(dwv, 4)], axis=0)
        unpad = lambda a: a.reshape(4, npad)[:, :ns].reshape(4 * ns)
        d_fdb[l] = jnp.concatenate([unpad(dbg), unpad(dbv)])
        dx1, dsh_f, dsc_f, dg3 = _pre_bwd(s['x1'], dh2, dx, row(p['norm_pre_ffn'], l), sc_f, name=f"pre_ffn_bwd_{l}")
        dy_dtype = F32 if kind == 1 else BF16
        dy, dg2, dgt_m, dy_sum = _post_bwd(s['y'], dx1, row(p['norm_post_mix'], l), gt_m, dy_dtype,
                                           name=f"post_mix_bwd_{l}")
        if kind == 0:
            dq = _mm(dy, w2[slot], nt=True, name=f"a_pw2_dx_{l}")
            gfull['a_pw2_w'][slot] = _mm(_transpose(s['q'], name=f"q_t_{l}"), dy, out_dtype=BF16,
                                         name=f"a_pw2_dw_{l}").reshape(N_DEV, D // N_DEV, D)
            gfull['a_pw2_b'][slot] = dy_sum.reshape(N_DEV, D // N_DEV)
            dz, dlg, dlb, dzs = _a_ln_bwd(s['z'], dq, a_ln_g[slot:slot + 1], a_ln_b[slot:slot + 1], name=f"a_ln_bwd_{l}")
            gfull['a_ln_g'][slot], gfull['a_ln_b'][slot], gfull['a_dw_b'][slot] = (
                t.reshape(N_DEV, D // N_DEV) for t in (dlg, dlb, dzs))
            dp, ddw = _dwconv_bwd(s['p'], dz, a_dw_w[slot], F32, name=f"a_dw_bwd_{l}")
            gfull['a_dw_w'][slot] = _col_slabs(ddw)
            dua, dug_a, sua, sug = _a_glu_bwd(s['ua'], s['ug'], dp, name=f"a_glu_bwd_{l}")
            gfull['a_pw1_b'][slot] = jnp.concatenate([sua, sug], axis=1).reshape(N_DEV, 2 * D // N_DEV)
            dh = _mm(dua, w1[slot, :, :D], nt=True, name=f"a_pw1_dx_a_{l}")
            dh = _mm(dug_a, w1[slot, :, D:], nt=True, add=dh, name=f"a_pw1_dx_g_{l}")
            ht = _transpose(s['h'], name=f"h_t_{l}")
            n1 = 2 * D // N_DEV
            gfull['a_pw1_w'][slot] = jnp.concatenate(
                [_mm(ht, dua, shard_cols=n1, out_dtype=BF16, name=f"a_pw1_dw_a_{l}"),
                 _mm(ht, dug_a, shard_cols=n1, out_dtype=BF16, name=f"a_pw1_dw_g_{l}")], axis=0)
        elif kind == 1:
            dm, d_bscale, d_bgb = _b_scale_bwd(dy, s['mb'], p['b_scale'][slot:slot + 1], name=f"b_scale_bwd_{l}")
            dpool = _group_mm(dm, bgw[slot], nt=True, name=f"b_mix_dx_{l}")
            dgw = _group_dw(_transpose(s['pooled'], name=f"pooled_t_{l}"), dm, G, name=f"b_mix_dw_{l}")
            gfull['b_group_w'][slot] = jnp.moveaxis(dgw.astype(BF16).reshape(G, N_DEV, gd // N_DEV, gd), 1, 0)
            dh = _pool_bwd(dpool, name=f"pool_bwd_{l}")
        else:
            dsg = _mm(dy, cout[slot], nt=True, name=f"c_out_dx_{l}")
            gfull['c_out_w'][slot] = _mm(_transpose(s['sg'], name=f"sg_t_{l}"), dy, out_dtype=BF16,
                                         name=f"c_out_dw_{l}").reshape(N_DEV, D // N_DEV, D)
            dgb, dcv_c = _c_gate_bwd1(s['gb'], s['cv'], dsg, name=f"c_gate_bwd1_{l}")
            dpc, dcw = _dwconv_bwd(s['p'], dcv_c, cconv[slot], F32, name=f"c_conv_bwd_{l}")
            gfull['c_conv_w'][slot] = _col_slabs(dcw)
            dgc, dv = _c_gate_bwd2(s['gc'], s['v'], dpc, name=f"c_gate_bwd2_{l}")
            ht = _transpose(s['h'], name=f"h_t_{l}")
            dh, parts = None, []
            for i, d_i in enumerate((dgb, dgc, dv)):
                dh = _mm(d_i, cin[slot, :, i * D:(i + 1) * D], nt=True, add=dh, name=f"c_in_dx_{i}_{l}")
                parts.append(_mm(ht, d_i, out_dtype=BF16, name=f"c_in_dw_{i}_{l}"))
            gfull['c_in_w'][slot] = _col_slabs(jnp.concatenate(parts, axis=1))
        dx, dsh_m, dsc_m, dg1 = _pre_bwd(s['x'], dh, dx1, row(p['norm_pre_mix'], l), sc_m, name=f"pre_mix_bwd_{l}")
        dmod[l] = jnp.concatenate([dsh_m, dsc_m, dgt_m, dsh_f, dsc_f, dgt_f], axis=0)
        dnorm[l] = (dg1, dg2, dg3, dg4)

    zero_row = jnp.zeros((1, D), F32)
    rows_ = [dmod[l] for l in range(L)]
    rows_ += [dnorm[l][i] for i in range(4) for l in range(L)]
    rows_ += [d_bgb if d_bgb is not None else zero_row, d_bscale if d_bscale is not None else zero_row]
    fdb_flat = jnp.concatenate([d_fdb[l] for l in range(L)])
    rows_ += [_pad_last(fdb_flat, _round_up(fdb_flat.size, D)).reshape(-1, D)]
    pack = jnp.concatenate(rows_, axis=0)
    pack = jnp.pad(pack, ((0, _round_up(pack.shape[0], 8) - pack.shape[0]), (0, 0)))
    pack_all = _exchange([pack], scatter=False, name="gather_small_grads")[0]
    red = _sum_slabs(pack_all, name="sum_small_grads")
    g_rep = {'mod_b': red[:6 * L].reshape(L, 6 * D)}
    for i, k in enumerate(('norm_pre_mix', 'norm_post_mix', 'norm_pre_ffn', 'norm_post_ffn')):
        g_rep[k] = red[6 * L + i * L:6 * L + (i + 1) * L]
    o = 6 * L + 4 * L
    g_rep['b_group_b'], g_rep['b_scale'] = red[o:o + 1], red[o + 1:o + 2]
    g_rep['f_dw_b'] = red[o + 2:].reshape(-1)[:L * 8 * ns].reshape(L, 8 * ns)

    dmod_all = pack_all[:, :6 * L].reshape(N_DEV, L, 6 * D)
    dmod_mine = jnp.moveaxis(lax.dynamic_slice_in_dim(dmod_all, me * n6, n6, axis=2), 0, 1)
    g_mod_w = _mod_bwd(c_all.T, dmod_mine, name="mod_bwd")

    stack = lambda d: jnp.stack([d[i] for i in sorted(d)], axis=1)
    big_g = [k for k in gfull if gfull[k] and k.endswith('_w') and k not in ('a_dw_w', 'c_conv_w', 'f_dw_w')]
    small_g = [k for k in gfull if gfull[k] and k not in big_g]
    full = [stack(gfull[k]) for k in big_g]
    got = _swap_cores(full, name="scatter_grads_cores")
    core = lax.axis_index("c").astype(jnp.int32).reshape(1)
    partial = [_add_core_slabs(a, b, core, name=f"add_core_slabs_{k}") for k, a, b in zip(big_g, full, got)]
    slabs = dict(zip(big_g, _swap_chips(partial, name="scatter_grads_chips")))
    slabs.update(zip(small_g, _exchange([stack(gfull[k]) for k in small_g], scatter=True, name="scatter_small_grads")))

    out = {}
    for k in WEIGHTS:
        w, m, v = p[k], p['m_' + k], p['v_' + k]
        if k in g_rep:
            g = g_rep[k].reshape((1,) + w.shape)
        elif k == 'mod_w':
            g = g_mod_w.reshape((1,) + w.shape)
        else:
            g = slabs[k]
        if k in ('f_up_w', 'f_dw_w'):
            w, m, v = (_pad_last(t, npad) for t in (w, m, v))
        shape = w.shape
        two_d = (-1, shape[-1])
        res = _adamw(w.reshape(two_d), g.reshape((g.shape[0],) + (w.size // shape[-1], shape[-1])), m.reshape(two_d),
                     v.reshape(two_d), name=f"adamw_{k}")
        res = [t.reshape(shape) for t in res]
        if k in ('f_up_w', 'f_dw_w'):
            res = [t[..., :ns] for t in res]
        out[k] = res

    grads, deltas, new_m, new_v = ([out[k][i] for k in WEIGHTS] for i in range(4))
    return (loss, dx.reshape(1, T, D), *grads, *deltas, *new_m, *new_v)
```

```python
import functools

import jax
import jax.numpy as jnp
from jax import lax
from jax.experimental import pallas as pl
from jax.experimental.pallas import tpu as pltpu

F32, BF16 = jnp.float32, jnp.bfloat16
AXES = ("x", "y", "c")
N_DEV = 8
LANE = 128
VMEM_LIMIT = 48 * 1024 * 1024
RMS_EPS, LN_EPS = 1e-6, 1e-5
POOL_WINDOWS = (2, 4, 8, 16)
N_MIXERS = 3
ADAM_LR, ADAM_B1, ADAM_B2, ADAM_EPS, ADAM_WD, ADAM_STEP = 0.001, 0.9, 0.999, 1e-08, 0.01, 10

WEIGHTS = ['mod_w', 'mod_b', 'norm_pre_mix', 'norm_post_mix', 'norm_pre_ffn', 'norm_post_ffn', 'a_pw1_w', 'a_pw1_b',
           'a_dw_w', 'a_dw_b', 'a_ln_g', 'a_ln_b', 'a_pw2_w', 'a_pw2_b', 'b_group_w', 'b_group_b', 'b_scale', 'c_in_w',
           'c_conv_w', 'c_out_w', 'f_up_w', 'f_dw_w', 'f_dw_b', 'f_down_w']
INPUTS = (['x', 'c'] + WEIGHTS + ['loss_target'] + ['m_' + w for w in WEIGHTS] + ['v_' + w for w in WEIGHTS])


def _tile(n, cap, mult):
    for t in range(min(n, cap), 0, -1):
        if n % t == 0 and t % mult == 0:
            return t
    return n


def _round_up(n, m):
    return -(-n // m) * m


def _params(*dims):
    return pltpu.CompilerParams(dimension_semantics=dims or None, vmem_limit_bytes=VMEM_LIMIT)


def _sigmoid(v):
    return jax.nn.sigmoid(v)


def _exchange(arrs, *, scatter, name):
    n = len(arrs)

    def body(*refs):
        ins, outs = refs[:n], refs[n:2 * n]
        send_sems, recv_sems, local_sems = refs[2 * n:]
        x, y, c = (lax.axis_index(a) for a in AXES)
        me = 4 * x + 2 * y + c

        def peer(k):
            px = 1 - x if (k >> 2) & 1 else x
            py = 1 - y if (k >> 1) & 1 else y
            pc = 1 - c if k & 1 else c
            return (px, py, pc), 4 * px + 2 * py + pc

        def remote(i, k, src, dst):
            return pltpu.make_async_remote_copy(
                src_ref=src, dst_ref=dst, send_sem=send_sems.at[i, k - 1], recv_sem=recv_sems.at[i, k - 1],
                device_id=peer(k)[0], device_id_type=pl.DeviceIdType.MESH)

        own = []
        for i in range(n):
            cp = pltpu.make_async_copy(ins[i].at[me] if scatter else ins[i], outs[i].at[me], local_sems.at[i])
            cp.start()
            own.append(cp)
        sent = []
        for k in range(1, N_DEV):
            for i in range(n):
                cp = remote(i, k, ins[i].at[peer(k)[1]] if scatter else ins[i], outs[i].at[me])
                cp.start()
                sent.append(cp)
        for k in range(1, N_DEV):
            for i in range(n):
                landing = outs[i].at[peer(k)[1]]
                remote(i, k, landing, landing).wait_recv()
        for cp in sent:
            cp.wait_send()
        for cp in own:
            cp.wait()

    any_spec = pl.BlockSpec(memory_space=pl.ANY)
    out_shape = [jax.ShapeDtypeStruct(a.shape if scatter else (N_DEV,) + a.shape, a.dtype) for a in arrs]
    return pl.pallas_call(
        body, name=name, out_shape=out_shape, in_specs=[any_spec] * n, out_specs=[any_spec] * n,
        scratch_shapes=[pltpu.SemaphoreType.DMA((n, N_DEV - 1)), pltpu.SemaphoreType.DMA((n, N_DEV - 1)),
                        pltpu.SemaphoreType.DMA((n,))],
    )(*arrs)


def _comm_call(body, arrs, out_shape, sem_shapes, name):
    any_spec = pl.BlockSpec(memory_space=pl.ANY)
    return pl.pallas_call(
        body, name=name, out_shape=out_shape, in_specs=[any_spec] * len(arrs), out_specs=[any_spec] * len(out_shape),
        scratch_shapes=[pltpu.SemaphoreType.DMA(s) for s in sem_shapes],
    )(*arrs)


def _gather_two_level(arrs, *, name):
    n = len(arrs)

    def body(*refs):
        ins, outs = refs[:n], refs[n:2 * n]
        send_sems, recv_sems, local_sems = refs[2 * n:]
        x, y, c = (lax.axis_index(a) for a in AXES)
        index = lambda px, py, pc: 4 * px + 2 * py + pc
        me, sibling = index(x, y, c), (x, y, 1 - c)
        chips = [(1 - x, y), (x, 1 - y), (1 - x, 1 - y)]

        def copy(i, k, block, to, src=None):
            return pltpu.make_async_remote_copy(
                src_ref=outs[i].at[block] if src is None else src, dst_ref=outs[i].at[block],
                send_sem=send_sems.at[i, k], recv_sem=recv_sems.at[i, k], device_id=to,
                device_id_type=pl.DeviceIdType.MESH)

        own = [pltpu.make_async_copy(ins[i], outs[i].at[me], local_sems.at[i]) for i in range(n)]
        first = [copy(i, 1 + j, me, (*chip, c), src=ins[i]) for j, chip in enumerate(chips) for i in range(n)]
        first += [copy(i, 0, me, sibling, src=ins[i]) for i in range(n)]
        for cp in own + first:
            cp.start()
        passed = []
        for j, chip in enumerate(chips):
            block = index(*chip, c)
            for i in range(n):
                copy(i, 1 + j, block, (x, y, c)).wait_recv()
            for i in range(n):
                cp = copy(i, 4 + j, block, sibling)
                cp.start()
                passed.append(cp)
        for i in range(n):
            copy(i, 0, index(x, y, 1 - c), (x, y, c)).wait_recv()
        for j, chip in enumerate(chips):
            for i in range(n):
                copy(i, 4 + j, index(*chip, 1 - c), (x, y, c)).wait_recv()
        for cp in first + passed:
            cp.wait_send()
        for cp in own:
            cp.wait()

    out_shape = [jax.ShapeDtypeStruct((N_DEV,) + a.shape, a.dtype) for a in arrs]
    return _comm_call(body, arrs, out_shape, [(n, 7), (n, 7), (n,)], name)


def _swap_cores(arrs, *, name):
    n = len(arrs)

    def body(*refs):
        ins, got = refs[:n], refs[n:2 * n]
        send_sems, recv_sems = refs[2 * n:]
        x, y, c = (lax.axis_index(a) for a in AXES)
        remote = [pltpu.make_async_remote_copy(
            src_ref=ins[i].at[2 * q + 1 - c], dst_ref=got[i].at[q], send_sem=send_sems.at[i, q],
            recv_sem=recv_sems.at[i, q], device_id=(x, y, 1 - c), device_id_type=pl.DeviceIdType.MESH)
            for q in range(4) for i in range(n)]
        for cp in remote:
            cp.start()
        for cp in remote:
            cp.wait_recv()
        for cp in remote:
            cp.wait_send()

    half = [jax.ShapeDtypeStruct((4,) + a.shape[1:], a.dtype) for a in arrs]
    return _comm_call(body, arrs, half, [(n, 4), (n, 4)], name)


def _swap_chips(arrs, *, name):
    n = len(arrs)

    def body(*refs):
        ins, outs = refs[:n], refs[n:2 * n]
        send_sems, recv_sems, local_sems = refs[2 * n:]
        x, y, c = (lax.axis_index(a) for a in AXES)
        mine = 2 * x + y

        def peer(k):
            px = 1 - x if k & 2 else x
            py = 1 - y if k & 1 else y
            return (px, py, c), 2 * px + py

        def remote(i, k, src, dst):
            return pltpu.make_async_remote_copy(
                src_ref=src, dst_ref=dst, send_sem=send_sems.at[i, k - 1], recv_sem=recv_sems.at[i, k - 1],
                device_id=peer(k)[0], device_id_type=pl.DeviceIdType.MESH)

        own = [pltpu.make_async_copy(ins[i].at[mine], outs[i].at[mine], local_sems.at[i]) for i in range(n)]
        sent = [remote(i, k, ins[i].at[peer(k)[1]], outs[i].at[mine]) for k in range(1, 4) for i in range(n)]
        for cp in own + sent:
            cp.start()
        for k in range(1, 4):
            for i in range(n):
                landing = outs[i].at[peer(k)[1]]
                remote(i, k, landing, landing).wait_recv()
        for cp in sent:
            cp.wait_send()
        for cp in own:
            cp.wait()

    out_shape = [jax.ShapeDtypeStruct(a.shape, a.dtype) for a in arrs]
    return _comm_call(body, arrs, out_shape, [(n, 3), (n, 3), (n,)], name)


def _add_core_slabs(full, got, core, *, name):
    C = got.shape[-1]
    R = got[0].size // C
    tr = _tile(R, max(16, (1 << 19) // C), 16)

    def body(core_ref, a_ref, b_ref, o_ref):
        o_ref[...] = (a_ref[...].astype(F32) + b_ref[...].astype(F32)).astype(o_ref.dtype)

    blk = pl.BlockSpec((None, tr, C), lambda q, i, core_ref: (q, i, 0))
    grid_spec = pltpu.PrefetchScalarGridSpec(
        num_scalar_prefetch=1, grid=(4, R // tr),
        in_specs=[pl.BlockSpec((None, None, tr, C), lambda q, i, core_ref: (q, core_ref[0], i, 0)), blk], out_specs=blk)
    return pl.pallas_call(
        body, name=name, grid_spec=grid_spec, out_shape=jax.ShapeDtypeStruct((4, R, C), got.dtype),
        compiler_params=_params("parallel", "parallel"),
    )(core, full.reshape(4, 2, R, C), got.reshape(4, R, C)).reshape(got.shape)


def _mm(a, b, *, name, nt=False, bias=None, add=None, out_dtype=F32, shard_cols=0):
    M, K = a.shape
    N = b.shape[0] if nt else b.shape[1]
    tm, tn = _tile(M, 1024, 16), shard_cols or _tile(N, 1024, LANE)

    def vmem_bytes(tk):
        blocks = 2 * (tm * tk + tk * tn) + tm * tn * jnp.dtype(out_dtype).itemsize + (4 * tm * tn if add is not None else 0)
        return 2 * blocks + (4 * tm * tn if tk < K else 0)

    tk = next((t for t in (_tile(K, 3072, LANE), _tile(K, 2048, LANE)) if vmem_bytes(t) <= VMEM_LIMIT - (12 << 20)),
              _tile(K, 1024, LANE))
    nk = K // tk
    dims = (((1,), (1,)), ((), ())) if nt else (((1,), (0,)), ((), ()))

    def body(*refs):
        refs = list(refs)
        a_ref, b_ref = refs.pop(0), refs.pop(0)
        bias_ref = refs.pop(0) if bias is not None else None
        add_ref = refs.pop(0) if add is not None else None
        o_ref = refs.pop(0)
        part = lax.dot_general(a_ref[...], b_ref[...], dims, preferred_element_type=F32)

        def finish(val):
            if bias_ref is not None:
                val = val + bias_ref[...]
            if add_ref is not None:
                val = val + add_ref[...]
            o_ref[...] = val.astype(o_ref.dtype)

        if nk == 1:
            finish(part)
        else:
            acc_ref = refs.pop(0)
            k = pl.program_id(2)

            @pl.when(k == 0)
            def _():
                acc_ref[...] = part

            @pl.when(k > 0)
            def _():
                acc_ref[...] += part

            @pl.when(k == nk - 1)
            def _():
                finish(acc_ref[...])

    in_specs = [pl.BlockSpec((tm, tk), lambda i, j, k: (i, k)),
                pl.BlockSpec((tn, tk), lambda i, j, k: (j, k)) if nt else pl.BlockSpec((tk, tn), lambda i, j, k: (k, j))]
    operands = [a, b]
    if bias is not None:
        in_specs.append(pl.BlockSpec((1, tn), lambda i, j, k: (0, j)))
        operands.append(bias)
    if add is not None:
        in_specs.append(pl.BlockSpec((tm, tn), lambda i, j, k: (i, j)))
        operands.append(add)
    if shard_cols:
        out_shape = jax.ShapeDtypeStruct((N // tn, M, tn), out_dtype)
        out_spec = pl.BlockSpec((None, tm, tn), lambda i, j, k: (j, i, 0))
    else:
        out_shape = jax.ShapeDtypeStruct((M, N), out_dtype)
        out_spec = pl.BlockSpec((tm, tn), lambda i, j, k: (i, j))
    return pl.pallas_call(
        body, name=name, grid=(M // tm, N // tn, nk), in_specs=in_specs, out_specs=out_spec, out_shape=out_shape,
        scratch_shapes=[pltpu.VMEM((tm, tn), F32)] if nk > 1 else [],
        compiler_params=_params("parallel", "parallel", "arbitrary"),
    )(*operands)


def _transpose(x, *, name):
    T, C = x.shape
    tt, tc = _tile(T, 512, LANE), _tile(C, 512, LANE)

    def body(x_ref, o_ref):
        o_ref[...] = x_ref[...].astype(F32).T.astype(BF16)

    return pl.pallas_call(
        body, name=name, grid=(T // tt, C // tc), in_specs=[pl.BlockSpec((tt, tc), lambda i, j: (i, j))],
        out_specs=pl.BlockSpec((tc, tt), lambda i, j: (j, i)), out_shape=jax.ShapeDtypeStruct((C, T), BF16),
        compiler_params=_params("parallel", "parallel"),
    )(x)


def _group_mm(a, w, *, name, nt=False, bias=None, scale=None):
    T, D = a.shape
    G, gd, _ = w.shape
    tm = _tile(T, 1024, 16)
    dims = (((1,), (1,)), ((), ())) if nt else (((1,), (0,)), ((), ()))
    fused = bias is not None

    def body(*refs):
        if fused:
            a_ref, w_ref, b_ref, s_ref, mb_ref, y_ref = refs
        else:
            a_ref, w_ref, y_ref = refs
        val = lax.dot_general(a_ref[...], w_ref[...], dims, preferred_element_type=F32)
        if fused:
            val = val + b_ref[...]
            mb_ref[...] = val
            val = val * s_ref[...]
        y_ref[...] = val

    blk = pl.BlockSpec((tm, gd), lambda i, g: (i, g))
    vec = pl.BlockSpec((1, gd), lambda i, g: (0, g))
    in_specs = [blk, pl.BlockSpec((None, gd, gd), lambda i, g: (g, 0, 0))] + ([vec, vec] if fused else [])
    td = jax.ShapeDtypeStruct((T, D), F32)
    return pl.pallas_call(
        body, name=name, grid=(T // tm, G), in_specs=in_specs, out_specs=[blk, blk] if fused else blk,
        out_shape=[td, td] if fused else td, compiler_params=_params("parallel", "parallel"),
    )(*([a, w, bias, scale] if fused else [a, w]))


def _group_dw(at, b, groups, *, name):
    D, T = at.shape
    gd = D // groups
    tk = _tile(T, 2048, LANE)

    def body(a_ref, b_ref, o_ref):
        @pl.when(pl.program_id(1) == 0)
        def _():
            o_ref[...] = jnp.zeros_like(o_ref)

        o_ref[...] += jnp.dot(a_ref[...], b_ref[...], preferred_element_type=F32)

    return pl.pallas_call(
        body, name=name, grid=(groups, T // tk),
        in_specs=[pl.BlockSpec((gd, tk), lambda g, k: (g, k)), pl.BlockSpec((tk, gd), lambda g, k: (k, g))],
        out_specs=pl.BlockSpec((None, gd, gd), lambda g, k: (g, 0, 0)),
        out_shape=jax.ShapeDtypeStruct((groups, gd, gd), F32), compiler_params=_params("parallel", "arbitrary"),
    )(at, b)


def _mod_fwd(c_all, w, b, *, name):
    L, D, n = w.shape
    B = c_all.shape[0]

    def body(c_ref, w_ref, b_ref, o_ref):
        cv = c_ref[...]
        o_ref[...] = jnp.dot(cv * _sigmoid(cv), w_ref[...], preferred_element_type=F32) + b_ref[...]

    return pl.pallas_call(
        body, name=name, grid=(L,),
        in_specs=[pl.BlockSpec((B, D), lambda l: (0, 0)), pl.BlockSpec((None, D, n), lambda l: (l, 0, 0)),
                  pl.BlockSpec((None, 1, n), lambda l: (l, 0, 0))],
        out_specs=pl.BlockSpec((None, B, n), lambda l: (l, 0, 0)), out_shape=jax.ShapeDtypeStruct((L, B, n), F32),
        compiler_params=_params("parallel"),
    )(c_all, w, b.reshape(L, 1, n))


def _mod_bwd(c_all_t, dmod, *, name):
    L, B, n = dmod.shape
    D = c_all_t.shape[0]

    def body(c_ref, d_ref, o_ref):
        cv = c_ref[...]
        o_ref[...] = jnp.dot(cv * _sigmoid(cv), d_ref[...], preferred_element_type=F32)

    return pl.pallas_call(
        body, name=name, grid=(L,),
        in_specs=[pl.BlockSpec((D, B), lambda l: (0, 0)), pl.BlockSpec((None, B, n), lambda l: (l, 0, 0))],
        out_specs=pl.BlockSpec((None, D, n), lambda l: (l, 0, 0)), out_shape=jax.ShapeDtypeStruct((L, D, n), F32),
        compiler_params=_params("parallel"),
    )(c_all_t, dmod)


def _rowwise(fn, tiles, vecs, out_dtypes, n_sums, *, name):
    T, C = tiles[0].shape
    tr = _tile(T, 512, 16)
    ch = _tile(tr, 32, 16)
    nt_, nv, no = len(tiles), len(vecs), len(out_dtypes)

    def body(*refs):
        t_refs, v_refs = refs[:nt_], refs[nt_:nt_ + nv]
        o_refs, s_refs = refs[nt_ + nv:nt_ + nv + no], refs[nt_ + nv + no:]
        vv = [v[...] for v in v_refs]
        acc = [None] * n_sums
        for r0 in range(0, tr, ch):
            rows = pl.ds(r0, ch)
            o_vals, s_vals = fn([t[rows, :] for t in t_refs], vv)
            for o, val in zip(o_refs, o_vals):
                o[rows, :] = val.astype(o.dtype)
            for i, val in enumerate(s_vals):
                part = jnp.sum(val, axis=0, keepdims=True)
                acc[i] = part if acc[i] is None else acc[i] + part
        if n_sums:
            first = pl.program_id(0) == 0

            @pl.when(first)
            def _():
                for s, val in zip(s_refs, acc):
                    s[...] = val

            @pl.when(jnp.logical_not(first))
            def _():
                for s, val in zip(s_refs, acc):
                    s[...] += val

    tile_spec = pl.BlockSpec((tr, C), lambda i: (i, 0))
    vec_spec = pl.BlockSpec((1, C), lambda i: (0, 0))
    res = pl.pallas_call(
        body, name=name, grid=(T // tr,), in_specs=[tile_spec] * nt_ + [vec_spec] * nv,
        out_specs=[tile_spec] * no + [vec_spec] * n_sums,
        out_shape=[jax.ShapeDtypeStruct((T, C), d) for d in out_dtypes] + [jax.ShapeDtypeStruct((1, C), F32)] * n_sums,
        compiler_params=_params("arbitrary"),
    )(*tiles, *vecs)
    return res


def _rms(v):
    return lax.rsqrt(jnp.mean(v * v, axis=-1, keepdims=True) + RMS_EPS)


def _prenorm_fwd(x, g, sc, sh, out_dtype, *, name):
    def fn(t, v):
        (xv,), (g_, sc_, sh_) = t, v
        return [(xv * _rms(xv)) * g_ * (1.0 + sc_) + sh_], []

    return _rowwise(fn, [x], [g, sc, sh], [out_dtype], 0, name=name)[0]


def _post_pre_fwd(x, y, gp, gate, g, sc, sh, out_dtype, *, name):
    def fn(t, v):
        (xv, yv), (gp_, gate_, g_, sc_, sh_) = t, v
        x1 = xv + gate_ * ((yv * _rms(yv)) * gp_)
        return [x1, (x1 * _rms(x1)) * g_ * (1.0 + sc_) + sh_], []

    return _rowwise(fn, [x, y], [gp, gate, g, sc, sh], [F32, out_dtype], 0, name=name)


def _post_loss(x, y, gp, gate, target, *, name):
    D = x.shape[1]

    def fn(t, v):
        (xv, yv, tv), (gp_, gate_) = t, v
        err = xv + gate_ * ((yv * _rms(yv)) * gp_) - tv
        return [err / D], [err * err]

    return _rowwise(fn, [x, y, target], [gp, gate], [F32], 1, name=name)


def _post_bwd(y, dxo, gp, gate, out_dtype, *, name):
    def fn(t, v):
        (yv, dv), (gp_, gate_) = t, v
        r = _rms(yv)
        yn = yv * r
        dyn = dv * gate_ * gp_
        dy = r * (dyn - yn * jnp.mean(dyn * yn, axis=-1, keepdims=True))
        return [dy], [dv * gate_ * yn, dv * yn * gp_, dy]

    return _rowwise(fn, [y, dxo], [gp, gate], [out_dtype], 3, name=name)


def _pre_bwd(x, dh, dres, g, sc, *, name):
    def fn(t, v):
        (xv, dhv, dr), (g_, sc_) = t, v
        dhv = dhv.astype(F32)
        r = _rms(xv)
        xn = xv * r
        dxn = dhv * g_ * (1.0 + sc_)
        dx = dr + r * (dxn - xn * jnp.mean(dxn * xn, axis=-1, keepdims=True))
        return [dx], [dhv, dhv * xn * g_, dhv * xn * (1.0 + sc_)]

    return _rowwise(fn, [x, dh, dres], [g, sc], [F32], 3, name=name)


def _a_ln_bwd(z, dq, g, b, *, name):
    def fn(t, v):
        (zv, dqv), (g_, b_) = t, v
        mu = jnp.mean(zv, axis=-1, keepdims=True)
        zc = zv - mu
        rstd = lax.rsqrt(jnp.mean(zc * zc, axis=-1, keepdims=True) + LN_EPS)
        zn = zc * rstd
        l = zn * g_ + b_
        s = _sigmoid(l)
        dl = dqv * (s * (1.0 + l * (1.0 - s)))
        dzn = dl * g_
        dz = rstd * (dzn - jnp.mean(dzn, axis=-1, keepdims=True) - zn * jnp.mean(dzn * zn, axis=-1, keepdims=True))
        return [dz], [dl * zn, dl, dz]

    return _rowwise(fn, [z, dq], [g, b], [F32], 3, name=name)


def _a_glu_bwd(ua, ug, dp, *, name):
    def fn(t, v):
        uav, ugv, dpv = t
        s = _sigmoid(ugv)
        dua = dpv * s
        dug = dpv * uav * s * (1.0 - s)
        return [dua, dug], [dua, dug]

    return _rowwise(fn, [ua, ug, dp], [], [BF16, BF16], 2, name=name)


def _b_scale_bwd(dy, mb, scale, *, name):
    def fn(t, v):
        (dyv, mbv), (s_,) = t, v
        dm = dyv * s_
        return [dm], [dyv * mbv, dm]

    return _rowwise(fn, [dy, mb], [scale], [BF16], 2, name=name)


def _c_gate_bwd1(gb, cv, ds, *, name):
    def fn(t, v):
        gbv, cvv, dsv = t
        return [dsv * cvv, dsv * gbv], []

    return _rowwise(fn, [gb, cv, ds], [], [BF16, F32], 0, name=name)


def _c_gate_bwd2(gc, v_, dp, *, name):
    def fn(t, v):
        gcv, vv, dpv = t
        return [dpv * vv, dpv * gcv], []

    return _rowwise(fn, [gc, v_, dp], [], [BF16, BF16], 0, name=name)


def _stage_causal(buf, t, halo, tt):
    @pl.when(t == 0)
    def _():
        buf[pl.ds(0, halo), :] = jnp.zeros((halo, buf.shape[1]), F32)

    @pl.when(t > 0)
    def _():
        buf[pl.ds(0, halo), :] = buf[pl.ds(tt, halo), :]


def _stage_anticausal(buf, t, halo, tt):
    @pl.when(t == 0)
    def _():
        buf[pl.ds(tt, halo), :] = jnp.zeros((halo, buf.shape[1]), F32)

    @pl.when(t > 0)
    def _():
        buf[pl.ds(tt, halo), :] = buf[pl.ds(0, halo), :]


SUBLANES = 8


def _make_shifted(buf, shifted):
    n = shifted.shape[1]
    for b in range(1, SUBLANES):
        shifted[b - 1, :, :] = buf[pl.ds(b, n), :]


def _rows_at(buf, shifted, offset, rows, cols):
    b = offset % SUBLANES
    if shifted is None or b == 0:
        return buf[pl.ds(offset, rows), cols]
    return shifted[b - 1, pl.ds(offset - b, rows), cols]


def _causal_taps(buf, w_ref, taps, halo, r0, rows, cols, shifted=None):
    acc = None
    for k in range(taps):
        term = w_ref[pl.ds(k, 1), cols] * _rows_at(buf, shifted, halo - (taps - 1) + k + r0, rows, cols)
        acc = term if acc is None else acc + term
    return acc


def _halo_tiles(T, C, halo, col_cap):
    tt = _tile(T, 512 if halo <= 8 else 256, 16)
    assert tt >= halo, (tt, halo)
    return tt, _tile(C, col_cap, LANE), _tile(tt, 128, 16)


def _col_blocks(tc):
    cb = LANE if tc % LANE == 0 else tc
    return [pl.ds(c0, cb) for c0 in range(0, tc, cb)]


def _ffn_act_fwd(ug, uv, wg, wv, bg, bv, *, name):
    T, F = ug.shape
    taps, halo = wg.shape[0], 8
    tt, tc, rc = _halo_tiles(T, F, halo, 1024)

    def body(ug_ref, uv_ref, wg_ref, wv_ref, bg_ref, bv_ref, a_ref, yg_ref, yv_ref, bufg, bufv):
        t = pl.program_id(1)
        for u_ref, buf in ((ug_ref, bufg), (uv_ref, bufv)):
            _stage_causal(buf, t, halo, tt)
            buf[pl.ds(halo, tt), :] = u_ref[...]
        for cols in _col_blocks(tc):
            for r0 in range(0, tt, rc):
                rows = pl.ds(r0, rc)
                yg = _causal_taps(bufg, wg_ref, taps, halo, r0, rc, cols) + bg_ref[:, cols]
                yv = _causal_taps(bufv, wv_ref, taps, halo, r0, rc, cols) + bv_ref[:, cols]
                a_ref[rows, cols] = (yg * _sigmoid(yg) * yv).astype(BF16)
                yg_ref[rows, cols] = yg.astype(BF16)
                yv_ref[rows, cols] = yv.astype(BF16)

    blk = pl.BlockSpec((tt, tc), lambda j, t: (t, j))
    wspec = pl.BlockSpec((taps, tc), lambda j, t: (0, j))
    vspec = pl.BlockSpec((1, tc), lambda j, t: (0, j))
    return pl.pallas_call(
        body, name=name, grid=(F // tc, T // tt), in_specs=[blk, blk, wspec, wspec, vspec, vspec],
        out_specs=[blk, blk, blk], out_shape=[jax.ShapeDtypeStruct((T, F), BF16)] * 3,
        scratch_shapes=[pltpu.VMEM((halo + tt, tc), F32), pltpu.VMEM((halo + tt, tc), F32)],
        compiler_params=_params("parallel", "arbitrary"),
    )(ug, uv, wg, wv, bg, bv)


def _ffn_act_bwd(yg, yv, da, *, name):
    T, F = yg.shape
    tt, tc = _tile(T, 512, 16), _tile(F, 1024, LANE)
    rc = _tile(tt, 128, 16)

    def body(yg_ref, yv_ref, da_ref, dg_ref, dv_ref, sg_ref, sv_ref):
        @pl.when(pl.program_id(1) == 0)
        def _():
            sg_ref[...] = jnp.zeros_like(sg_ref)
            sv_ref[...] = jnp.zeros_like(sv_ref)

        for cols in _col_blocks(tc):
            sum_g = sum_v = None
            for r0 in range(0, tt, rc):
                rows = pl.ds(r0, rc)
                g, v = yg_ref[rows, cols].astype(F32), yv_ref[rows, cols].astype(F32)
                s = _sigmoid(g)
                d = da_ref[rows, cols].astype(F32)
                dyg = d * v * (s * (1.0 + g * (1.0 - s)))
                dyv = d * (g * s)
                dg_ref[rows, cols] = dyg.astype(dg_ref.dtype)
                dv_ref[rows, cols] = dyv.astype(dv_ref.dtype)
                part_g, part_v = jnp.sum(dyg, axis=0, keepdims=True), jnp.sum(dyv, axis=0, keepdims=True)
                sum_g = part_g if sum_g is None else sum_g + part_g
                sum_v = part_v if sum_v is None else sum_v + part_v
            sg_ref[:, cols] += sum_g
            sv_ref[:, cols] += sum_v

    blk = pl.BlockSpec((tt, tc), lambda j, t: (t, j))
    vspec = pl.BlockSpec((1, tc), lambda j, t: (0, j))
    td, vd = jax.ShapeDtypeStruct((T, F), BF16), jax.ShapeDtypeStruct((1, F), F32)
    return pl.pallas_call(
        body, name=name, grid=(F // tc, T // tt), in_specs=[blk, blk, blk], out_specs=[blk, blk, vspec, vspec],
        out_shape=[td, td, vd, vd], compiler_params=_params("parallel", "arbitrary"),
    )(yg, yv, da)


def _dwconv_bwd(x, dy, w, out_dtype, *, name):
    T, C = x.shape
    taps = w.shape[0]
    halo = _round_up(taps - 1, 8)
    tt, tc, rc = _halo_tiles(T, C, halo, 512)
    nt_ = T // tt
    long_conv = taps > SUBLANES

    def body(x_ref, dy_ref, w_ref, dx_ref, dw_ref, buf, *scratch):
        t = pl.program_id(1)
        shifted = scratch[0] if long_conv else None

        @pl.when(t == 0)
        def _():
            dw_ref[...] = jnp.zeros_like(dw_ref)

        _stage_anticausal(buf, t, halo, tt)
        buf[pl.ds(0, tt), :] = dy_ref[...].astype(F32)
        if long_conv:
            _make_shifted(buf, shifted)
        for cols in _col_blocks(tc):
            for r0 in range(0, tt, rc):
                rows = pl.ds(r0, rc)
                xv = x_ref[rows, cols]
                acc = None
                for j in range(taps):
                    k = taps - 1 - j
                    ahead = _rows_at(buf, shifted, r0 + j, rc, cols)
                    term = w_ref[pl.ds(k, 1), cols] * ahead
                    acc = term if acc is None else acc + term
                    dw_ref[pl.ds(k, 1), cols] += jnp.sum(xv * ahead, axis=0, keepdims=True)
                dx_ref[rows, cols] = acc.astype(dx_ref.dtype)

    blk = pl.BlockSpec((tt, tc), lambda j, t: (nt_ - 1 - t, j))
    wspec = pl.BlockSpec((taps, tc), lambda j, t: (0, j))
    return pl.pallas_call(
        body, name=name, grid=(C // tc, nt_), in_specs=[blk, blk, wspec], out_specs=[blk, wspec],
        out_shape=[jax.ShapeDtypeStruct((T, C), out_dtype), jax.ShapeDtypeStruct((taps, C), F32)],
        scratch_shapes=[pltpu.VMEM((tt + halo, tc), F32)] + (
            [pltpu.VMEM((SUBLANES - 1, tt + halo - SUBLANES, tc), F32)] if long_conv else []),
        compiler_params=_params("parallel", "arbitrary"),
    )(x, dy, w)


def _a_conv_fwd(ua, ug, w, b, ln_g, ln_b, *, name):
    T, D = ua.shape
    taps = w.shape[0]
    halo = _round_up(taps - 1, 8)
    tt = _tile(T, 128, 16)
    assert tt >= halo
    rc, ch = _tile(tt, 128, 16), _tile(tt, 32, 16)

    def body(ua_ref, ug_ref, w_ref, b_ref, g_ref, lb_ref, p_ref, z_ref, q_ref, buf, shifted):
        t = pl.program_id(0)
        _stage_causal(buf, t, halo, tt)
        for r0 in range(0, tt, ch):
            rows = pl.ds(r0, ch)
            pv = ua_ref[rows, :] * _sigmoid(ug_ref[rows, :])
            p_ref[rows, :] = pv
            buf[pl.ds(halo + r0, ch), :] = pv
        _make_shifted(buf, shifted)
        for cols in _col_blocks(D):
            for r0 in range(0, tt, rc):
                z_ref[pl.ds(r0, rc), cols] = (
                    _causal_taps(buf, w_ref, taps, halo, r0, rc, cols, shifted) + b_ref[:, cols])
        for r0 in range(0, tt, ch):
            rows = pl.ds(r0, ch)
            zv = z_ref[rows, :]
            zc = zv - jnp.mean(zv, axis=-1, keepdims=True)
            l = zc * lax.rsqrt(jnp.mean(zc * zc, axis=-1, keepdims=True) + LN_EPS) * g_ref[...] + lb_ref[...]
            q_ref[rows, :] = (l * _sigmoid(l)).astype(BF16)

    blk = pl.BlockSpec((tt, D), lambda t: (t, 0))
    vec = pl.BlockSpec((1, D), lambda t: (0, 0))
    td = jax.ShapeDtypeStruct((T, D), F32)
    return pl.pallas_call(
        body, name=name, grid=(T // tt,),
        in_specs=[blk, blk, pl.BlockSpec((taps, D), lambda t: (0, 0)), vec, vec, vec], out_specs=[blk, blk, blk],
        out_shape=[td, td, jax.ShapeDtypeStruct((T, D), BF16)],
        scratch_shapes=[pltpu.VMEM((halo + tt, D), F32), pltpu.VMEM((SUBLANES - 1, halo + tt - SUBLANES, D), F32)],
        compiler_params=_params("arbitrary"),
    )(ua, ug, w, b, ln_g, ln_b)


def _c_gate_fwd(gb, gc, v, w, *, name):
    T, D = gb.shape
    taps, halo = w.shape[0], 8
    tt, tc, rc = _halo_tiles(T, D, halo, 1024)

    def body(gb_ref, gc_ref, v_ref, w_ref, p_ref, cv_ref, s_ref, buf):
        t = pl.program_id(1)
        _stage_causal(buf, t, halo, tt)
        pv = gc_ref[...] * v_ref[...]
        p_ref[...] = pv
        buf[pl.ds(halo, tt), :] = pv
        for cols in _col_blocks(tc):
            for r0 in range(0, tt, rc):
                rows = pl.ds(r0, rc)
                cv = _causal_taps(buf, w_ref, taps, halo, r0, rc, cols)
                cv_ref[rows, cols] = cv
                s_ref[rows, cols] = (gb_ref[rows, cols] * cv).astype(BF16)

    blk = pl.BlockSpec((tt, tc), lambda j, t: (t, j))
    td = jax.ShapeDtypeStruct((T, D), F32)
    return pl.pallas_call(
        body, name=name, grid=(D // tc, T // tt),
        in_specs=[blk, blk, blk, pl.BlockSpec((taps, tc), lambda j, t: (0, j))], out_specs=[blk, blk, blk],
        out_shape=[td, td, jax.ShapeDtypeStruct((T, D), BF16)],
        scratch_shapes=[pltpu.VMEM((halo + tt, tc), F32)], compiler_params=_params("parallel", "arbitrary"),
    )(gb, gc, v, w)


def _pool_count(base, r0, rows, width, window):
    pos = lax.broadcasted_iota(jnp.int32, (rows, width), 0) + (base + r0 + 1)
    return jnp.minimum(pos, window).astype(F32)


def _pool_fwd(h, *, name):
    T, D = h.shape
    halo = _round_up(max(POOL_WINDOWS), 8)
    gd = D // len(POOL_WINDOWS)
    tt = _tile(T, 256, 16)
    assert tt >= halo
    rc = _tile(tt, 128, 16)

    def body(h_ref, o_ref, buf):
        t = pl.program_id(0)
        _stage_causal(buf, t, halo, tt)
        buf[pl.ds(halo, tt), :] = h_ref[...]
        for g, window in enumerate(POOL_WINDOWS):
            for cols_in_group in _col_blocks(gd):
                cols = pl.ds(g * gd + cols_in_group.start, cols_in_group.size)
                for r0 in range(0, tt, rc):
                    acc = None
                    for j in range(window):
                        term = buf[pl.ds(halo - j + r0, rc), cols]
                        acc = term if acc is None else acc + term
                    cnt = _pool_count(t * tt, r0, rc, cols.size, window)
                    o_ref[pl.ds(r0, rc), cols] = (acc / cnt - buf[pl.ds(halo + r0, rc), cols]).astype(BF16)

    blk = pl.BlockSpec((tt, D), lambda t: (t, 0))
    return pl.pallas_call(
        body, name=name, grid=(T // tt,), in_specs=[blk], out_specs=blk, out_shape=jax.ShapeDtypeStruct((T, D), BF16),
        scratch_shapes=[pltpu.VMEM((halo + tt, D), F32)], compiler_params=_params("arbitrary"),
    )(h)


def _pool_bwd(dp, *, name):
    T, D = dp.shape
    halo = _round_up(max(POOL_WINDOWS), 8)
    gd = D // len(POOL_WINDOWS)
    tt = _tile(T, 256, 16)
    assert tt >= halo
    rc = _tile(tt, 128, 16)
    nt_ = T // tt

    def body(dp_ref, o_ref, buf):
        t = pl.program_id(0)
        base = (nt_ - 1 - t) * tt
        _stage_anticausal(buf, t, halo, tt)
        for g, window in enumerate(POOL_WINDOWS):
            for cols_in_group in _col_blocks(gd):
                cols = pl.ds(g * gd + cols_in_group.start, cols_in_group.size)
                for r0 in range(0, tt, rc):
                    rows = pl.ds(r0, rc)
                    buf[rows, cols] = dp_ref[rows, cols] / _pool_count(base, r0, rc, cols.size, window)
                for r0 in range(0, tt, rc):
                    acc = None
                    for j in range(window):
                        term = buf[pl.ds(r0 + j, rc), cols]
                        acc = term if acc is None else acc + term
                    o_ref[pl.ds(r0, rc), cols] = acc - dp_ref[pl.ds(r0, rc), cols]

    blk = pl.BlockSpec((tt, D), lambda t: (nt_ - 1 - t, 0))
    return pl.pallas_call(
        body, name=name, grid=(nt_,), in_specs=[blk], out_specs=blk, out_shape=jax.ShapeDtypeStruct((T, D), F32),
        scratch_shapes=[pltpu.VMEM((tt + halo, D), F32)], compiler_params=_params("arbitrary"),
    )(dp)


def _sum_slabs(slabs, *, name):
    S, R, C = slabs.shape

    def body(s_ref, o_ref):
        acc = s_ref[0]
        for p in range(1, S):
            acc = acc + s_ref[p]
        o_ref[...] = acc

    return pl.pallas_call(body, name=name, out_shape=jax.ShapeDtypeStruct((R, C), F32),
                          compiler_params=_params())(slabs)


def _adamw(w, slabs, m, v, *, name):
    S, R, C = slabs.shape
    sub = 16 if slabs.dtype == BF16 else 8
    tr = _tile(R, max(sub, (1 << 18) // (C * S)), sub)
    c1, c2 = 1.0 - ADAM_B1 ** ADAM_STEP, 1.0 - ADAM_B2 ** ADAM_STEP

    def body(w_ref, s_ref, m_ref, v_ref, g_ref, d_ref, nm_ref, nv_ref):
        g = s_ref[0].astype(F32)
        for p in range(1, S):
            g = g + s_ref[p].astype(F32)
        nm = ADAM_B1 * m_ref[...] + (1.0 - ADAM_B1) * g
        nv = ADAM_B2 * v_ref[...] + (1.0 - ADAM_B2) * (g * g)
        g_ref[...] = g
        nm_ref[...] = nm
        nv_ref[...] = nv
        d_ref[...] = -ADAM_LR * ((nm / c1) / (jnp.sqrt(nv / c2) + ADAM_EPS) + ADAM_WD * w_ref[...])

    blk = pl.BlockSpec((tr, C), lambda i: (i, 0))
    sd = jax.ShapeDtypeStruct((R, C), F32)
    return pl.pallas_call(
        body, name=name, grid=(R // tr,), in_specs=[blk, pl.BlockSpec((S, tr, C), lambda i: (0, i, 0)), blk, blk],
        out_specs=[blk] * 4, out_shape=[sd] * 4, compiler_params=_params("parallel"),
    )(w, slabs, m, v)


def _cols_natural(g):
    g = jnp.moveaxis(g, 0, -2)
    return g.reshape(g.shape[:-2] + (g.shape[-2] * g.shape[-1],))


def _rows_natural(g):
    g = jnp.moveaxis(g, 0, 1)
    return g.reshape((g.shape[0], g.shape[1] * g.shape[2]) + g.shape[3:])


def _col_slabs(full, n_shards=N_DEV):
    n = full.shape[-1] // n_shards
    return jnp.moveaxis(full.reshape(full.shape[:-1] + (n_shards, n)), -2, 0)


def _pad_last(a, n):
    return jnp.pad(a, [(0, 0)] * (a.ndim - 1) + [(0, n - a.shape[-1])])


def kernel(x, c, mod_w, mod_b, norm_pre_mix, norm_post_mix, norm_pre_ffn, norm_post_ffn, a_pw1_w, a_pw1_b, a_dw_w, a_dw_b, a_ln_g, a_ln_b, a_pw2_w, a_pw2_b, b_group_w, b_group_b, b_scale, c_in_w, c_conv_w, c_out_w, f_up_w, f_dw_w, f_dw_b, f_down_w, loss_target, m_mod_w, m_mod_b, m_norm_pre_mix, m_norm_post_mix, m_norm_pre_ffn, m_norm_post_ffn, m_a_pw1_w, m_a_pw1_b, m_a_dw_w, m_a_dw_b, m_a_ln_g, m_a_ln_b, m_a_pw2_w, m_a_pw2_b, m_b_group_w, m_b_group_b, m_b_scale, m_c_in_w, m_c_conv_w, m_c_out_w, m_f_up_w, m_f_dw_w, m_f_dw_b, m_f_down_w, v_mod_w, v_mod_b, v_norm_pre_mix, v_norm_post_mix, v_norm_pre_ffn, v_norm_post_ffn, v_a_pw1_w, v_a_pw1_b, v_a_dw_w, v_a_dw_b, v_a_ln_g, v_a_ln_b, v_a_pw2_w, v_a_pw2_b, v_b_group_w, v_b_group_b, v_b_scale, v_c_in_w, v_c_conv_w, v_c_out_w, v_f_up_w, v_f_dw_w, v_f_dw_b, v_f_down_w):
    p = dict(locals())
    assert list(p) == INPUTS
    x0 = p['x'][0]
    target = p['loss_target'][0]
    T, D = x0.shape
    L = p['mod_w'].shape[0]
    G = len(POOL_WINDOWS)
    gd = D // G
    ns = p['f_up_w'].shape[-1]
    npad = _round_up(ns, LANE)
    Fp = 4 * npad
    me = 4 * lax.axis_index("x") + 2 * lax.axis_index("y") + lax.axis_index("c")
    row = lambda a, i: a[i:i + 1]

    big = ['a_pw1_w', 'a_pw2_w', 'b_group_w', 'c_in_w', 'c_out_w', 'f_down_w']
    small = ['a_pw1_b', 'a_dw_w', 'a_dw_b', 'a_ln_g', 'a_ln_b', 'a_pw2_b', 'c_conv_w']
    send = [p[k].astype(BF16) for k in big] + [_pad_last(p['f_up_w'], npad).astype(BF16)]
    send += [p[k] for k in small] + [_pad_last(p['f_dw_w'], npad), p['c']]
    got = _gather_two_level(send, name="gather_weights")
    gw = dict(zip(big + ['f_up_w'] + small + ['f_dw_w', 'c'], got))

    w1 = _cols_natural(gw['a_pw1_w'])
    b1 = _cols_natural(gw['a_pw1_b'])
    a_dw_w = _cols_natural(gw['a_dw_w'])
    a_dw_b, a_ln_g, a_ln_b, b2 = (_cols_natural(gw[k]) for k in ('a_dw_b', 'a_ln_g', 'a_ln_b', 'a_pw2_b'))
    w2 = _rows_natural(gw['a_pw2_w'])
    bgw = jnp.moveaxis(gw['b_group_w'], 0, 2).reshape((-1, G, gd, gd))
    cin = _cols_natural(gw['c_in_w'])
    cconv = _cols_natural(gw['c_conv_w'])
    cout = _rows_natural(gw['c_out_w'])
    wup_g, wup_v = _cols_natural(gw['f_up_w'][:4]), _cols_natural(gw['f_up_w'][4:])
    fdw_g, fdw_v = _cols_natural(gw['f_dw_w'][:4]), _cols_natural(gw['f_dw_w'][4:])
    pad_ff = lambda a: _pad_last(a.reshape(a.shape[:-1] + (4, ns)), npad).reshape(a.shape[:-1] + (Fp,))
    fdb_g, fdb_v = pad_ff(p['f_dw_b'][:, :4 * ns]), pad_ff(p['f_dw_b'][:, 4 * ns:])
    wdown = _rows_natural(gw['f_down_w'])
    wdown = jnp.pad(wdown.reshape(L, 4, ns, D), ((0, 0), (0, 0), (0, npad - ns), (0, 0))).reshape(L, Fp, D)

    c_all = gw['c'].reshape(N_DEV, D)
    n6 = p['mod_w'].shape[-1]
    mod_part = _mod_fwd(c_all, p['mod_w'], lax.dynamic_slice_in_dim(p['mod_b'], me * n6, n6, axis=1), name="mod_fwd")
    mod_all = _exchange([mod_part], scatter=False, name="gather_mod")[0]
    mod_all = jnp.moveaxis(mod_all, 0, 2).reshape(L, N_DEV, N_DEV * n6)
    mod = lax.dynamic_index_in_dim(mod_all, me, axis=1, keepdims=False).reshape(L, 6, D)

    saved = []
    x = x0
    h = _prenorm_fwd(x, row(p['norm_pre_mix'], 0), row(mod[0], 1), row(mod[0], 0), BF16, name="prenorm_first")
    dx = loss_cols = None
    for l in range(L):
        kind, slot = l % N_MIXERS, l // N_MIXERS
        sh_m, sc_m, gt_m, sh_f, sc_f, gt_f = (row(mod[l], i) for i in range(6))
        s = dict(x=x, h=h)
        if kind == 0:
            ua = _mm(h, w1[slot, :, :D], bias=b1[slot:slot + 1, :D], name=f"a_pw1_a_{l}")
            ug = _mm(h, w1[slot, :, D:], bias=b1[slot:slot + 1, D:], name=f"a_pw1_g_{l}")
            pa, z, q = _a_conv_fwd(ua, ug, a_dw_w[slot], a_dw_b[slot:slot + 1], a_ln_g[slot:slot + 1],
                                   a_ln_b[slot:slot + 1], name=f"a_conv_fwd_{l}")
            y = _mm(q, w2[slot], bias=b2[slot:slot + 1], name=f"a_pw2_{l}")
            s.update(ua=ua, ug=ug, p=pa, z=z, q=q)
        elif kind == 1:
            pooled = _pool_fwd(h, name=f"pool_fwd_{l}")
            mb, y = _group_mm(pooled, bgw[slot], bias=p['b_group_b'][slot:slot + 1],
                              scale=p['b_scale'][slot:slot + 1], name=f"b_mix_{l}")
            s.update(pooled=pooled, mb=mb)
        else:
            gb, gc, v = (_mm(h, cin[slot, :, i * D:(i + 1) * D], name=f"c_in_{i}_{l}") for i in range(3))
            pc, cv, sg = _c_gate_fwd(gb, gc, v, cconv[slot], name=f"c_gate_fwd_{l}")
            y = _mm(sg, cout[slot], name=f"c_out_{l}")
            s.update(gb=gb, gc=gc, v=v, p=pc, cv=cv, sg=sg)
        x1, h2 = _post_pre_fwd(x, y, row(p['norm_post_mix'], l), gt_m, row(p['norm_pre_ffn'], l), sc_f, sh_f, BF16,
                               name=f"post_mix_{l}")
        fug = _mm(h2, wup_g[l], name=f"f_up_g_{l}")
        fuv = _mm(h2, wup_v[l], name=f"f_up_v_{l}")
        act, fyg, fyv = _ffn_act_fwd(fug, fuv, fdw_g[l], fdw_v[l], fdb_g[l:l + 1], fdb_v[l:l + 1],
                                     name=f"ffn_act_fwd_{l}")
        y2 = _mm(act, wdown[l], name=f"f_down_{l}")
        s.update(y=y, x1=x1, h2=h2, fug=fug, fuv=fuv, fyg=fyg, fyv=fyv, act=act, y2=y2)
        saved.append(s)
        if l + 1 < L:
            nxt_dtype = F32 if (l + 1) % N_MIXERS == 1 else BF16
            x, h = _post_pre_fwd(x1, y2, row(p['norm_post_ffn'], l), gt_f, row(p['norm_pre_mix'], l + 1),
                                 row(mod[l + 1], 1), row(mod[l + 1], 0), nxt_dtype, name=f"post_ffn_{l}")
        else:
            dx, loss_cols = _post_loss(x1, y2, row(p['norm_post_ffn'], l), gt_f, target, name="post_loss")
    loss = lax.psum(0.5 * jnp.sum(loss_cols) / D, AXES)

    gfull = {}
    for k in ('a_pw1_w', 'a_pw1_b', 'a_dw_w', 'a_dw_b', 'a_ln_g', 'a_ln_b', 'a_pw2_w', 'a_pw2_b', 'b_group_w',
              'c_in_w', 'c_conv_w', 'c_out_w', 'f_up_w', 'f_dw_w', 'f_down_w'):
        gfull[k] = {}
    dmod, dnorm, d_fdb = {}, {}, {}
    d_bgb = d_bscale = None
    for l in reversed(range(L)):
        kind, slot = l % N_MIXERS, l // N_MIXERS
        s = saved[l]
        sh_m, sc_m, gt_m, sh_f, sc_f, gt_f = (row(mod[l], i) for i in range(6))
        dy2, dg4, dgt_f, _ = _post_bwd(s['y2'], dx, row(p['norm_post_ffn'], l), gt_f, BF16, name=f"post_ffn_bwd_{l}")
        da = _mm(dy2, wdown[l], nt=True, out_dtype=BF16, name=f"f_down_dx_{l}")
        dwd = _mm(_transpose(dy2, name=f"dy2_t_{l}"), s['act'], out_dtype=BF16, name=f"f_down_dw_{l}").T
        gfull['f_down_w'][l] = dwd.reshape(4, npad, D)[:, :ns].reshape(N_DEV, ns // 2, D)
        dcg, dcv, dbg, dbv = _ffn_act_bwd(s['fyg'], s['fyv'], da, name=f"ffn_act_bwd_{l}")
        dug, dwg = _dwconv_bwd(s['fug'], dcg, fdw_g[l], BF16, name=f"f_dw_bwd_g_{l}")
        duv, dwv = _dwconv_bwd(s['fuv'], dcv, fdw_v[l], BF16, name=f"f_dw_bwd_v_{l}")
        dh2 = _mm(dug, wup_g[l], nt=True, name=f"f_up_dx_g_{l}")
        dh2 = _mm(duv, wup_v[l], nt=True, add=dh2, name=f"f_up_dx_v_{l}")
        h2t = _transpose(s['h2'], name=f"h2_t_{l}")
        gfull['f_up_w'][l] = jnp.concatenate(
            [_mm(h2t, dug, shard_cols=npad, out_dtype=BF16, name=f"f_up_dw_g_{l}"),
             _mm(h2t, duv, shard_cols=npad, out_dtype=BF16, name=f"f_up_dw_v_{l}")], axis=0)
        gfull['f_dw_w'][l] = jnp.concatenate([_col_slabs(dwg, 4), _col_slabs(dwv, 4)], axis=0)
        unpad = lambda a: a.reshape(4, npad)[:, :ns].reshape(4 * ns)
        d_fdb[l] = jnp.concatenate([unpad(dbg), unpad(dbv)])
        dx1, dsh_f, dsc_f, dg3 = _pre_bwd(s['x1'], dh2, dx, row(p['norm_pre_ffn'], l), sc_f, name=f"pre_ffn_bwd_{l}")
        dy_dtype = F32 if kind == 1 else BF16
        dy, dg2, dgt_m, dy_sum = _post_bwd(s['y'], dx1, row(p['norm_post_mix'], l), gt_m, dy_dtype,
                                           name=f"post_mix_bwd_{l}")
        if kind == 0:
            dq = _mm(dy, w2[slot], nt=True, name=f"a_pw2_dx_{l}")
            gfull['a_pw2_w'][slot] = _mm(_transpose(s['q'], name=f"q_t_{l}"), dy, out_dtype=BF16,
                                         name=f"a_pw2_dw_{l}").reshape(N_DEV, D // N_DEV, D)
            gfull['a_pw2_b'][slot] = dy_sum.reshape(N_DEV, D // N_DEV)
            dz, dlg, dlb, dzs = _a_ln_bwd(s['z'], dq, a_ln_g[slot:slot + 1], a_ln_b[slot:slot + 1], name=f"a_ln_bwd_{l}")
            gfull['a_ln_g'][slot], gfull['a_ln_b'][slot], gfull['a_dw_b'][slot] = (
                t.reshape(N_DEV, D // N_DEV) for t in (dlg, dlb, dzs))
            dp, ddw = _dwconv_bwd(s['p'], dz, a_dw_w[slot], F32, name=f"a_dw_bwd_{l}")
            gfull['a_dw_w'][slot] = _col_slabs(ddw)
            dua, dug_a, sua, sug = _a_glu_bwd(s['ua'], s['ug'], dp, name=f"a_glu_bwd_{l}")
            gfull['a_pw1_b'][slot] = jnp.concatenate([sua, sug], axis=1).reshape(N_DEV, 2 * D // N_DEV)
            dh = _mm(dua, w1[slot, :, :D], nt=True, name=f"a_pw1_dx_a_{l}")
            dh = _mm(dug_a, w1[slot, :, D:], nt=True, add=dh, name=f"a_pw1_dx_g_{l}")
            ht = _transpose(s['h'], name=f"h_t_{l}")
            n1 = 2 * D // N_DEV
            gfull['a_pw1_w'][slot] = jnp.concatenate(
                [_mm(ht, dua, shard_cols=n1, out_dtype=BF16, name=f"a_pw1_dw_a_{l}"),
                 _mm(ht, dug_a, shard_cols=n1, out_dtype=BF16, name=f"a_pw1_dw_g_{l}")], axis=0)
        elif kind == 1:
            dm, d_bscale, d_bgb = _b_scale_bwd(dy, s['mb'], p['b_scale'][slot:slot + 1], name=f"b_scale_bwd_{l}")
            dpool = _group_mm(dm, bgw[slot], nt=True, name=f"b_mix_dx_{l}")
            dgw = _group_dw(_transpose(s['pooled'], name=f"pooled_t_{l}"), dm, G, name=f"b_mix_dw_{l}")
            gfull['b_group_w'][slot] = jnp.moveaxis(dgw.astype(BF16).reshape(G, N_DEV, gd // N_DEV, gd), 1, 0)
            dh = _pool_bwd(dpool, name=f"pool_bwd_{l}")
        else:
            dsg = _mm(dy, cout[slot], nt=True, name=f"c_out_dx_{l}")
            gfull['c_out_w'][slot] = _mm(_transpose(s['sg'], name=f"sg_t_{l}"), dy, out_dtype=BF16,
                                         name=f"c_out_dw_{l}").reshape(N_DEV, D // N_DEV, D)
            dgb, dcv_c = _c_gate_bwd1(s['gb'], s['cv'], dsg, name=f"c_gate_bwd1_{l}")
            dpc, dcw = _dwconv_bwd(s['p'], dcv_c, cconv[slot], F32, name=f"c_conv_bwd_{l}")
            gfull['c_conv_w'][slot] = _col_slabs(dcw)
            dgc, dv = _c_gate_bwd2(s['gc'], s['v'], dpc, name=f"c_gate_bwd2_{l}")
            ht = _transpose(s['h'], name=f"h_t_{l}")
            dh, parts = None, []
            for i, d_i in enumerate((dgb, dgc, dv)):
                dh = _mm(d_i, cin[slot, :, i * D:(i + 1) * D], nt=True, add=dh, name=f"c_in_dx_{i}_{l}")
                parts.append(_mm(ht, d_i, out_dtype=BF16, name=f"c_in_dw_{i}_{l}"))
            gfull['c_in_w'][slot] = _col_slabs(jnp.concatenate(parts, axis=1))
        dx, dsh_m, dsc_m, dg1 = _pre_bwd(s['x'], dh, dx1, row(p['norm_pre_mix'], l), sc_m, name=f"pre_mix_bwd_{l}")
        dmod[l] = jnp.concatenate([dsh_m, dsc_m, dgt_m, dsh_f, dsc_f, dgt_f], axis=0)
        dnorm[l] = (dg1, dg2, dg3, dg4)

    zero_row = jnp.zeros((1, D), F32)
    rows_ = [dmod[l] for l in range(L)]
    rows_ += [dnorm[l][i] for i in range(4) for l in range(L)]
    rows_ += [d_bgb if d_bgb is not None else zero_row, d_bscale if d_bscale is not None else zero_row]
    fdb_flat = jnp.concatenate([d_fdb[l] for l in range(L)])
    rows_ += [_pad_last(fdb_flat, _round_up(fdb_flat.size, D)).reshape(-1, D)]
    pack = jnp.concatenate(rows_, axis=0)
    pack = jnp.pad(pack, ((0, _round_up(pack.shape[0], 8) - pack.shape[0]), (0, 0)))
    pack_all = _exchange([pack], scatter=False, name="gather_small_grads")[0]
    red = _sum_slabs(pack_all, name="sum_small_grads")
    g_rep = {'mod_b': red[:6 * L].reshape(L, 6 * D)}
    for i, k in enumerate(('norm_pre_mix', 'norm_post_mix', 'norm_pre_ffn', 'norm_post_ffn')):
        g_rep[k] = red[6 * L + i * L:6 * L + (i + 1) * L]
    o = 6 * L + 4 * L
    g_rep['b_group_b'], g_rep['b_scale'] = red[o:o + 1], red[o + 1:o + 2]
    g_rep['f_dw_b'] = red[o + 2:].reshape(-1)[:L * 8 * ns].reshape(L, 8 * ns)

    dmod_all = pack_all[:, :6 * L].reshape(N_DEV, L, 6 * D)
    dmod_mine = jnp.moveaxis(lax.dynamic_slice_in_dim(dmod_all, me * n6, n6, axis=2), 0, 1)
    g_mod_w = _mod_bwd(c_all.T, dmod_mine, name="mod_bwd")

    stack = lambda d: jnp.stack([d[i] for i in sorted(d)], axis=1)
    big_g = [k for k in gfull if gfull[k] and k.endswith('_w') and k not in ('a_dw_w', 'c_conv_w', 'f_dw_w')]
    small_g = [k for k in gfull if gfull[k] and k not in big_g]
    full = [stack(gfull[k]) for k in big_g]
    got = _swap_cores(full, name="scatter_grads_cores")
    core = lax.axis_index("c").astype(jnp.int32).reshape(1)
    partial = [_add_core_slabs(a, b, core, name=f"add_core_slabs_{k}") for k, a, b in zip(big_g, full, got)]
    slabs = dict(zip(big_g, _swap_chips(partial, name="scatter_grads_chips")))
    slabs.update(zip(small_g, _exchange([stack(gfull[k]) for k in small_g], scatter=True, name="scatter_small_grads")))

    out = {}
    for k in WEIGHTS:
        w, m, v = p[k], p['m_' + k], p['v_' + k]
        if k in g_rep:
            g = g_rep[k].reshape((1,) + w.shape)
        elif k == 'mod_w':
            g = g_mod_w.reshape((1,) + w.shape)
        else:
            g = slabs[k]
        if k in ('f_up_w', 'f_dw_w'):
            w, m, v = (_pad_last(t, npad) for t in (w, m, v))
        shape = w.shape
        two_d = (-1, shape[-1])
        res = _adamw(w.reshape(two_d), g.reshape((g.shape[0],) + (w.size // shape[-1], shape[-1])), m.reshape(two_d),
                     v.reshape(two_d), name=f"adamw_{k}")
        res = [t.reshape(shape) for t in res]
        if k in ('f_up_w', 'f_dw_w'):
            res = [t[..., :ns] for t in res]
        out[k] = res

    grads, deltas, new_m, new_v = ([out[k][i] for k in WEIGHTS] for i in range(4))
    return (loss, dx.reshape(1, T, D), *grads, *deltas, *new_m, *new_v)
```

```python
import functools

import jax
import jax.numpy as jnp
from jax import lax
from jax.experimental import pallas as pl
from jax.experimental.pallas import tpu as pltpu

F32, BF16 = jnp.float32, jnp.bfloat16
AXES = ("x", "y", "c")
N_DEV = 8
LANE = 128
VMEM_LIMIT = 48 * 1024 * 1024
RMS_EPS, LN_EPS = 1e-6, 1e-5
POOL_WINDOWS = (2, 4, 8, 16)
N_MIXERS = 3
ADAM_LR, ADAM_B1, ADAM_B2, ADAM_EPS, ADAM_WD, ADAM_STEP = 0.001, 0.9, 0.999, 1e-08, 0.01, 10

WEIGHTS = ['mod_w', 'mod_b', 'norm_pre_mix', 'norm_post_mix', 'norm_pre_ffn', 'norm_post_ffn', 'a_pw1_w', 'a_pw1_b',
           'a_dw_w', 'a_dw_b', 'a_ln_g', 'a_ln_b', 'a_pw2_w', 'a_pw2_b', 'b_group_w', 'b_group_b', 'b_scale', 'c_in_w',
           'c_conv_w', 'c_out_w', 'f_up_w', 'f_dw_w', 'f_dw_b', 'f_down_w']
INPUTS = (['x', 'c'] + WEIGHTS + ['loss_target'] + ['m_' + w for w in WEIGHTS] + ['v_' + w for w in WEIGHTS])


def _tile(n, cap, mult):
    for t in range(min(n, cap), 0, -1):
        if n % t == 0 and t % mult == 0:
            return t
    return n


def _round_up(n, m):
    return -(-n // m) * m


def _params(*dims):
    return pltpu.CompilerParams(dimension_semantics=dims or None, vmem_limit_bytes=VMEM_LIMIT)


def _sigmoid(v):
    return jax.nn.sigmoid(v)


def _exchange(arrs, *, scatter, name):
    n = len(arrs)

    def body(*refs):
        ins, outs = refs[:n], refs[n:2 * n]
        send_sems, recv_sems, local_sems = refs[2 * n:]
        x, y, c = (lax.axis_index(a) for a in AXES)
        me = 4 * x + 2 * y + c

        def peer(k):
            px = 1 - x if (k >> 2) & 1 else x
            py = 1 - y if (k >> 1) & 1 else y
            pc = 1 - c if k & 1 else c
            return (px, py, pc), 4 * px + 2 * py + pc

        def remote(i, k, src, dst):
            return pltpu.make_async_remote_copy(
                src_ref=src, dst_ref=dst, send_sem=send_sems.at[i, k - 1], recv_sem=recv_sems.at[i, k - 1],
                device_id=peer(k)[0], device_id_type=pl.DeviceIdType.MESH)

        own = []
        for i in range(n):
            cp = pltpu.make_async_copy(ins[i].at[me] if scatter else ins[i], outs[i].at[me], local_sems.at[i])
            cp.start()
            own.append(cp)
        sent = []
        for k in range(1, N_DEV):
            for i in range(n):
                cp = remote(i, k, ins[i].at[peer(k)[1]] if scatter else ins[i], outs[i].at[me])
                cp.start()
                sent.append(cp)
        for k in range(1, N_DEV):
            for i in range(n):
                landing = outs[i].at[peer(k)[1]]
                remote(i, k, landing, landing).wait_recv()
        for cp in sent:
            cp.wait_send()
        for cp in own:
            cp.wait()

    any_spec = pl.BlockSpec(memory_space=pl.ANY)
    out_shape = [jax.ShapeDtypeStruct(a.shape if scatter else (N_DEV,) + a.shape, a.dtype) for a in arrs]
    return pl.pallas_call(
        body, name=name, out_shape=out_shape, in_specs=[any_spec] * n, out_specs=[any_spec] * n,
        scratch_shapes=[pltpu.SemaphoreType.DMA((n, N_DEV - 1)), pltpu.SemaphoreType.DMA((n, N_DEV - 1)),
                        pltpu.SemaphoreType.DMA((n,))],
    )(*arrs)


def _comm_call(body, arrs, out_shape, sem_shapes, name):
    any_spec = pl.BlockSpec(memory_space=pl.ANY)
    return pl.pallas_call(
        body, name=name, out_shape=out_shape, in_specs=[any_spec] * len(arrs), out_specs=[any_spec] * len(out_shape),
        scratch_shapes=[pltpu.SemaphoreType.DMA(s) for s in sem_shapes],
    )(*arrs)


def _gather_two_level(arrs, *, name):
    n = len(arrs)

    def body(*refs):
        ins, outs = refs[:n], refs[n:2 * n]
        send_sems, recv_sems, local_sems = refs[2 * n:]
        x, y, c = (lax.axis_index(a) for a in AXES)
        index = lambda px, py, pc: 4 * px + 2 * py + pc
        me, sibling = index(x, y, c), (x, y, 1 - c)
        chips = [(1 - x, y), (x, 1 - y), (1 - x, 1 - y)]

        def copy(i, k, block, to, src=None):
            return pltpu.make_async_remote_copy(
                src_ref=outs[i].at[block] if src is None else src, dst_ref=outs[i].at[block],
                send_sem=send_sems.at[i, k], recv_sem=recv_sems.at[i, k], device_id=to,
                device_id_type=pl.DeviceIdType.MESH)

        own = [pltpu.make_async_copy(ins[i], outs[i].at[me], local_sems.at[i]) for i in range(n)]
        first = [copy(i, 1 + j, me, (*chip, c), src=ins[i]) for j, chip in enumerate(chips) for i in range(n)]
        first += [copy(i, 0, me, sibling, src=ins[i]) for i in range(n)]
        for cp in own + first:
            cp.start()
        passed = []
        for j, chip in enumerate(chips):
            block = index(*chip, c)
            for i in range(n):
                copy(i, 1 + j, block, (x, y, c)).wait_recv()
            for i in range(n):
                cp = copy(i, 4 + j, block, sibling)
                cp.start()
                passed.append(cp)
        for i in range(n):
            copy(i, 0, index(x, y, 1 - c), (x, y, c)).wait_recv()
        for j, chip in enumerate(chips):
            for i in range(n):
                copy(i, 4 + j, index(*chip, 1 - c), (x, y, c)).wait_recv()
        for cp in first + passed:
            cp.wait_send()
        for cp in own:
            cp.wait()

    out_shape = [jax.ShapeDtypeStruct((N_DEV,) + a.shape, a.dtype) for a in arrs]
    return _comm_call(body, arrs, out_shape, [(n, 7), (n, 7), (n,)], name)


def _swap_cores(arrs, *, name):
    n = len(arrs)

    def body(*refs):
        ins, got = refs[:n], refs[n:2 * n]
        send_sems, recv_sems = refs[2 * n:]
        x, y, c = (lax.axis_index(a) for a in AXES)
        remote = [pltpu.make_async_remote_copy(
            src_ref=ins[i].at[2 * q + 1 - c], dst_ref=got[i].at[q], send_sem=send_sems.at[i, q],
            recv_sem=recv_sems.at[i, q], device_id=(x, y, 1 - c), device_id_type=pl.DeviceIdType.MESH)
            for q in range(4) for i in range(n)]
        for cp in remote:
            cp.start()
        for cp in remote:
            cp.wait_recv()
        for cp in remote:
            cp.wait_send()

    half = [jax.ShapeDtypeStruct((4,) + a.shape[1:], a.dtype) for a in arrs]
    return _comm_call(body, arrs, half, [(n, 4), (n, 4)], name)


def _swap_chips(arrs, *, name):
    n = len(arrs)

    def body(*refs):
        ins, outs = refs[:n], refs[n:2 * n]
        send_sems, recv_sems, local_sems = refs[2 * n:]
        x, y, c = (lax.axis_index(a) for a in AXES)
        mine = 2 * x + y

        def peer(k):
            px = 1 - x if k & 2 else x
            py = 1 - y if k & 1 else y
            return (px, py, c), 2 * px + py

        def remote(i, k, src, dst):
            return pltpu.make_async_remote_copy(
                src_ref=src, dst_ref=dst, send_sem=send_sems.at[i, k - 1], recv_sem=recv_sems.at[i, k - 1],
                device_id=peer(k)[0], device_id_type=pl.DeviceIdType.MESH)

        own = [pltpu.make_async_copy(ins[i].at[mine], outs[i].at[mine], local_sems.at[i]) for i in range(n)]
        sent = [remote(i, k, ins[i].at[peer(k)[1]], outs[i].at[mine]) for k in range(1, 4) for i in range(n)]
        for cp in own + sent:
            cp.start()
        for k in range(1, 4):
            for i in range(n):
                landing = outs[i].at[peer(k)[1]]
                remote(i, k, landing, landing).wait_recv()
        for cp in sent:
            cp.wait_send()
        for cp in own:
            cp.wait()

    out_shape = [jax.ShapeDtypeStruct(a.shape, a.dtype) for a in arrs]
    return _comm_call(body, arrs, out_shape, [(n, 3), (n, 3), (n,)], name)


def _add_core_slabs(full, got, core, *, name):
    C = got.shape[-1]
    R = got[0].size // C
    tr = _tile(R, max(16, (1 << 19) // C), 16)

    def body(core_ref, a_ref, b_ref, o_ref):
        o_ref[...] = (a_ref[...].astype(F32) + b_ref[...].astype(F32)).astype(o_ref.dtype)

    blk = pl.BlockSpec((None, tr, C), lambda q, i, core_ref: (q, i, 0))
    grid_spec = pltpu.PrefetchScalarGridSpec(
        num_scalar_prefetch=1, grid=(4, R // tr),
        in_specs=[pl.BlockSpec((None, None, tr, C), lambda q, i, core_ref: (q, core_ref[0], i, 0)), blk], out_specs=blk)
    return pl.pallas_call(
        body, name=name, grid_spec=grid_spec, out_shape=jax.ShapeDtypeStruct((4, R, C), got.dtype),
        compiler_params=_params("parallel", "parallel"),
    )(core, full.reshape(4, 2, R, C), got.reshape(4, R, C)).reshape(got.shape)


def _mm(a, b, *, name, nt=False, bias=None, add=None, out_dtype=F32, shard_cols=0):
    M, K = a.shape
    N = b.shape[0] if nt else b.shape[1]
    tm, tn = _tile(M, 1024, 16), shard_cols or _tile(N, 1024, LANE)

    def vmem_bytes(tk):
        blocks = 2 * (tm * tk + tk * tn) + tm * tn * jnp.dtype(out_dtype).itemsize + (4 * tm * tn if add is not None else 0)
        return 2 * blocks + (4 * tm * tn if tk < K else 0)

    tk = next((t for t in (_tile(K, 3072, LANE), _tile(K, 2048, LANE)) if vmem_bytes(t) <= VMEM_LIMIT - (12 << 20)),
              _tile(K, 1024, LANE))
    nk = K // tk
    dims = (((1,), (1,)), ((), ())) if nt else (((1,), (0,)), ((), ()))

    def body(*refs):
        refs = list(refs)
        a_ref, b_ref = refs.pop(0), refs.pop(0)
        bias_ref = refs.pop(0) if bias is not None else None
        add_ref = refs.pop(0) if add is not None else None
        o_ref = refs.pop(0)
        part = lax.dot_general(a_ref[...], b_ref[...], dims, preferred_element_type=F32)

        def finish(val):
            if bias_ref is not None:
                val = val + bias_ref[...]
            if add_ref is not None:
                val = val + add_ref[...]
            o_ref[...] = val.astype(o_ref.dtype)

        if nk == 1:
            finish(part)
        else:
            acc_ref = refs.pop(0)
            k = pl.program_id(2)

            @pl.when(k == 0)
            def _():
                acc_ref[...] = part

            @pl.when(k > 0)
            def _():
                acc_ref[...] += part

            @pl.when(k == nk - 1)
            def _():
                finish(acc_ref[...])

    in_specs = [pl.BlockSpec((tm, tk), lambda i, j, k: (i, k)),
                pl.BlockSpec((tn, tk), lambda i, j, k: (j, k)) if nt else pl.BlockSpec((tk, tn), lambda i, j, k: (k, j))]
    operands = [a, b]
    if bias is not None:
        in_specs.append(pl.BlockSpec((1, tn), lambda i, j, k: (0, j)))
        operands.append(bias)
    if add is not None:
        in_specs.append(pl.BlockSpec((tm, tn), lambda i, j, k: (i, j)))
        operands.append(add)
    if shard_cols:
        out_shape = jax.ShapeDtypeStruct((N // tn, M, tn), out_dtype)
        out_spec = pl.BlockSpec((None, tm, tn), lambda i, j, k: (j, i, 0))
    else:
        out_shape = jax.ShapeDtypeStruct((M, N), out_dtype)
        out_spec = pl.BlockSpec((tm, tn), lambda i, j, k: (i, j))
    return pl.pallas_call(
        body, name=name, grid=(M // tm, N // tn, nk), in_specs=in_specs, out_specs=out_spec, out_shape=out_shape,
        scratch_shapes=[pltpu.VMEM((tm, tn), F32)] if nk > 1 else [],
        compiler_params=_params("parallel", "parallel", "arbitrary"),
    )(*operands)


def _transpose(x, *, name):
    T, C = x.shape
    tt, tc = _tile(T, 512, LANE), _tile(C, 512, LANE)

    def body(x_ref, o_ref):
        o_ref[...] = x_ref[...].astype(F32).T.astype(BF16)

    return pl.pallas_call(
        body, name=name, grid=(T // tt, C // tc), in_specs=[pl.BlockSpec((tt, tc), lambda i, j: (i, j))],
        out_specs=pl.BlockSpec((tc, tt), lambda i, j: (j, i)), out_shape=jax.ShapeDtypeStruct((C, T), BF16),
        compiler_params=_params("parallel", "parallel"),
    )(x)


def _group_mm(a, w, *, name, nt=False, bias=None, scale=None):
    T, D = a.shape
    G, gd, _ = w.shape
    tm = _tile(T, 1024, 16)
    dims = (((1,), (1,)), ((), ())) if nt else (((1,), (0,)), ((), ()))
    fused = bias is not None

    def body(*refs):
        if fused:
            a_ref, w_ref, b_ref, s_ref, mb_ref, y_ref = refs
        else:
            a_ref, w_ref, y_ref = refs
        val = lax.dot_general(a_ref[...], w_ref[...], dims, preferred_element_type=F32)
        if fused:
            val = val + b_ref[...]
            mb_ref[...] = val
            val = val * s_ref[...]
        y_ref[...] = val

    blk = pl.BlockSpec((tm, gd), lambda i, g: (i, g))
    vec = pl.BlockSpec((1, gd), lambda i, g: (0, g))
    in_specs = [blk, pl.BlockSpec((None, gd, gd), lambda i, g: (g, 0, 0))] + ([vec, vec] if fused else [])
    td = jax.ShapeDtypeStruct((T, D), F32)
    return pl.pallas_call(
        body, name=name, grid=(T // tm, G), in_specs=in_specs, out_specs=[blk, blk] if fused else blk,
        out_shape=[td, td] if fused else td, compiler_params=_params("parallel", "parallel"),
    )(*([a, w, bias, scale] if fused else [a, w]))


def _group_dw(at, b, groups, *, name):
    D, T = at.shape
    gd = D // groups
    tk = _tile(T, 2048, LANE)

    def body(a_ref, b_ref, o_ref):
        @pl.when(pl.program_id(1) == 0)
        def _():
            o_ref[...] = jnp.zeros_like(o_ref)

        o_ref[...] += jnp.dot(a_ref[...], b_ref[...], preferred_element_type=F32)

    return pl.pallas_call(
        body, name=name, grid=(groups, T // tk),
        in_specs=[pl.BlockSpec((gd, tk), lambda g, k: (g, k)), pl.BlockSpec((tk, gd), lambda g, k: (k, g))],
        out_specs=pl.BlockSpec((None, gd, gd), lambda g, k: (g, 0, 0)),
        out_shape=jax.ShapeDtypeStruct((groups, gd, gd), F32), compiler_params=_params("parallel", "arbitrary"),
    )(at, b)


def _mod_fwd(c_all, w, b, *, name):
    L, D, n = w.shape
    B = c_all.shape[0]

    def body(c_ref, w_ref, b_ref, o_ref):
        cv = c_ref[...]
        o_ref[...] = jnp.dot(cv * _sigmoid(cv), w_ref[...], preferred_element_type=F32) + b_ref[...]

    return pl.pallas_call(
        body, name=name, grid=(L,),
        in_specs=[pl.BlockSpec((B, D), lambda l: (0, 0)), pl.BlockSpec((None, D, n), lambda l: (l, 0, 0)),
                  pl.BlockSpec((None, 1, n), lambda l: (l, 0, 0))],
        out_specs=pl.BlockSpec((None, B, n), lambda l: (l, 0, 0)), out_shape=jax.ShapeDtypeStruct((L, B, n), F32),
        compiler_params=_params("parallel"),
    )(c_all, w, b.reshape(L, 1, n))


def _mod_bwd(c_all_t, dmod, *, name):
    L, B, n = dmod.shape
    D = c_all_t.shape[0]

    def body(c_ref, d_ref, o_ref):
        cv = c_ref[...]
        o_ref[...] = jnp.dot(cv * _sigmoid(cv), d_ref[...], preferred_element_type=F32)

    return pl.pallas_call(
        body, name=name, grid=(L,),
        in_specs=[pl.BlockSpec((D, B), lambda l: (0, 0)), pl.BlockSpec((None, B, n), lambda l: (l, 0, 0))],
        out_specs=pl.BlockSpec((None, D, n), lambda l: (l, 0, 0)), out_shape=jax.ShapeDtypeStruct((L, D, n), F32),
        compiler_params=_params("parallel"),
    )(c_all_t, dmod)


def _rowwise(fn, tiles, vecs, out_dtypes, n_sums, *, name):
    T, C = tiles[0].shape
    tr = _tile(T, 512, 16)
    ch = _tile(tr, 32, 16)
    nt_, nv, no = len(tiles), len(vecs), len(out_dtypes)

    def body(*refs):
        t_refs, v_refs = refs[:nt_], refs[nt_:nt_ + nv]
        o_refs, s_refs = refs[nt_ + nv:nt_ + nv + no], refs[nt_ + nv + no:]
        vv = [v[...] for v in v_refs]
        acc = [None] * n_sums
        for r0 in range(0, tr, ch):
            rows = pl.ds(r0, ch)
            o_vals, s_vals = fn([t[rows, :] for t in t_refs], vv)
            for o, val in zip(o_refs, o_vals):
                o[rows, :] = val.astype(o.dtype)
            for i, val in enumerate(s_vals):
                part = jnp.sum(val, axis=0, keepdims=True)
                acc[i] = part if acc[i] is None else acc[i] + part
        if n_sums:
            first = pl.program_id(0) == 0

            @pl.when(first)
            def _():
                for s, val in zip(s_refs, acc):
                    s[...] = val

            @pl.when(jnp.logical_not(first))
            def _():
                for s, val in zip(s_refs, acc):
                    s[...] += val

    tile_spec = pl.BlockSpec((tr, C), lambda i: (i, 0))
    vec_spec = pl.BlockSpec((1, C), lambda i: (0, 0))
    res = pl.pallas_call(
        body, name=name, grid=(T // tr,), in_specs=[tile_spec] * nt_ + [vec_spec] * nv,
        out_specs=[tile_spec] * no + [vec_spec] * n_sums,
        out_shape=[jax.ShapeDtypeStruct((T, C), d) for d in out_dtypes] + [jax.ShapeDtypeStruct((1, C), F32)] * n_sums,
        compiler_params=_params("arbitrary"),
    )(*tiles, *vecs)
    return res


def _rms(v):
    return lax.rsqrt(jnp.mean(v * v, axis=-1, keepdims=True) + RMS_EPS)


def _prenorm_fwd(x, g, sc, sh, out_dtype, *, name):
    def fn(t, v):
        (xv,), (g_, sc_, sh_) = t, v
        return [(xv * _rms(xv)) * g_ * (1.0 + sc_) + sh_], []

    return _rowwise(fn, [x], [g, sc, sh], [out_dtype], 0, name=name)[0]


def _post_pre_fwd(x, y, gp, gate, g, sc, sh, out_dtype, *, name):
    def fn(t, v):
        (xv, yv), (gp_, gate_, g_, sc_, sh_) = t, v
        x1 = xv + gate_ * ((yv * _rms(yv)) * gp_)
        return [x1, (x1 * _rms(x1)) * g_ * (1.0 + sc_) + sh_], []

    return _rowwise(fn, [x, y], [gp, gate, g, sc, sh], [F32, out_dtype], 0, name=name)


def _post_loss(x, y, gp, gate, target, *, name):
    D = x.shape[1]

    def fn(t, v):
        (xv, yv, tv), (gp_, gate_) = t, v
        err = xv + gate_ * ((yv * _rms(yv)) * gp_) - tv
        return [err / D], [err * err]

    return _rowwise(fn, [x, y, target], [gp, gate], [F32], 1, name=name)


def _post_bwd(y, dxo, gp, gate, out_dtype, *, name):
    def fn(t, v):
        (yv, dv), (gp_, gate_) = t, v
        r = _rms(yv)
        yn = yv * r
        dyn = dv * gate_ * gp_
        dy = r * (dyn - yn * jnp.mean(dyn * yn, axis=-1, keepdims=True))
        return [dy], [dv * gate_ * yn, dv * yn * gp_, dy]

    return _rowwise(fn, [y, dxo], [gp, gate], [out_dtype], 3, name=name)


def _pre_bwd(x, dh, dres, g, sc, *, name):
    def fn(t, v):
        (xv, dhv, dr), (g_, sc_) = t, v
        dhv = dhv.astype(F32)
        r = _rms(xv)
        xn = xv * r
        dxn = dhv * g_ * (1.0 + sc_)
        dx = dr + r * (dxn - xn * jnp.mean(dxn * xn, axis=-1, keepdims=True))
        return [dx], [dhv, dhv * xn * g_, dhv * xn * (1.0 + sc_)]

    return _rowwise(fn, [x, dh, dres], [g, sc], [F32], 3, name=name)


def _a_ln_bwd(z, dq, g, b, *, name):
    def fn(t, v):
        (zv, dqv), (g_, b_) = t, v
        mu = jnp.mean(zv, axis=-1, keepdims=True)
        zc = zv - mu
        rstd = lax.rsqrt(jnp.mean(zc * zc, axis=-1, keepdims=True) + LN_EPS)
        zn = zc * rstd
        l = zn * g_ + b_
        s = _sigmoid(l)
        dl = dqv * (s * (1.0 + l * (1.0 - s)))
        dzn = dl * g_
        dz = rstd * (dzn - jnp.mean(dzn, axis=-1, keepdims=True) - zn * jnp.mean(dzn * zn, axis=-1, keepdims=True))
        return [dz], [dl * zn, dl, dz]

    return _rowwise(fn, [z, dq], [g, b], [F32], 3, name=name)


def _a_glu_bwd(ua, ug, dp, *, name):
    def fn(t, v):
        uav, ugv, dpv = t
        s = _sigmoid(ugv)
        dua = dpv * s
        dug = dpv * uav * s * (1.0 - s)
        return [dua, dug], [dua, dug]

    return _rowwise(fn, [ua, ug, dp], [], [BF16, BF16], 2, name=name)


def _b_scale_bwd(dy, mb, scale, *, name):
    def fn(t, v):
        (dyv, mbv), (s_,) = t, v
        dm = dyv * s_
        return [dm], [dyv * mbv, dm]

    return _rowwise(fn, [dy, mb], [scale], [BF16], 2, name=name)


def _c_gate_bwd1(gb, cv, ds, *, name):
    def fn(t, v):
        gbv, cvv, dsv = t
        return [dsv * cvv, dsv * gbv], []

    return _rowwise(fn, [gb, cv, ds], [], [BF16, F32], 0, name=name)


def _c_gate_bwd2(gc, v_, dp, *, name):
    def fn(t, v):
        gcv, vv, dpv = t
        return [dpv * vv, dpv * gcv], []

    return _rowwise(fn, [gc, v_, dp], [], [BF16, BF16], 0, name=name)


def _stage_causal(buf, t, halo, tt):
    @pl.when(t == 0)
    def _():
        buf[pl.ds(0, halo), :] = jnp.zeros((halo, buf.shape[1]), F32)

    @pl.when(t > 0)
    def _():
        buf[pl.ds(0, halo), :] = buf[pl.ds(tt, halo), :]


def _stage_anticausal(buf, t, halo, tt):
    @pl.when(t == 0)
    def _():
        buf[pl.ds(tt, halo), :] = jnp.zeros((halo, buf.shape[1]), F32)

    @pl.when(t > 0)
    def _():
        buf[pl.ds(tt, halo), :] = buf[pl.ds(0, halo), :]


SUBLANES = 8


def _make_shifted(buf, shifted):
    n = shifted.shape[1]
    for b in range(1, SUBLANES):
        shifted[b - 1, :, :] = buf[pl.ds(b, n), :]


def _rows_at(buf, shifted, offset, rows, cols):
    b = offset % SUBLANES
    if shifted is None or b == 0:
        return buf[pl.ds(offset, rows), cols]
    return shifted[b - 1, pl.ds(offset - b, rows), cols]


def _causal_taps(buf, w_ref, taps, halo, r0, rows, cols, shifted=None):
    acc = None
    for k in range(taps):
        term = w_ref[pl.ds(k, 1), cols] * _rows_at(buf, shifted, halo - (taps - 1) + k + r0, rows, cols)
        acc = term if acc is None else acc + term
    return acc


def _halo_tiles(T, C, halo, col_cap):
    tt = _tile(T, 512 if halo <= 8 else 256, 16)
    assert tt >= halo, (tt, halo)
    return tt, _tile(C, col_cap, LANE), _tile(tt, 128, 16)


def _col_blocks(tc):
    cb = LANE if tc % LANE == 0 else tc
    return [pl.ds(c0, cb) for c0 in range(0, tc, cb)]


def _ffn_act_fwd(ug, uv, wg, wv, bg, bv, *, name):
    T, F = ug.shape
    taps, halo = wg.shape[0], 8
    tt, tc, rc = _halo_tiles(T, F, halo, 1024)

    def body(ug_ref, uv_ref, wg_ref, wv_ref, bg_ref, bv_ref, a_ref, yg_ref, yv_ref, bufg, bufv):
        t = pl.program_id(1)
        for u_ref, buf in ((ug_ref, bufg), (uv_ref, bufv)):
            _stage_causal(buf, t, halo, tt)
            buf[pl.ds(halo, tt), :] = u_ref[...]
        for cols in _col_blocks(tc):
            for r0 in range(0, tt, rc):
                rows = pl.ds(r0, rc)
                yg = _causal_taps(bufg, wg_ref, taps, halo, r0, rc, cols) + bg_ref[:, cols]
                yv = _causal_taps(bufv, wv_ref, taps, halo, r0, rc, cols) + bv_ref[:, cols]
                a_ref[rows, cols] = (yg * _sigmoid(yg) * yv).astype(BF16)
                yg_ref[rows, cols] = yg.astype(BF16)
                yv_ref[rows, cols] = yv.astype(BF16)

    blk = pl.BlockSpec((tt, tc), lambda j, t: (t, j))
    wspec = pl.BlockSpec((taps, tc), lambda j, t: (0, j))
    vspec = pl.BlockSpec((1, tc), lambda j, t: (0, j))
    return pl.pallas_call(
        body, name=name, grid=(F // tc, T // tt), in_specs=[blk, blk, wspec, wspec, vspec, vspec],
        out_specs=[blk, blk, blk], out_shape=[jax.ShapeDtypeStruct((T, F), BF16)] * 3,
        scratch_shapes=[pltpu.VMEM((halo + tt, tc), F32), pltpu.VMEM((halo + tt, tc), F32)],
        compiler_params=_params("parallel", "arbitrary"),
    )(ug, uv, wg, wv, bg, bv)


def _ffn_act_bwd(yg, yv, da, *, name):
    T, F = yg.shape
    tt, tc = _tile(T, 512, 16), _tile(F, 1024, LANE)
    rc = _tile(tt, 128, 16)

    def body(yg_ref, yv_ref, da_ref, dg_ref, dv_ref, sg_ref, sv_ref):
        @pl.when(pl.program_id(1) == 0)
        def _():
            sg_ref[...] = jnp.zeros_like(sg_ref)
            sv_ref[...] = jnp.zeros_like(sv_ref)

        for cols in _col_blocks(tc):
            sum_g = sum_v = None
            for r0 in range(0, tt, rc):
                rows = pl.ds(r0, rc)
                g, v = yg_ref[rows, cols].astype(F32), yv_ref[rows, cols].astype(F32)
                s = _sigmoid(g)
                d = da_ref[rows, cols].astype(F32)
                dyg = d * v * (s * (1.0 + g * (1.0 - s)))
                dyv = d * (g * s)
                dg_ref[rows, cols] = dyg.astype(dg_ref.dtype)
                dv_ref[rows, cols] = dyv.astype(dv_ref.dtype)
                part_g, part_v = jnp.sum(dyg, axis=0, keepdims=True), jnp.sum(dyv, axis=0, keepdims=True)
                sum_g = part_g if sum_g is None else sum_g + part_g
                sum_v = part_v if sum_v is None else sum_v + part_v
            sg_ref[:, cols] += sum_g
            sv_ref[:, cols] += sum_v

    blk = pl.BlockSpec((tt, tc), lambda j, t: (t, j))
    vspec = pl.BlockSpec((1, tc), lambda j, t: (0, j))
    td, vd = jax.ShapeDtypeStruct((T, F), BF16), jax.ShapeDtypeStruct((1, F), F32)
    return pl.pallas_call(
        body, name=name, grid=(F // tc, T // tt), in_specs=[blk, blk, blk], out_specs=[blk, blk, vspec, vspec],
        out_shape=[td, td, vd, vd], compiler_params=_params("parallel", "arbitrary"),
    )(yg, yv, da)


def _dwconv_bwd(x, dy, w, out_dtype, *, name):
    T, C = x.shape
    taps = w.shape[0]
    halo = _round_up(taps - 1, 8)
    long_conv = taps > SUBLANES
    tt, tc, rc = _halo_tiles(T, C, halo, 512 if long_conv else 1024)
    nt_ = T // tt

    def body(x_ref, dy_ref, w_ref, dx_ref, dw_ref, buf, *scratch):
        t = pl.program_id(1)
        shifted = scratch[0] if long_conv else None

        @pl.when(t == 0)
        def _():
            dw_ref[...] = jnp.zeros_like(dw_ref)

        _stage_anticausal(buf, t, halo, tt)
        buf[pl.ds(0, tt), :] = dy_ref[...].astype(F32)
        if long_conv:
            _make_shifted(buf, shifted)
        for cols in _col_blocks(tc):
            for r0 in range(0, tt, rc):
                rows = pl.ds(r0, rc)
                xv = x_ref[rows, cols]
                acc = None
                for j in range(taps):
                    k = taps - 1 - j
                    ahead = _rows_at(buf, shifted, r0 + j, rc, cols)
                    term = w_ref[pl.ds(k, 1), cols] * ahead
                    acc = term if acc is None else acc + term
                    dw_ref[pl.ds(k, 1), cols] += jnp.sum(xv * ahead, axis=0, keepdims=True)
                dx_ref[rows, cols] = acc.astype(dx_ref.dtype)

    blk = pl.BlockSpec((tt, tc), lambda j, t: (nt_ - 1 - t, j))
    wspec = pl.BlockSpec((taps, tc), lambda j, t: (0, j))
    return pl.pallas_call(
        body, name=name, grid=(C // tc, nt_), in_specs=[blk, blk, wspec], out_specs=[blk, wspec],
        out_shape=[jax.ShapeDtypeStruct((T, C), out_dtype), jax.ShapeDtypeStruct((taps, C), F32)],
        scratch_shapes=[pltpu.VMEM((tt + halo, tc), F32)] + (
            [pltpu.VMEM((SUBLANES - 1, tt + halo - SUBLANES, tc), F32)] if long_conv else []),
        compiler_params=_params("parallel", "arbitrary"),
    )(x, dy, w)


def _a_conv_fwd(ua, ug, w, b, ln_g, ln_b, *, name):
    T, D = ua.shape
    taps = w.shape[0]
    halo = _round_up(taps - 1, 8)
    tt = _tile(T, 128, 16)
    assert tt >= halo
    rc, ch = _tile(tt, 128, 16), _tile(tt, 32, 16)

    def body(ua_ref, ug_ref, w_ref, b_ref, g_ref, lb_ref, p_ref, z_ref, q_ref, buf, shifted):
        t = pl.program_id(0)
        _stage_causal(buf, t, halo, tt)
        for r0 in range(0, tt, ch):
            rows = pl.ds(r0, ch)
            pv = ua_ref[rows, :] * _sigmoid(ug_ref[rows, :])
            p_ref[rows, :] = pv
            buf[pl.ds(halo + r0, ch), :] = pv
        _make_shifted(buf, shifted)
        for cols in _col_blocks(D):
            for r0 in range(0, tt, rc):
                z_ref[pl.ds(r0, rc), cols] = (
                    _causal_taps(buf, w_ref, taps, halo, r0, rc, cols, shifted) + b_ref[:, cols])
        for r0 in range(0, tt, ch):
            rows = pl.ds(r0, ch)
            zv = z_ref[rows, :]
            zc = zv - jnp.mean(zv, axis=-1, keepdims=True)
            l = zc * lax.rsqrt(jnp.mean(zc * zc, axis=-1, keepdims=True) + LN_EPS) * g_ref[...] + lb_ref[...]
            q_ref[rows, :] = (l * _sigmoid(l)).astype(BF16)

    blk = pl.BlockSpec((tt, D), lambda t: (t, 0))
    vec = pl.BlockSpec((1, D), lambda t: (0, 0))
    td = jax.ShapeDtypeStruct((T, D), F32)
    return pl.pallas_call(
        body, name=name, grid=(T // tt,),
        in_specs=[blk, blk, pl.BlockSpec((taps, D), lambda t: (0, 0)), vec, vec, vec], out_specs=[blk, blk, blk],
        out_shape=[td, td, jax.ShapeDtypeStruct((T, D), BF16)],
        scratch_shapes=[pltpu.VMEM((halo + tt, D), F32), pltpu.VMEM((SUBLANES - 1, halo + tt - SUBLANES, D), F32)],
        compiler_params=_params("arbitrary"),
    )(ua, ug, w, b, ln_g, ln_b)


def _c_gate_fwd(gb, gc, v, w, *, name):
    T, D = gb.shape
    taps, halo = w.shape[0], 8
    tt, tc, rc = _halo_tiles(T, D, halo, 1024)

    def body(gb_ref, gc_ref, v_ref, w_ref, p_ref, cv_ref, s_ref, buf):
        t = pl.program_id(1)
        _stage_causal(buf, t, halo, tt)
        pv = gc_ref[...] * v_ref[...]
        p_ref[...] = pv
        buf[pl.ds(halo, tt), :] = pv
        for cols in _col_blocks(tc):
            for r0 in range(0, tt, rc):
                rows = pl.ds(r0, rc)
                cv = _causal_taps(buf, w_ref, taps, halo, r0, rc, cols)
                cv_ref[rows, cols] = cv
                s_ref[rows, cols] = (gb_ref[rows, cols] * cv).astype(BF16)

    blk = pl.BlockSpec((tt, tc), lambda j, t: (t, j))
    td = jax.ShapeDtypeStruct((T, D), F32)
    return pl.pallas_call(
        body, name=name, grid=(D // tc, T // tt),
        in_specs=[blk, blk, blk, pl.BlockSpec((taps, tc), lambda j, t: (0, j))], out_specs=[blk, blk, blk],
        out_shape=[td, td, jax.ShapeDtypeStruct((T, D), BF16)],
        scratch_shapes=[pltpu.VMEM((halo + tt, tc), F32)], compiler_params=_params("parallel", "arbitrary"),
    )(gb, gc, v, w)


def _pool_count(base, r0, rows, width, window):
    pos = lax.broadcasted_iota(jnp.int32, (rows, width), 0) + (base + r0 + 1)
    return jnp.minimum(pos, window).astype(F32)


def _pool_fwd(h, *, name):
    T, D = h.shape
    halo = _round_up(max(POOL_WINDOWS), 8)
    gd = D // len(POOL_WINDOWS)
    tt = _tile(T, 256, 16)
    assert tt >= halo
    rc = _tile(tt, 128, 16)

    def body(h_ref, o_ref, buf):
        t = pl.program_id(0)
        _stage_causal(buf, t, halo, tt)
        buf[pl.ds(halo, tt), :] = h_ref[...]
        for g, window in enumerate(POOL_WINDOWS):
            for cols_in_group in _col_blocks(gd):
                cols = pl.ds(g * gd + cols_in_group.start, cols_in_group.size)
                for r0 in range(0, tt, rc):
                    acc = None
                    for j in range(window):
                        term = buf[pl.ds(halo - j + r0, rc), cols]
                        acc = term if acc is None else acc + term
                    cnt = _pool_count(t * tt, r0, rc, cols.size, window)
                    o_ref[pl.ds(r0, rc), cols] = (acc / cnt - buf[pl.ds(halo + r0, rc), cols]).astype(BF16)

    blk = pl.BlockSpec((tt, D), lambda t: (t, 0))
    return pl.pallas_call(
        body, name=name, grid=(T // tt,), in_specs=[blk], out_specs=blk, out_shape=jax.ShapeDtypeStruct((T, D), BF16),
        scratch_shapes=[pltpu.VMEM((halo + tt, D), F32)], compiler_params=_params("arbitrary"),
    )(h)


def _pool_bwd(dp, *, name):
    T, D = dp.shape
    halo = _round_up(max(POOL_WINDOWS), 8)
    gd = D // len(POOL_WINDOWS)
    tt = _tile(T, 256, 16)
    assert tt >= halo
    rc = _tile(tt, 128, 16)
    nt_ = T // tt

    def body(dp_ref, o_ref, buf):
        t = pl.program_id(0)
        base = (nt_ - 1 - t) * tt
        _stage_anticausal(buf, t, halo, tt)
        for g, window in enumerate(POOL_WINDOWS):
            for cols_in_group in _col_blocks(gd):
                cols = pl.ds(g * gd + cols_in_group.start, cols_in_group.size)
                for r0 in range(0, tt, rc):
                    rows = pl.ds(r0, rc)
                    buf[rows, cols] = dp_ref[rows, cols] / _pool_count(base, r0, rc, cols.size, window)
                for r0 in range(0, tt, rc):
                    acc = None
                    for j in range(window):
                        term = buf[pl.ds(r0 + j, rc), cols]
                        acc = term if acc is None else acc + term
                    o_ref[pl.ds(r0, rc), cols] = acc - dp_ref[pl.ds(r0, rc), cols]

    blk = pl.BlockSpec((tt, D), lambda t: (nt_ - 1 - t, 0))
    return pl.pallas_call(
        body, name=name, grid=(nt_,), in_specs=[blk], out_specs=blk, out_shape=jax.ShapeDtypeStruct((T, D), F32),
        scratch_shapes=[pltpu.VMEM((tt + halo, D), F32)], compiler_params=_params("arbitrary"),
    )(dp)


def _sum_slabs(slabs, *, name):
    S, R, C = slabs.shape

    def body(s_ref, o_ref):
        acc = s_ref[0]
        for p in range(1, S):
            acc = acc + s_ref[p]
        o_ref[...] = acc

    return pl.pallas_call(body, name=name, out_shape=jax.ShapeDtypeStruct((R, C), F32),
                          compiler_params=_params())(slabs)


def _adamw(w, slabs, m, v, *, name):
    S, R, C = slabs.shape
    sub = 16 if slabs.dtype == BF16 else 8
    tr = _tile(R, max(sub, (1 << 18) // (C * S)), sub)
    c1, c2 = 1.0 - ADAM_B1 ** ADAM_STEP, 1.0 - ADAM_B2 ** ADAM_STEP

    def body(w_ref, s_ref, m_ref, v_ref, g_ref, d_ref, nm_ref, nv_ref):
        g = s_ref[0].astype(F32)
        for p in range(1, S):
            g = g + s_ref[p].astype(F32)
        nm = ADAM_B1 * m_ref[...] + (1.0 - ADAM_B1) * g
        nv = ADAM_B2 * v_ref[...] + (1.0 - ADAM_B2) * (g * g)
        g_ref[...] = g
        nm_ref[...] = nm
        nv_ref[...] = nv
        d_ref[...] = -ADAM_LR * ((nm / c1) / (jnp.sqrt(nv / c2) + ADAM_EPS) + ADAM_WD * w_ref[...])

    blk = pl.BlockSpec((tr, C), lambda i: (i, 0))
    sd = jax.ShapeDtypeStruct((R, C), F32)
    return pl.pallas_call(
        body, name=name, grid=(R // tr,), in_specs=[blk, pl.BlockSpec((S, tr, C), lambda i: (0, i, 0)), blk, blk],
        out_specs=[blk] * 4, out_shape=[sd] * 4, compiler_params=_params("parallel"),
    )(w, slabs, m, v)


def _cols_natural(g):
    g = jnp.moveaxis(g, 0, -2)
    return g.reshape(g.shape[:-2] + (g.shape[-2] * g.shape[-1],))


def _rows_natural(g):
    g = jnp.moveaxis(g, 0, 1)
    return g.reshape((g.shape[0], g.shape[1] * g.shape[2]) + g.shape[3:])


def _col_slabs(full, n_shards=N_DEV):
    n = full.shape[-1] // n_shards
    return jnp.moveaxis(full.reshape(full.shape[:-1] + (n_shards, n)), -2, 0)


def _pad_last(a, n):
    return jnp.pad(a, [(0, 0)] * (a.ndim - 1) + [(0, n - a.shape[-1])])


def kernel(x, c, mod_w, mod_b, norm_pre_mix, norm_post_mix, norm_pre_ffn, norm_post_ffn, a_pw1_w, a_pw1_b, a_dw_w, a_dw_b, a_ln_g, a_ln_b, a_pw2_w, a_pw2_b, b_group_w, b_group_b, b_scale, c_in_w, c_conv_w, c_out_w, f_up_w, f_dw_w, f_dw_b, f_down_w, loss_target, m_mod_w, m_mod_b, m_norm_pre_mix, m_norm_post_mix, m_norm_pre_ffn, m_norm_post_ffn, m_a_pw1_w, m_a_pw1_b, m_a_dw_w, m_a_dw_b, m_a_ln_g, m_a_ln_b, m_a_pw2_w, m_a_pw2_b, m_b_group_w, m_b_group_b, m_b_scale, m_c_in_w, m_c_conv_w, m_c_out_w, m_f_up_w, m_f_dw_w, m_f_dw_b, m_f_down_w, v_mod_w, v_mod_b, v_norm_pre_mix, v_norm_post_mix, v_norm_pre_ffn, v_norm_post_ffn, v_a_pw1_w, v_a_pw1_b, v_a_dw_w, v_a_dw_b, v_a_ln_g, v_a_ln_b, v_a_pw2_w, v_a_pw2_b, v_b_group_w, v_b_group_b, v_b_scale, v_c_in_w, v_c_conv_w, v_c_out_w, v_f_up_w, v_f_dw_w, v_f_dw_b, v_f_down_w):
    p = dict(locals())
    assert list(p) == INPUTS
    x0 = p['x'][0]
    target = p['loss_target'][0]
    T, D = x0.shape
    L = p['mod_w'].shape[0]
    G = len(POOL_WINDOWS)
    gd = D // G
    ns = p['f_up_w'].shape[-1]
    npad = _round_up(ns, LANE)
    Fp = 4 * npad
    me = 4 * lax.axis_index("x") + 2 * lax.axis_index("y") + lax.axis_index("c")
    row = lambda a, i: a[i:i + 1]

    big = ['a_pw1_w', 'a_pw2_w', 'b_group_w', 'c_in_w', 'c_out_w', 'f_down_w']
    small = ['a_pw1_b', 'a_dw_w', 'a_dw_b', 'a_ln_g', 'a_ln_b', 'a_pw2_b', 'c_conv_w']
    send = [p[k].astype(BF16) for k in big] + [_pad_last(p['f_up_w'], npad).astype(BF16)]
    send += [p[k] for k in small] + [_pad_last(p['f_dw_w'], npad), p['c']]
    got = _gather_two_level(send, name="gather_weights")
    gw = dict(zip(big + ['f_up_w'] + small + ['f_dw_w', 'c'], got))

    w1 = _cols_natural(gw['a_pw1_w'])
    b1 = _cols_natural(gw['a_pw1_b'])
    a_dw_w = _cols_natural(gw['a_dw_w'])
    a_dw_b, a_ln_g, a_ln_b, b2 = (_cols_natural(gw[k]) for k in ('a_dw_b', 'a_ln_g', 'a_ln_b', 'a_pw2_b'))
    w2 = _rows_natural(gw['a_pw2_w'])
    bgw = jnp.moveaxis(gw['b_group_w'], 0, 2).reshape((-1, G, gd, gd))
    cin = _cols_natural(gw['c_in_w'])
    cconv = _cols_natural(gw['c_conv_w'])
    cout = _rows_natural(gw['c_out_w'])
    wup_g, wup_v = _cols_natural(gw['f_up_w'][:4]), _cols_natural(gw['f_up_w'][4:])
    fdw_g, fdw_v = _cols_natural(gw['f_dw_w'][:4]), _cols_natural(gw['f_dw_w'][4:])
    pad_ff = lambda a: _pad_last(a.reshape(a.shape[:-1] + (4, ns)), npad).reshape(a.shape[:-1] + (Fp,))
    fdb_g, fdb_v = pad_ff(p['f_dw_b'][:, :4 * ns]), pad_ff(p['f_dw_b'][:, 4 * ns:])
    wdown = _rows_natural(gw['f_down_w'])
    wdown = jnp.pad(wdown.reshape(L, 4, ns, D), ((0, 0), (0, 0), (0, npad - ns), (0, 0))).reshape(L, Fp, D)

    c_all = gw['c'].reshape(N_DEV, D)
    n6 = p['mod_w'].shape[-1]
    mod_part = _mod_fwd(c_all, p['mod_w'], lax.dynamic_slice_in_dim(p['mod_b'], me * n6, n6, axis=1), name="mod_fwd")
    mod_all = _exchange([mod_part], scatter=False, name="gather_mod")[0]
    mod_all = jnp.moveaxis(mod_all, 0, 2).reshape(L, N_DEV, N_DEV * n6)
    mod = lax.dynamic_index_in_dim(mod_all, me, axis=1, keepdims=False).reshape(L, 6, D)

    saved = []
    x = x0
    h = _prenorm_fwd(x, row(p['norm_pre_mix'], 0), row(mod[0], 1), row(mod[0], 0), BF16, name="prenorm_first")
    dx = loss_cols = None
    for l in range(L):
        kind, slot = l % N_MIXERS, l // N_MIXERS
        sh_m, sc_m, gt_m, sh_f, sc_f, gt_f = (row(mod[l], i) for i in range(6))
        s = dict(x=x, h=h)
        if kind == 0:
            ua = _mm(h, w1[slot, :, :D], bias=b1[slot:slot + 1, :D], name=f"a_pw1_a_{l}")
            ug = _mm(h, w1[slot, :, D:], bias=b1[slot:slot + 1, D:], name=f"a_pw1_g_{l}")
            pa, z, q = _a_conv_fwd(ua, ug, a_dw_w[slot], a_dw_b[slot:slot + 1], a_ln_g[slot:slot + 1],
                                   a_ln_b[slot:slot + 1], name=f"a_conv_fwd_{l}")
            y = _mm(q, w2[slot], bias=b2[slot:slot + 1], name=f"a_pw2_{l}")
            s.update(ua=ua, ug=ug, p=pa, z=z, q=q)
        elif kind == 1:
            pooled = _pool_fwd(h, name=f"pool_fwd_{l}")
            mb, y = _group_mm(pooled, bgw[slot], bias=p['b_group_b'][slot:slot + 1],
                              scale=p['b_scale'][slot:slot + 1], name=f"b_mix_{l}")
            s.update(pooled=pooled, mb=mb)
        else:
            gb, gc, v = (_mm(h, cin[slot, :, i * D:(i + 1) * D], name=f"c_in_{i}_{l}") for i in range(3))
            pc, cv, sg = _c_gate_fwd(gb, gc, v, cconv[slot], name=f"c_gate_fwd_{l}")
            y = _mm(sg, cout[slot], name=f"c_out_{l}")
            s.update(gb=gb, gc=gc, v=v, p=pc, cv=cv, sg=sg)
        x1, h2 = _post_pre_fwd(x, y, row(p['norm_post_mix'], l), gt_m, row(p['norm_pre_ffn'], l), sc_f, sh_f, BF16,
                               name=f"post_mix_{l}")
        fug = _mm(h2, wup_g[l], name=f"f_up_g_{l}")
        fuv = _mm(h2, wup_v[l], name=f"f_up_v_{l}")
        act, fyg, fyv = _ffn_act_fwd(fug, fuv, fdw_g[l], fdw_v[l], fdb_g[l:l + 1], fdb_v[l:l + 1],
                                     name=f"ffn_act_fwd_{l}")
        y2 = _mm(act, wdown[l], name=f"f_down_{l}")
        s.update(y=y, x1=x1, h2=h2, fug=fug, fuv=fuv, fyg=fyg, fyv=fyv, act=act, y2=y2)
        saved.append(s)
        if l + 1 < L:
            nxt_dtype = F32 if (l + 1) % N_MIXERS == 1 else BF16
            x, h = _post_pre_fwd(x1, y2, row(p['norm_post_ffn'], l), gt_f, row(p['norm_pre_mix'], l + 1),
                                 row(mod[l + 1], 1), row(mod[l + 1], 0), nxt_dtype, name=f"post_ffn_{l}")
        else:
            dx, loss_cols = _post_loss(x1, y2, row(p['norm_post_ffn'], l), gt_f, target, name="post_loss")
    loss = lax.psum(0.5 * jnp.sum(loss_cols) / D, AXES)

    gfull = {}
    for k in ('a_pw1_w', 'a_pw1_b', 'a_dw_w', 'a_dw_b', 'a_ln_g', 'a_ln_b', 'a_pw2_w', 'a_pw2_b', 'b_group_w',
              'c_in_w', 'c_conv_w', 'c_out_w', 'f_up_w', 'f_dw_w', 'f_down_w'):
        gfull[k] = {}
    dmod, dnorm, d_fdb = {}, {}, {}
    d_bgb = d_bscale = None
    for l in reversed(range(L)):
        kind, slot = l % N_MIXERS, l // N_MIXERS
        s = saved[l]
        sh_m, sc_m, gt_m, sh_f, sc_f, gt_f = (row(mod[l], i) for i in range(6))
        dy2, dg4, dgt_f, _ = _post_bwd(s['y2'], dx, row(p['norm_post_ffn'], l), gt_f, BF16, name=f"post_ffn_bwd_{l}")
        da = _mm(dy2, wdown[l], nt=True, out_dtype=BF16, name=f"f_down_dx_{l}")
        dwd = _mm(_transpose(dy2, name=f"dy2_t_{l}"), s['act'], out_dtype=BF16, name=f"f_down_dw_{l}").T
        gfull['f_down_w'][l] = dwd.reshape(4, npad, D)[:, :ns].reshape(N_DEV, ns // 2, D)
        dcg, dcv, dbg, dbv = _ffn_act_bwd(s['fyg'], s['fyv'], da, name=f"ffn_act_bwd_{l}")
        dug, dwg = _dwconv_bwd(s['fug'], dcg, fdw_g[l], BF16, name=f"f_dw_bwd_g_{l}")
        duv, dwv = _dwconv_bwd(s['fuv'], dcv, fdw_v[l], BF16, name=f"f_dw_bwd_v_{l}")
        dh2 = _mm(dug, wup_g[l], nt=True, name=f"f_up_dx_g_{l}")
        dh2 = _mm(duv, wup_v[l], nt=True, add=dh2, out_dtype=BF16, name=f"f_up_dx_v_{l}")
        h2t = _transpose(s['h2'], name=f"h2_t_{l}")
        gfull['f_up_w'][l] = jnp.concatenate(
            [_mm(h2t, dug, shard_cols=npad, out_dtype=BF16, name=f"f_up_dw_g_{l}"),
             _mm(h2t, duv, shard_cols=npad, out_dtype=BF16, name=f"f_up_dw_v_{l}")], axis=0)
        gfull['f_dw_w'][l] = jnp.concatenate([_col_slabs(dwg, 4), _col_slabs(dwv, 4)], axis=0)
        unpad = lambda a: a.reshape(4, npad)[:, :ns].reshape(4 * ns)
        d_fdb[l] = jnp.concatenate([unpad(dbg), unpad(dbv)])
        dx1, dsh_f, dsc_f, dg3 = _pre_bwd(s['x1'], dh2, dx, row(p['norm_pre_ffn'], l), sc_f, name=f"pre_ffn_bwd_{l}")
        dy_dtype = F32 if kind == 1 else BF16
        dy, dg2, dgt_m, dy_sum = _post_bwd(s['y'], dx1, row(p['norm_post_mix'], l), gt_m, dy_dtype,
                                           name=f"post_mix_bwd_{l}")
        if kind == 0:
            dq = _mm(dy, w2[slot], nt=True, name=f"a_pw2_dx_{l}")
            gfull['a_pw2_w'][slot] = _mm(_transpose(s['q'], name=f"q_t_{l}"), dy, out_dtype=BF16,
                                         name=f"a_pw2_dw_{l}").reshape(N_DEV, D // N_DEV, D)
            gfull['a_pw2_b'][slot] = dy_sum.reshape(N_DEV, D // N_DEV)
            dz, dlg, dlb, dzs = _a_ln_bwd(s['z'], dq, a_ln_g[slot:slot + 1], a_ln_b[slot:slot + 1], name=f"a_ln_bwd_{l}")
            gfull['a_ln_g'][slot], gfull['a_ln_b'][slot], gfull['a_dw_b'][slot] = (
                t.reshape(N_DEV, D // N_DEV) for t in (dlg, dlb, dzs))
            dp, ddw = _dwconv_bwd(s['p'], dz, a_dw_w[slot], F32, name=f"a_dw_bwd_{l}")
            gfull['a_dw_w'][slot] = _col_slabs(ddw)
            dua, dug_a, sua, sug = _a_glu_bwd(s['ua'], s['ug'], dp, name=f"a_glu_bwd_{l}")
            gfull['a_pw1_b'][slot] = jnp.concatenate([sua, sug], axis=1).reshape(N_DEV, 2 * D // N_DEV)
            dh = _mm(dua, w1[slot, :, :D], nt=True, name=f"a_pw1_dx_a_{l}")
            dh = _mm(dug_a, w1[slot, :, D:], nt=True, add=dh, name=f"a_pw1_dx_g_{l}")
            ht = _transpose(s['h'], name=f"h_t_{l}")
            n1 = 2 * D // N_DEV
            gfull['a_pw1_w'][slot] = jnp.concatenate(
                [_mm(ht, dua, shard_cols=n1, out_dtype=BF16, name=f"a_pw1_dw_a_{l}"),
                 _mm(ht, dug_a, shard_cols=n1, out_dtype=BF16, name=f"a_pw1_dw_g_{l}")], axis=0)
        elif kind == 1:
            dm, d_bscale, d_bgb = _b_scale_bwd(dy, s['mb'], p['b_scale'][slot:slot + 1], name=f"b_scale_bwd_{l}")
            dpool = _group_mm(dm, bgw[slot], nt=True, name=f"b_mix_dx_{l}")
            dgw = _group_dw(_transpose(s['pooled'], name=f"pooled_t_{l}"), dm, G, name=f"b_mix_dw_{l}")
            gfull['b_group_w'][slot] = jnp.moveaxis(dgw.astype(BF16).reshape(G, N_DEV, gd // N_DEV, gd), 1, 0)
            dh = _pool_bwd(dpool, name=f"pool_bwd_{l}")
        else:
            dsg = _mm(dy, cout[slot], nt=True, name=f"c_out_dx_{l}")
            gfull['c_out_w'][slot] = _mm(_transpose(s['sg'], name=f"sg_t_{l}"), dy, out_dtype=BF16,
                                         name=f"c_out_dw_{l}").reshape(N_DEV, D // N_DEV, D)
            dgb, dcv_c = _c_gate_bwd1(s['gb'], s['cv'], dsg, name=f"c_gate_bwd1_{l}")
            dpc, dcw = _dwconv_bwd(s['p'], dcv_c, cconv[slot], F32, name=f"c_conv_bwd_{l}")
            gfull['c_conv_w'][slot] = _col_slabs(dcw)
            dgc, dv = _c_gate_bwd2(s['gc'], s['v'], dpc, name=f"c_gate_bwd2_{l}")
            ht = _transpose(s['h'], name=f"h_t_{l}")
            dh, parts = None, []
            for i, d_i in enumerate((dgb, dgc, dv)):
                dh = _mm(d_i, cin[slot, :, i * D:(i + 1) * D], nt=True, add=dh, name=f"c_in_dx_{i}_{l}")
                parts.append(_mm(ht, d_i, out_dtype=BF16, name=f"c_in_dw_{i}_{l}"))
            gfull['c_in_w'][slot] = _col_slabs(jnp.concatenate(parts, axis=1))
        dx, dsh_m, dsc_m, dg1 = _pre_bwd(s['x'], dh, dx1, row(p['norm_pre_mix'], l), sc_m, name=f"pre_mix_bwd_{l}")
        dmod[l] = jnp.concatenate([dsh_m, dsc_m, dgt_m, dsh_f, dsc_f, dgt_f], axis=0)
        dnorm[l] = (dg1, dg2, dg3, dg4)

    zero_row = jnp.zeros((1, D), F32)
    rows_ = [dmod[l] for l in range(L)]
    rows_ += [dnorm[l][i] for i in range(4) for l in range(L)]
    rows_ += [d_bgb if d_bgb is not None else zero_row, d_bscale if d_bscale is not None else zero_row]
    fdb_flat = jnp.concatenate([d_fdb[l] for l in range(L)])
    rows_ += [_pad_last(fdb_flat, _round_up(fdb_flat.size, D)).reshape(-1, D)]
    pack = jnp.concatenate(rows_, axis=0)
    pack = jnp.pad(pack, ((0, _round_up(pack.shape[0], 8) - pack.shape[0]), (0, 0)))
    pack_all = _exchange([pack], scatter=False, name="gather_small_grads")[0]
    red = _sum_slabs(pack_all, name="sum_small_grads")
    g_rep = {'mod_b': red[:6 * L].reshape(L, 6 * D)}
    for i, k in enumerate(('norm_pre_mix', 'norm_post_mix', 'norm_pre_ffn', 'norm_post_ffn')):
        g_rep[k] = red[6 * L + i * L:6 * L + (i + 1) * L]
    o = 6 * L + 4 * L
    g_rep['b_group_b'], g_rep['b_scale'] = red[o:o + 1], red[o + 1:o + 2]
    g_rep['f_dw_b'] = red[o + 2:].reshape(-1)[:L * 8 * ns].reshape(L, 8 * ns)

    dmod_all = pack_all[:, :6 * L].reshape(N_DEV, L, 6 * D)
    dmod_mine = jnp.moveaxis(lax.dynamic_slice_in_dim(dmod_all, me * n6, n6, axis=2), 0, 1)
    g_mod_w = _mod_bwd(c_all.T, dmod_mine, name="mod_bwd")

    stack = lambda d: jnp.stack([d[i] for i in sorted(d)], axis=1)
    big_g = [k for k in gfull if gfull[k] and k.endswith('_w') and k not in ('a_dw_w', 'c_conv_w', 'f_dw_w')]
    small_g = [k for k in gfull if gfull[k] and k not in big_g]
    full = [stack(gfull[k]) for k in big_g]
    got = _swap_cores(full, name="scatter_grads_cores")
    core = lax.axis_index("c").astype(jnp.int32).reshape(1)
    partial = [_add_core_slabs(a, b, core, name=f"add_core_slabs_{k}") for k, a, b in zip(big_g, full, got)]
    slabs = dict(zip(big_g, _swap_chips(partial, name="scatter_grads_chips")))
    slabs.update(zip(small_g, _exchange([stack(gfull[k]) for k in small_g], scatter=True, name="scatter_small_grads")))

    out = {}
    for k in WEIGHTS:
        w, m, v = p[k], p['m_' + k], p['v_' + k]
        if k in g_rep:
            g = g_rep[k].reshape((1,) + w.shape)
        elif k == 'mod_w':
            g = g_mod_w.reshape((1,) + w.shape)
        else:
            g = slabs[k]
        if k in ('f_up_w', 'f_dw_w'):
            w, m, v = (_pad_last(t, npad) for t in (w, m, v))
        shape = w.shape
        two_d = (-1, shape[-1])
        res = _adamw(w.reshape(two_d), g.reshape((g.shape[0],) + (w.size // shape[-1], shape[-1])), m.reshape(two_d),
                     v.reshape(two_d), name=f"adamw_{k}")
        res = [t.reshape(shape) for t in res]
        if k in ('f_up_w', 'f_dw_w'):
            res = [t[..., :ns] for t in res]
        out[k] = res

    grads, deltas, new_m, new_v = ([out[k][i] for k in WEIGHTS] for i in range(4))
    return (loss, dx.reshape(1, T, D), *grads, *deltas, *new_m, *new_v)
```
